```python
import jax, jax.numpy as jnp
from jax import lax
import numpy as np

D_MODEL = 1024
BATCH = 4
SEQ = 8192
DEPTH = 1
DEC_BATCH = 128
DEC_SEQ = 4
PAST_LEN = 16384
PAGE_SIZE = 128

MIX_WIDTH = D_MODEL
LRU_WIDTH = MIX_WIDTH // 2
LRU_BLOCKS = 8
LRU_BLOCK_W = LRU_WIDTH // LRU_BLOCKS
LRU_C = 8.0
CONV_W = 4
HEAD_DIM = 64
N_HEADS = (MIX_WIDTH - LRU_WIDTH) // HEAD_DIM
N_KV_HEADS = 2
GROUP = N_HEADS // N_KV_HEADS
ATTN_WIDTH = N_HEADS * HEAD_DIM
KV_WIDTH = N_KV_HEADS * HEAD_DIM
WINDOW = 128
ATTN_BLOCK = WINDOW
ROPE_THETA = 10000.0
D_FF = 2816
NORM_EPS = 1e-6
IN_WIDTH = 2 * LRU_WIDTH + ATTN_WIDTH + 2 * KV_WIDTH
SPLITS = [LRU_WIDTH, 2 * LRU_WIDTH, 2 * LRU_WIDTH + ATTN_WIDTH, 2 * LRU_WIDTH + ATTN_WIDTH + KV_WIDTH]

kernel_name = 'hymba_rglru_swa_sink_macaron_step'


def rms_norm(x, g):
    xf = x.astype(jnp.float32)
    y = xf * lax.rsqrt(jnp.mean(xf * xf, axis=-1, keepdims=True) + NORM_EPS)
    return (y * g.astype(jnp.float32)).astype(x.dtype)


def swiglu_ffn(x, w_gate, w_up, w_down):
    return (jax.nn.silu(x @ w_gate) * (x @ w_up)) @ w_down


def rope(x, pos):
    half = HEAD_DIM // 2
    inv_freq = ROPE_THETA ** (-jnp.arange(half, dtype=jnp.float32) / half)
    ang = pos.astype(jnp.float32)[:, None] * inv_freq[None, :]
    cos = jnp.cos(ang)[:, None, :]
    sin = jnp.sin(ang)[:, None, :]
    xf = x.astype(jnp.float32)
    x1, x2 = xf[..., :half], xf[..., half:]
    return jnp.concatenate([x1 * cos - x2 * sin, x2 * cos + x1 * sin], axis=-1).astype(x.dtype)


def causal_conv(x, buf, w, b):
    T = x.shape[1]
    xc = jnp.concatenate([buf.astype(x.dtype), x], axis=1)
    y = b
    for j in range(CONV_W):
        y = y + xc[:, j:j + T] * w[j]
    return y, xc[:, xc.shape[1] - (CONV_W - 1):]


def _lin_combine(left, right):
    a1, b1 = left
    a2, b2 = right
    return a1 * a2, a2 * b1 + b2


def rg_lru(x, h0, wa, ba, wx, bx, lam):
    B, T, C = x.shape
    xb = x.reshape(B, T, LRU_BLOCKS, LRU_BLOCK_W)
    r = jax.nn.sigmoid(jnp.einsum('btnc,ncd->btnd', xb, wa).reshape(B, T, C) + ba).astype(jnp.float32)
    i = jax.nn.sigmoid(jnp.einsum('btnc,ncd->btnd', xb, wx).reshape(B, T, C) + bx).astype(jnp.float32)
    log_a = -LRU_C * jax.nn.softplus(-lam.astype(jnp.float32)) * r
    a = jnp.exp(log_a)
    u = jnp.sqrt(-jnp.expm1(2.0 * log_a)) * (i * x.astype(jnp.float32))
    u = u.at[:, 0].add(a[:, 0] * h0.astype(jnp.float32))
    _, h = lax.associative_scan(_lin_combine, (a, u), axis=1)
    return h, h[:, -1]


def sink_attention(q, k, v, mask, sinks):
    s = jnp.einsum('...qkgd,...jkd->...kgqj', q, k, preferred_element_type=jnp.float32) * (HEAD_DIM ** -0.5)
    s = jnp.where(mask[..., None, None, :, :], s, -jnp.inf)
    sk = sinks.astype(jnp.float32).reshape(N_KV_HEADS, GROUP, 1, 1)
    mx = jnp.maximum(jnp.max(s, axis=-1, keepdims=True), sk)
    p = jnp.exp(s - mx)
    p = p / (jnp.sum(p, axis=-1, keepdims=True) + jnp.exp(sk - mx))
    o = jnp.einsum('...kgqj,...jkd->...qkgd', p, v.astype(jnp.float32))
    return o.astype(q.dtype)


def swa_banded(q, k, v, sinks):
    B, S = q.shape[0], q.shape[1]
    nb = S // ATTN_BLOCK
    qb = q.reshape(B, nb, ATTN_BLOCK, N_KV_HEADS, GROUP, HEAD_DIM)

    def band(t):
        cur = t.reshape(B, nb, ATTN_BLOCK, N_KV_HEADS, HEAD_DIM)
        prev = jnp.concatenate([jnp.zeros_like(cur[:, :1]), cur[:, :-1]], axis=1)
        return jnp.concatenate([prev, cur], axis=2)

    kb, vb = band(k), band(v)
    qi = jnp.arange(ATTN_BLOCK)[:, None]
    kj = jnp.arange(2 * ATTN_BLOCK)[None, :]
    d = qi + ATTN_BLOCK - kj
    blk = jnp.arange(nb)[:, None, None]
    mask = (d >= 0) & (d < WINDOW) & ((blk > 0) | (kj >= ATTN_BLOCK))
    o = sink_attention(qb, kb, vb, mask, sinks)
    return o.reshape(B, S, N_HEADS, HEAD_DIM)


def swa_cached(q, k, v, k_cache, v_cache, sinks):
    B, T = q.shape[0], q.shape[1]
    W = k_cache.shape[1]
    kk = jnp.concatenate([k_cache.astype(k.dtype), k], axis=1)
    vv = jnp.concatenate([v_cache.astype(v.dtype), v], axis=1)
    d = (jnp.arange(T)[:, None] + W) - jnp.arange(W + T)[None, :]
    mask = (d >= 0) & (d < WINDOW)
    qg = q.reshape(B, T, N_KV_HEADS, GROUP, HEAD_DIM)
    o = sink_attention(qg, kk, vv, mask, sinks)
    return o.reshape(B, T, N_HEADS, HEAD_DIM), kk[:, T:], vv[:, T:]


def token_mixer(hn, pos, h0, conv_buf, k_cache, v_cache, win, mp):
    (w_in, conv_w, conv_b, wa, ba, wx, bx, lam, qn, kn, sinks, lru_on, attn_on, w_out) = mp
    B, T, _ = hn.shape
    proj = hn @ w_in
    x_lru, gate, q, k, v = jnp.split(proj, SPLITS, axis=-1)
    xc, new_conv = causal_conv(x_lru, conv_buf, conv_w, conv_b)
    h, h_last = rg_lru(xc, h0, wa, ba, wx, bx, lam)
    lru_out = rms_norm((h * jax.nn.gelu(gate.astype(jnp.float32))).astype(hn.dtype), lru_on)
    q = rope(rms_norm(q.reshape(B, T, N_HEADS, HEAD_DIM), qn), pos)
    k = rope(rms_norm(k.reshape(B, T, N_KV_HEADS, HEAD_DIM), kn), pos)
    v = v.reshape(B, T, N_KV_HEADS, HEAD_DIM)
    if k_cache is None:
        o = swa_banded(q, k, v, sinks)
        new_k, new_v = k[:, T - win:], v[:, T - win:]
    else:
        o, new_k, new_v = swa_cached(q, k, v, k_cache, v_cache, sinks)
    attn_out = rms_norm(o.reshape(B, T, ATTN_WIDTH), attn_on)
    y = jnp.concatenate([lru_out, attn_out], axis=-1) @ w_out
    return y, h_last.astype(hn.dtype), new_conv, new_k, new_v


def decoder_layer(x, pos, h0, conv_buf, k_cache, v_cache, win, ffn1, mix, ffn2):
    x = x + 0.5 * swiglu_ffn(rms_norm(x, ffn1[0]), ffn1[1], ffn1[2], ffn1[3])
    y, h_last, new_conv, new_k, new_v = token_mixer(rms_norm(x, mix[0]), pos, h0, conv_buf,
                                                    k_cache, v_cache, win, mix[1:])
    x = x + y
    x = x + 0.5 * swiglu_ffn(rms_norm(x, ffn2[0]), ffn2[1], ffn2[2], ffn2[3])
    return x, h_last, new_conv, new_k, new_v


def setup_inputs(seed: int = 0) -> dict:
    key = jax.random.key(seed)
    ks = jax.random.split(key, 32)
    f32 = jnp.float32
    L = DEPTH
    win = min(WINDOW, PAST_LEN)

    def nrm(k, shape, s):
        return jax.random.normal(k, shape, f32) * s

    def gain(k, shape):
        return 1.0 + 0.01 * jax.random.normal(k, shape, f32)

    u = jax.random.uniform(ks[18], (L, LRU_WIDTH), f32, 0.9, 0.999)
    sg = u ** (1.0 / LRU_C)
    lam = jnp.log(sg) - jnp.log1p(-sg)
    return {
        'x_prompt': nrm(ks[0], (BATCH, SEQ, D_MODEL), 1.0),
        'x_sample': nrm(ks[1], (DEC_BATCH, DEC_SEQ, D_MODEL), 1.0),
        'state_lru_h': nrm(ks[2], (L, DEC_BATCH, LRU_WIDTH), 0.5),
        'state_conv': nrm(ks[3], (L, DEC_BATCH, CONV_W - 1, LRU_WIDTH), 1.0),
        'cache_k': nrm(ks[4], (L, DEC_BATCH, win, N_KV_HEADS, HEAD_DIM), 1.0),
        'cache_v': nrm(ks[5], (L, DEC_BATCH, win, N_KV_HEADS, HEAD_DIM), 1.0),
        'ffn1_norm': gain(ks[6], (L, D_MODEL)),
        'ffn1_w_gate': nrm(ks[7], (L, D_MODEL, D_FF), D_MODEL ** -0.5),
        'ffn1_w_up': nrm(ks[8], (L, D_MODEL, D_FF), D_MODEL ** -0.5),
        'ffn1_w_down': nrm(ks[9], (L, D_FF, D_MODEL), D_FF ** -0.5),
        'mix_norm': gain(ks[10], (L, D_MODEL)),
        'w_in': nrm(ks[11], (L, D_MODEL, IN_WIDTH), D_MODEL ** -0.5),
        'conv_w': nrm(ks[12], (L, CONV_W, LRU_WIDTH), CONV_W ** -0.5),
        'conv_b': nrm(ks[13], (L, LRU_WIDTH), 0.01),
        'lru_wa': nrm(ks[14], (L, LRU_BLOCKS, LRU_BLOCK_W, LRU_BLOCK_W), LRU_BLOCK_W ** -0.5),
        'lru_ba': nrm(ks[15], (L, LRU_WIDTH), 0.01),
        'lru_wx': nrm(ks[16], (L, LRU_BLOCKS, LRU_BLOCK_W, LRU_BLOCK_W), LRU_BLOCK_W ** -0.5),
        'lru_bx': nrm(ks[17], (L, LRU_WIDTH), 0.01),
        'lru_lambda': lam,
        'q_norm': gain(ks[19], (L, HEAD_DIM)),
        'k_norm': gain(ks[20], (L, HEAD_DIM)),
        'attn_sinks': nrm(ks[21], (L, N_HEADS), 0.5),
        'lru_out_norm': gain(ks[22], (L, LRU_WIDTH)),
        'attn_out_norm': gain(ks[23], (L, ATTN_WIDTH)),
        'w_out': nrm(ks[24], (L, MIX_WIDTH, D_MODEL), MIX_WIDTH ** -0.5),
        'ffn2_norm': gain(ks[25], (L, D_MODEL)),
        'ffn2_w_gate': nrm(ks[26], (L, D_MODEL, D_FF), D_MODEL ** -0.5),
        'ffn2_w_up': nrm(ks[27], (L, D_MODEL, D_FF), D_MODEL ** -0.5),
        'ffn2_w_down': nrm(ks[28], (L, D_FF, D_MODEL), D_FF ** -0.5),
    }


def reference(x_prompt, x_sample, state_lru_h, state_conv, cache_k, cache_v,
              ffn1_norm, ffn1_w_gate, ffn1_w_up, ffn1_w_down,
              mix_norm, w_in, conv_w, conv_b, lru_wa, lru_ba, lru_wx, lru_bx, lru_lambda,
              q_norm, k_norm, attn_sinks, lru_out_norm, attn_out_norm, w_out,
              ffn2_norm, ffn2_w_gate, ffn2_w_up, ffn2_w_down):
    win = min(WINDOW, PAST_LEN)
    pos_p = jnp.arange(SEQ, dtype=jnp.int32)
    pos_s = PAST_LEN + jnp.arange(DEC_SEQ, dtype=jnp.int32)
    yp, ys = x_prompt, x_sample
    p_h, p_c, p_k, p_v = [], [], [], []
    s_h, s_c, s_k, s_v = [], [], [], []
    for l in range(DEPTH):
        ffn1 = (ffn1_norm[l], ffn1_w_gate[l], ffn1_w_up[l], ffn1_w_down[l])
        mix = (mix_norm[l], w_in[l], conv_w[l], conv_b[l], lru_wa[l], lru_ba[l], lru_wx[l], lru_bx[l],
               lru_lambda[l], q_norm[l], k_norm[l], attn_sinks[l], lru_out_norm[l], attn_out_norm[l], w_out[l])
        ffn2 = (ffn2_norm[l], ffn2_w_gate[l], ffn2_w_up[l], ffn2_w_down[l])
        h0 = jnp.zeros((BATCH, LRU_WIDTH), x_prompt.dtype)
        c0 = jnp.zeros((BATCH, CONV_W - 1, LRU_WIDTH), x_prompt.dtype)
        yp, h, c, k, v = decoder_layer(yp, pos_p, h0, c0, None, None, win, ffn1, mix, ffn2)
        p_h.append(h); p_c.append(c); p_k.append(k); p_v.append(v)
        ys, h, c, k, v = decoder_layer(ys, pos_s, state_lru_h[l], state_conv[l], cache_k[l], cache_v[l],
                                       win, ffn1, mix, ffn2)
        s_h.append(h); s_c.append(c); s_k.append(k); s_v.append(v)
    prompt_lru_h, prompt_conv = jnp.stack(p_h), jnp.stack(p_c)
    prompt_k, prompt_v = jnp.stack(p_k), jnp.stack(p_v)
    sample_lru_h, sample_conv = jnp.stack(s_h), jnp.stack(s_c)
    sample_k, sample_v = jnp.stack(s_k), jnp.stack(s_v)
    return (yp, ys, prompt_lru_h, prompt_conv, prompt_k, prompt_v, sample_lru_h, sample_conv, sample_k, sample_v)
```

```python
import functools

import jax
import jax.numpy as jnp
from jax import lax
from jax.experimental import pallas as pl
from jax.experimental.pallas import tpu as pltpu

F32 = jnp.float32
BF16 = jnp.bfloat16

NORM_EPS = 1e-6
LRU_C = 8.0
CONV_W = 4
HEAD_DIM = 64
N_KV_HEADS = 2
WINDOW = 128
ROPE_THETA = 10000.0
PAST_LEN = 16384

V7X_LANES = 128
V7X_SUBLANES = 8
V7X_MXU_DIM = 256
V7X_VMEM_BYTES = 64 * 1024 * 1024

TOKEN_TILE = 512
TIME_TILE = 512
SAMPLE_SEQ_TILE = 16
FF_CHUNK = V7X_MXU_DIM


def _vmem_limit(nbytes):
    return int(min(nbytes, V7X_VMEM_BYTES - 4 * 1024 * 1024))


def _resident(shape):
    nd = len(shape)
    return pl.BlockSpec(shape, lambda *_: (0,) * nd, pipeline_mode=pl.Buffered(1))


def _rms(x, g):
    ms = jnp.mean(x * x, axis=-1, keepdims=True)
    return (x * lax.rsqrt(ms + NORM_EPS)) * g


def _ffn_accumulate(xn_ref, wg_ref, wu_ref, wd_ref, acc_ref):
    acc_ref[...] = jnp.zeros_like(acc_ref)

    def chunk(c, carry):
        xn = xn_ref[...]
        g = jnp.dot(xn, wg_ref[c], preferred_element_type=F32)
        u = jnp.dot(xn, wu_ref[c], preferred_element_type=F32)
        a = (jax.nn.silu(g) * u).astype(BF16)
        acc_ref[...] += jnp.dot(a, wd_ref[c], preferred_element_type=F32)
        return carry

    lax.fori_loop(0, wg_ref.shape[0], chunk, 0)


def _ffn_in_kernel(x_ref, g1_ref, wg_ref, wu_ref, wd_ref, gm_ref, winl_ref, wina_ref,
                   x1_ref, pl_ref, pa_ref, xn_ref, acc_ref):
    xn_ref[...] = _rms(x_ref[...], g1_ref[...]).astype(BF16)
    _ffn_accumulate(xn_ref, wg_ref, wu_ref, wd_ref, acc_ref)
    x1 = x_ref[...] + 0.5 * acc_ref[...]
    x1_ref[...] = x1
    xn_ref[...] = _rms(x1, gm_ref[...]).astype(BF16)
    pl_ref[...] = jnp.dot(xn_ref[...], winl_ref[...], preferred_element_type=F32)
    pa_ref[...] = jnp.dot(xn_ref[...], wina_ref[...], preferred_element_type=F32)


def _ffn_in(x, g1, wg, wu, wd, gm, winl, wina):
    n, d = x.shape
    tm = min(TOKEN_TILE, n)
    wl, wa = winl.shape[1], wina.shape[1]
    row = lambda w: pl.BlockSpec((tm, w), lambda i: (i, 0))
    weights = 2 * (wg.size + wu.size + wd.size + winl.size + wina.size)
    tiles = 2 * 4 * tm * (2 * d + wl + wa) + tm * d * (2 + 4) + 4 * 4 * tm * max(FF_CHUNK, wl)
    return pl.pallas_call(
        _ffn_in_kernel,
        grid=(n // tm,),
        in_specs=[row(d), _resident(g1.shape), _resident(wg.shape), _resident(wu.shape), _resident(wd.shape),
                  _resident(gm.shape), _resident(winl.shape), _resident(wina.shape)],
        out_specs=[row(d), row(wl), row(wa)],
        out_shape=[jax.ShapeDtypeStruct((n, d), F32), jax.ShapeDtypeStruct((n, wl), F32),
                   jax.ShapeDtypeStruct((n, wa), F32)],
        scratch_shapes=[pltpu.VMEM((tm, d), BF16), pltpu.VMEM((tm, d), F32)],
        compiler_params=pltpu.CompilerParams(dimension_semantics=("arbitrary",),
                                             vmem_limit_bytes=_vmem_limit(weights + tiles + (8 << 20))),
        name="ffn_in",
    )(x, g1, wg, wu, wd, gm, winl, wina)


def _out_ffn_kernel(lo_ref, ao_ref, x1_ref, wol_ref, woa_ref, g2_ref, wg_ref, wu_ref, wd_ref,
                    out_ref, xn_ref, acc_ref):
    y = jnp.dot(lo_ref[...], wol_ref[...], preferred_element_type=F32)
    y = y + jnp.dot(ao_ref[...], woa_ref[...], preferred_element_type=F32)
    x2 = x1_ref[...] + y
    out_ref[...] = x2
    xn_ref[...] = _rms(x2, g2_ref[...]).astype(BF16)
    _ffn_accumulate(xn_ref, wg_ref, wu_ref, wd_ref, acc_ref)
    out_ref[...] = out_ref[...] + 0.5 * acc_ref[...]


def _out_ffn(lo, ao, x1, wol, woa, g2, wg, wu, wd):
    n, d = x1.shape
    tm = min(TOKEN_TILE, n)
    row = lambda w: pl.BlockSpec((tm, w), lambda i: (i, 0))
    weights = 2 * (wg.size + wu.size + wd.size + wol.size + woa.size)
    tiles = 2 * tm * (2 * lo.shape[1] + 2 * ao.shape[1] + 8 * d) + tm * d * (2 + 4) + 4 * 4 * tm * d
    return pl.pallas_call(
        _out_ffn_kernel,
        grid=(n // tm,),
        in_specs=[row(lo.shape[1]), row(ao.shape[1]), row(d), _resident(wol.shape), _resident(woa.shape),
                  _resident(g2.shape), _resident(wg.shape), _resident(wu.shape), _resident(wd.shape)],
        out_specs=row(d),
        out_shape=jax.ShapeDtypeStruct((n, d), F32),
        scratch_shapes=[pltpu.VMEM((tm, d), BF16), pltpu.VMEM((tm, d), F32)],
        compiler_params=pltpu.CompilerParams(dimension_semantics=("arbitrary",),
                                             vmem_limit_bytes=_vmem_limit(weights + tiles + (8 << 20))),
        name="out_ffn",
    )(lo, ao, x1, wol, woa, g2, wg, wu, wd)


def _lru_gates(xc, gate_w, lam_ref):
    wa_ref, ba_ref, wx_ref, bx_ref = gate_w
    xcb = xc.astype(BF16)
    r = jax.nn.sigmoid(jnp.dot(xcb, wa_ref[...], preferred_element_type=F32) + ba_ref[...])
    i = jax.nn.sigmoid(jnp.dot(xcb, wx_ref[...], preferred_element_type=F32) + bx_ref[...])
    log_a = (-LRU_C * jax.nn.softplus(-lam_ref[...])) * r
    a = jnp.exp(log_a)
    u = jnp.sqrt(-jnp.tanh(log_a) * (a * a + 1.0)) * (i * xc)
    return a, u


def _lru_kernel(p_ref, cw_ref, cb_ref, wa_ref, ba_ref, wx_ref, bx_ref, lam_ref, on_ref,
                out_ref, hl_ref, cs_ref, xh_ref, h_ref, a_ref, u_ref):
    ti = pl.program_id(1)
    tt = p_ref.shape[1]
    c = out_ref.shape[2]
    hist = V7X_SUBLANES

    @pl.when(ti == 0)
    def _():
        xh_ref[0:hist, :] = jnp.zeros((hist, c), F32)
        h_ref[...] = jnp.zeros_like(h_ref)

    x = p_ref[0, :, 0:c]
    xh_ref[hist:hist + tt, :] = x
    w = cw_ref[...]
    xc = cb_ref[...]
    for j in range(CONV_W):
        off = hist - (CONV_W - 1) + j
        xc = xc + xh_ref[off:off + tt, :] * w[j:j + 1, :]
    xh_ref[0:hist, :] = x[tt - hist:tt, :]

    a, u = _lru_gates(xc, (wa_ref, ba_ref, wx_ref, bx_ref), lam_ref)

    groups = tt // V7X_SUBLANES
    a3 = a.reshape(groups, V7X_SUBLANES, c)
    u3 = u.reshape(groups, V7X_SUBLANES, c)
    sub = lax.broadcasted_iota(jnp.int32, a3.shape, 1)
    k = 1
    while k < V7X_SUBLANES:
        m = sub >= k
        a_prev = pltpu.roll(a3, k, 1)
        u_prev = pltpu.roll(u3, k, 1)
        u3 = jnp.where(m, a3 * u_prev + u3, u3)
        a3 = jnp.where(m, a3 * a_prev, a3)
        k *= 2
    a_ref[...] = a3
    u_ref[...] = u3

    def step(g, h):
        hr = a_ref[g] * h + u_ref[g]
        u_ref[g] = hr
        return hr[V7X_SUBLANES - 1:V7X_SUBLANES, :]

    h = lax.fori_loop(0, groups, step, h_ref[...], unroll=8)
    h_ref[...] = h

    hs = u_ref[...].reshape(tt, c)
    y = hs * jax.nn.gelu(p_ref[0, :, c:2 * c])
    out_ref[0] = _rms(y, on_ref[...]).astype(out_ref.dtype)

    @pl.when(ti == pl.num_programs(1) - 1)
    def _():
        hl_ref[0] = h
        cs_ref[0] = x[tt - (CONV_W - 1):tt, :]


def _lru_prompt(proj_l, cw, cb, wa, ba, wx, bx, lam, on):
    b, t, w2 = proj_l.shape
    c = w2 // 2
    tt = min(TIME_TILE, t)
    params = [cw, cb, wa, ba, wx, bx, lam, on]
    return pl.pallas_call(
        _lru_kernel,
        grid=(b, t // tt),
        in_specs=[pl.BlockSpec((1, tt, w2), lambda i, j: (i, j, 0))] + [_resident(p.shape) for p in params],
        out_specs=[pl.BlockSpec((1, tt, c), lambda i, j: (i, j, 0)),
                   pl.BlockSpec((1, 1, c), lambda i, j: (i, 0, 0)),
                   pl.BlockSpec((1, CONV_W - 1, c), lambda i, j: (i, 0, 0))],
        out_shape=[jax.ShapeDtypeStruct((b, t, c), BF16), jax.ShapeDtypeStruct((b, 1, c), F32),
                   jax.ShapeDtypeStruct((b, CONV_W - 1, c), F32)],
        scratch_shapes=[pltpu.VMEM((tt + V7X_SUBLANES, c), F32), pltpu.VMEM((1, c), F32),
                        pltpu.VMEM((tt // V7X_SUBLANES, V7X_SUBLANES, c), F32),
                        pltpu.VMEM((tt // V7X_SUBLANES, V7X_SUBLANES, c), F32)],
        compiler_params=pltpu.CompilerParams(dimension_semantics=("arbitrary", "arbitrary"),
                                             vmem_limit_bytes=_vmem_limit(40 << 20)),
        name="lru_prompt",
    )(proj_l, *params)


def _lru_sample_kernel(x_ref, cst_ref, h0_ref, cw_ref, cb_ref, wa_ref, ba_ref, wx_ref, bx_ref, lam_ref, on_ref,
                       out_ref, hl_ref, cs_ref):
    t = x_ref.shape[0]
    c = out_ref.shape[2]
    w = cw_ref[...]
    rows = [cst_ref[j] for j in range(CONV_W - 1)] + [x_ref[j, :, 0:c] for j in range(t)]
    h = h0_ref[...]
    for s in range(t):
        xc = cb_ref[...]
        for j in range(CONV_W):
            xc = xc + rows[s + j] * w[j:j + 1, :]
        a, u = _lru_gates(xc, (wa_ref, ba_ref, wx_ref, bx_ref), lam_ref)
        h = a * h + u
        y = h * jax.nn.gelu(x_ref[s, :, c:2 * c])
        out_ref[s] = _rms(y, on_ref[...]).astype(out_ref.dtype)
    hl_ref[...] = h
    for j in range(CONV_W - 1):
        cs_ref[j] = rows[t + j]


def _lru_sample(x_tm, cst_tm, h0, cw, cb, wa, ba, wx, bx, lam, on):
    t, s, w2 = x_tm.shape
    c = w2 // 2
    return pl.pallas_call(
        _lru_sample_kernel,
        out_shape=[jax.ShapeDtypeStruct((t, s, c), BF16), jax.ShapeDtypeStruct((s, c), F32),
                   jax.ShapeDtypeStruct((CONV_W - 1, s, c), F32)],
        compiler_params=pltpu.CompilerParams(vmem_limit_bytes=_vmem_limit(32 << 20)),
        name="lru_sample",
    )(x_tm, cst_tm, h0, cw, cb, wa, ba, wx, bx, lam, on)


def _head_norm_rope(xc, gain, cos, sin, ones_ref):
    x2 = xc * xc
    hi = x2.astype(BF16)
    lo = (x2 - hi.astype(F32)).astype(BF16)
    ss = jnp.dot(hi, ones_ref[...], preferred_element_type=F32) + jnp.dot(lo, ones_ref[...], preferred_element_type=F32)
    xn = (xc * lax.rsqrt(ss * (1.0 / HEAD_DIM) + NORM_EPS)) * gain
    lane = lax.broadcasted_iota(jnp.int32, xn.shape, xn.ndim - 1)
    first_half = (lane & (HEAD_DIM // 2)) == 0
    ax = xn.ndim - 1
    swapped = jnp.where(first_half, pltpu.roll(xn, V7X_LANES - HEAD_DIM // 2, ax), pltpu.roll(xn, HEAD_DIM // 2, ax))
    return xn * cos + swapped * sin


def _group_lhs(qcols, g, lane_lo, axis):
    zero = jnp.zeros_like(qcols[0])
    parts = [jnp.where(lane_lo, qcols[2 * g], zero), jnp.where(lane_lo, qcols[2 * g + 1], zero),
             jnp.where(lane_lo, zero, qcols[2 * g]), jnp.where(lane_lo, zero, qcols[2 * g + 1])]
    return jnp.concatenate(parts, axis=axis).astype(BF16)


def _attn_kernel(pa_ref, cos_ref, sin_ref, qn_ref, kn_ref, sk_ref, ones_ref, on_ref,
                 out_ref, pk_ref, pv_ref, k2_ref, v2_ref, o_ref):
    ti = pl.program_id(1)
    tq = pa_ref.shape[1]
    blk = WINDOW
    nblk = tq // blk
    lanes = V7X_LANES
    aw = out_ref.shape[2]
    ncol = aw // lanes

    @pl.when(ti == 0)
    def _():
        k2_ref[:, blk:2 * blk, :] = jnp.zeros((N_KV_HEADS, blk, lanes), BF16)
        v2_ref[:, blk:2 * blk, :] = jnp.zeros((2, blk, lanes), BF16)

    lane = lax.broadcasted_iota(jnp.int32, (blk, lanes), 1)
    lane_lo = lane < HEAD_DIM
    rows4 = 4 * blk
    qi = lax.broadcasted_iota(jnp.int32, (rows4, 2 * blk), 0) % blk
    col = lax.broadcasted_iota(jnp.int32, (rows4, 2 * blk), 1)
    kj = col % blk
    in_slot0 = col < blk
    neg_inf = jnp.float32(-jnp.inf)

    for jb in range(nblk):
        r0 = jb * blk
        cos = cos_ref[r0:r0 + blk, :]
        sin = sin_ref[r0:r0 + blk, :]
        qcols = [_head_norm_rope(pa_ref[0, r0:r0 + blk, cc * lanes:(cc + 1) * lanes], qn_ref[...], cos, sin, ones_ref)
                 * (HEAD_DIM ** -0.5) for cc in range(ncol)]
        k = _head_norm_rope(pa_ref[0, r0:r0 + blk, aw:aw + lanes], kn_ref[...], cos, sin, ones_ref)
        v = pa_ref[0, r0:r0 + blk, aw + lanes:aw + 2 * lanes]

        cur = (jb % 2) * blk
        k_sw = pltpu.roll(k, HEAD_DIM, 1)
        k2_ref[0, cur:cur + blk, :] = jnp.where(lane_lo, k, k_sw).astype(BF16)
        k2_ref[1, cur:cur + blk, :] = jnp.where(lane_lo, k_sw, k).astype(BF16)
        v2_ref[0, cur:cur + blk, :] = v.astype(BF16)
        v2_ref[1, cur:cur + blk, :] = pltpu.roll(v, HEAD_DIM, 1).astype(BF16)

        is_cur = in_slot0 if cur == 0 else jnp.logical_not(in_slot0)
        first = jnp.where(ti > 0, 0, blk) if jb == 0 else 0
        valid = jnp.logical_or(jnp.logical_and(is_cur, kj <= qi),
                               jnp.logical_and(jnp.logical_not(is_cur), kj > qi + first))

        for g in range(N_KV_HEADS):
            lhs = _group_lhs(qcols, g, lane_lo, 0)
            s = lax.dot_general(lhs, k2_ref[g], (((1,), (1,)), ((), ())), preferred_element_type=F32)
            s = jnp.where(valid, s, neg_inf)
            heads = (4 * g, 4 * g + 2, 4 * g + 1, 4 * g + 3)
            sk = jnp.concatenate([jnp.broadcast_to(sk_ref[h:h + 1, 0:1], (blk, 1)) for h in heads], axis=0)
            mx = jnp.maximum(jnp.max(s, axis=-1, keepdims=True), sk)
            p = jnp.exp(s - mx)
            den = jnp.sum(p, axis=-1, keepdims=True) + jnp.exp(sk - mx)
            pb = p.astype(BF16)
            o_even = jnp.dot(pb[0:2 * blk], v2_ref[g], preferred_element_type=F32) / den[0:2 * blk]
            o_odd = jnp.dot(pb[2 * blk:4 * blk], v2_ref[1 - g], preferred_element_type=F32) / den[2 * blk:4 * blk]
            for half in range(2):
                cc = 2 * g + half
                o_ref[:, cc * lanes:(cc + 1) * lanes] = jnp.where(
                    lane_lo, o_even[half * blk:(half + 1) * blk], o_odd[half * blk:(half + 1) * blk])

        out_ref[0, r0:r0 + blk, :] = _rms(o_ref[...], on_ref[...]).astype(out_ref.dtype)

        if jb == nblk - 1:
            @pl.when(ti == pl.num_programs(1) - 1)
            def _():
                pk_ref[0] = k
                pv_ref[0] = v


def _attn_prompt(proj_a, cos, sin, qn, kn, sk, ones, on):
    b, t, w = proj_a.shape
    aw = on.shape[1]
    kvw = N_KV_HEADS * HEAD_DIM
    tq = min(TIME_TILE, t)
    assert tq % (2 * WINDOW) == 0 and t % tq == 0
    params = [qn, kn, sk, ones, on]
    return pl.pallas_call(
        _attn_kernel,
        grid=(b, t // tq),
        in_specs=[pl.BlockSpec((1, tq, w), lambda i, j: (i, j, 0)),
                  pl.BlockSpec((tq, V7X_LANES), lambda i, j: (j, 0)),
                  pl.BlockSpec((tq, V7X_LANES), lambda i, j: (j, 0))] + [_resident(p.shape) for p in params],
        out_specs=[pl.BlockSpec((1, tq, aw), lambda i, j: (i, j, 0)),
                   pl.BlockSpec((1, WINDOW, kvw), lambda i, j: (i, 0, 0)),
                   pl.BlockSpec((1, WINDOW, kvw), lambda i, j: (i, 0, 0))],
        out_shape=[jax.ShapeDtypeStruct((b, t, aw), BF16), jax.ShapeDtypeStruct((b, WINDOW, kvw), F32),
                   jax.ShapeDtypeStruct((b, WINDOW, kvw), F32)],
        scratch_shapes=[pltpu.VMEM((N_KV_HEADS, 2 * WINDOW, V7X_LANES), BF16),
                        pltpu.VMEM((2, 2 * WINDOW, V7X_LANES), BF16),
                        pltpu.VMEM((WINDOW, aw), F32)],
        compiler_params=pltpu.CompilerParams(dimension_semantics=("arbitrary", "arbitrary"),
                                             vmem_limit_bytes=_vmem_limit(40 << 20)),
        name="attn_prompt",
    )(proj_a, cos, sin, *params)


def _qk_prep_kernel(pa_ref, cos_ref, sin_ref, qn_ref, kn_ref, ones_ref, q_ref, k_ref):
    lanes = V7X_LANES
    aw = q_ref.shape[1]
    cos = cos_ref[...]
    sin = sin_ref[...]
    for cc in range(aw // lanes):
        q_ref[:, cc * lanes:(cc + 1) * lanes] = _head_norm_rope(
            pa_ref[:, cc * lanes:(cc + 1) * lanes], qn_ref[...], cos, sin, ones_ref) * (HEAD_DIM ** -0.5)
    k_ref[...] = _head_norm_rope(pa_ref[:, aw:aw + lanes], kn_ref[...], cos, sin, ones_ref)


def _qk_prep(pa, cos, sin, qn, kn, ones, aw):
    n = pa.shape[0]
    return pl.pallas_call(
        _qk_prep_kernel,
        out_shape=[jax.ShapeDtypeStruct((n, aw), F32), jax.ShapeDtypeStruct((n, N_KV_HEADS * HEAD_DIM), F32)],
        compiler_params=pltpu.CompilerParams(vmem_limit_bytes=_vmem_limit(32 << 20)),
        name="qk_prep_sample",
    )(pa, cos, sin, qn, kn, ones)


def _attn_cached_kernel(q_ref, kn_ref, vn_ref, ck_ref, cv_ref, sk_ref, on_ref,
                        out_ref, nk_ref, nv_ref, kk_ref, vv_ref):
    sb, t, aw = q_ref.shape
    w = ck_ref.shape[1]
    lanes = V7X_LANES
    jpad = kk_ref.shape[1]

    kk_ref[:, 0:w, :] = ck_ref[...]
    vv_ref[:, 0:w, :] = cv_ref[...]
    kk_ref[:, w:w + t, :] = kn_ref[...]
    vv_ref[:, w:w + t, :] = vn_ref[...]
    kk_ref[:, w + t:jpad, :] = jnp.zeros((sb, jpad - w - t, lanes), F32)
    vv_ref[:, w + t:jpad, :] = jnp.zeros((sb, jpad - w - t, lanes), F32)
    nk_ref[...] = kk_ref[:, t:t + w, :]
    nv_ref[...] = vv_ref[:, t:t + w, :]

    kk = kk_ref[...]
    vv = vv_ref[...]
    k_sw = pltpu.roll(kk, HEAD_DIM, 2)
    v_sw = pltpu.roll(vv, HEAD_DIM, 2)
    lane_k = lax.broadcasted_iota(jnp.int32, kk.shape, 2) < HEAD_DIM
    lane_q = lax.broadcasted_iota(jnp.int32, (sb, t, lanes), 2) < HEAD_DIM
    k2 = (jnp.where(lane_k, kk, k_sw).astype(BF16), jnp.where(lane_k, k_sw, kk).astype(BF16))
    v2 = (vv.astype(BF16), v_sw.astype(BF16))

    qcols = [q_ref[:, :, cc * lanes:(cc + 1) * lanes] for cc in range(aw // lanes)]
    tq = lax.broadcasted_iota(jnp.int32, (sb, 4 * t, jpad), 1) % t
    j = lax.broadcasted_iota(jnp.int32, (sb, 4 * t, jpad), 2)
    valid = jnp.logical_and(j > tq, j <= tq + w)
    neg_inf = jnp.float32(-jnp.inf)

    ocols = []
    for g in range(N_KV_HEADS):
        lhs = _group_lhs(qcols, g, lane_q, 1)
        s = jnp.einsum('bqd,bjd->bqj', lhs, k2[g], preferred_element_type=F32)
        s = jnp.where(valid, s, neg_inf)
        heads = (4 * g, 4 * g + 2, 4 * g + 1, 4 * g + 3)
        sk = jnp.concatenate([jnp.broadcast_to(sk_ref[h:h + 1, 0:1].reshape(1, 1, 1), (sb, t, 1)) for h in heads],
                             axis=1)
        mx = jnp.maximum(jnp.max(s, axis=-1, keepdims=True), sk)
        p = jnp.exp(s - mx)
        den = jnp.sum(p, axis=-1, keepdims=True) + jnp.exp(sk - mx)
        pb = p.astype(BF16)
        o_even = jnp.einsum('bqj,bjd->bqd', pb[:, 0:2 * t], v2[g], preferred_element_type=F32) / den[:, 0:2 * t]
        o_odd = jnp.einsum('bqj,bjd->bqd', pb[:, 2 * t:4 * t], v2[1 - g], preferred_element_type=F32) / den[:, 2 * t:4 * t]
        for half in range(2):
            ocols.append(jnp.where(lane_q, o_even[:, half * t:(half + 1) * t], o_odd[:, half * t:(half + 1) * t]))
    o = jnp.concatenate(ocols, axis=2)
    out_ref[...] = _rms(o, on_ref[...].reshape(1, 1, aw)).astype(out_ref.dtype)


def _attn_cached(q3, kn3, vn3, ck, cv, sk, on):
    s, t, aw = q3.shape
    w, kvw = ck.shape[1], ck.shape[2]
    sb = min(SAMPLE_SEQ_TILE, s)
    jpad = -(-(w + t) // V7X_SUBLANES) * V7X_SUBLANES
    seq = lambda a, c: pl.BlockSpec((sb, a, c), lambda i: (i, 0, 0))
    return pl.pallas_call(
        _attn_cached_kernel,
        grid=(s // sb,),
        in_specs=[seq(t, aw), seq(t, kvw), seq(t, kvw), seq(w, kvw), seq(w, kvw), _resident(sk.shape), _resident(on.shape)],
        out_specs=[seq(t, aw), seq(w, kvw), seq(w, kvw)],
        out_shape=[jax.ShapeDtypeStruct((s, t, aw), BF16), jax.ShapeDtypeStruct((s, w, kvw), F32),
                   jax.ShapeDtypeStruct((s, w, kvw), F32)],
        scratch_shapes=[pltpu.VMEM((sb, jpad, kvw), F32), pltpu.VMEM((sb, jpad, kvw), F32)],
        compiler_params=pltpu.CompilerParams(dimension_semantics=("arbitrary",),
                                             vmem_limit_bytes=_vmem_limit(40 << 20)),
        name="attn_sample",
    )(q3, kn3, vn3, ck, cv, sk, on)


def _rope_tables(pos):
    half = HEAD_DIM // 2
    inv_freq = ROPE_THETA ** (-jnp.arange(half, dtype=F32) / half)
    ang = pos.astype(F32)[:, None] * inv_freq[None, :]
    cos = jnp.cos(ang)
    sin = jnp.sin(ang)
    reps = V7X_LANES // HEAD_DIM
    return jnp.tile(jnp.concatenate([cos, cos], axis=1), (1, reps)), jnp.tile(jnp.concatenate([-sin, sin], axis=1), (1, reps))


def _block_diag(w):
    n, c, d = w.shape
    eye = jnp.eye(n, dtype=w.dtype)
    return (w[:, :, None, :] * eye[:, None, :, None]).reshape(n * c, n * d)


def _chunk_cols(w):
    d, f = w.shape
    return w.astype(BF16).reshape(d, f // FF_CHUNK, FF_CHUNK).transpose(1, 0, 2)


def _chunk_rows(w):
    f, d = w.shape
    return w.astype(BF16).reshape(f // FF_CHUNK, FF_CHUNK, d)


def _row(v):
    return v.reshape(1, -1).astype(F32)


def kernel(x_prompt, x_sample, state_lru_h, state_conv, cache_k, cache_v, ffn1_norm, ffn1_w_gate, ffn1_w_up, ffn1_w_down, mix_norm, w_in, conv_w, conv_b, lru_wa, lru_ba, lru_wx, lru_bx, lru_lambda, q_norm, k_norm, attn_sinks, lru_out_norm, attn_out_norm, w_out, ffn2_norm, ffn2_w_gate, ffn2_w_up, ffn2_w_down):
    depth = w_in.shape[0]
    b, t, d = x_prompt.shape
    s, ts, _ = x_sample.shape
    c = lru_lambda.shape[1]
    aw = attn_out_norm.shape[1]
    kvw = N_KV_HEADS * HEAD_DIM
    win = cache_k.shape[2]
    assert t >= WINDOW and win == WINDOW

    cos_p, sin_p = _rope_tables(jnp.arange(t, dtype=jnp.int32))
    cos_s, sin_s = _rope_tables(PAST_LEN + jnp.arange(ts, dtype=jnp.int32))
    cos_s, sin_s = jnp.tile(cos_s, (s, 1)), jnp.tile(sin_s, (s, 1))
    ones = _block_diag(jnp.ones((V7X_LANES // HEAD_DIM, HEAD_DIM, HEAD_DIM), BF16))

    yp = x_prompt.reshape(b * t, d)
    ys = x_sample.reshape(s * ts, d)
    outs = [[] for _ in range(8)]
    for l in range(depth):
        ffn1 = (_row(ffn1_norm[l]), _chunk_cols(ffn1_w_gate[l]), _chunk_cols(ffn1_w_up[l]), _chunk_rows(ffn1_w_down[l]))
        ffn2 = (_row(ffn2_norm[l]), _chunk_cols(ffn2_w_gate[l]), _chunk_cols(ffn2_w_up[l]), _chunk_rows(ffn2_w_down[l]))
        win_b = w_in[l].astype(BF16)
        winl, wina = win_b[:, :2 * c], win_b[:, 2 * c:]
        wout_b = w_out[l].astype(BF16)
        wol, woa = wout_b[:c], wout_b[c:]
        lru_p = (conv_w[l], _row(conv_b[l]), _block_diag(lru_wa[l]).astype(BF16), _row(lru_ba[l]),
                 _block_diag(lru_wx[l]).astype(BF16), _row(lru_bx[l]), _row(lru_lambda[l]), _row(lru_out_norm[l]))
        reps = V7X_LANES // HEAD_DIM
        qn = _row(jnp.tile(q_norm[l], reps))
        kn = _row(jnp.tile(k_norm[l], reps))
        sk = jnp.broadcast_to(attn_sinks[l].astype(F32)[:, None], (attn_sinks.shape[1], V7X_LANES))
        a_on = _row(attn_out_norm[l])

        x1, pl_p, pa_p = _ffn_in(yp, ffn1[0], ffn1[1], ffn1[2], ffn1[3], _row(mix_norm[l]), winl, wina)
        lo, hl, cs = _lru_prompt(pl_p.reshape(b, t, 2 * c), *lru_p)
        ao, pk, pv = _attn_prompt(pa_p.reshape(b, t, aw + 2 * kvw), cos_p, sin_p, qn, kn, sk, ones, a_on)
        yp = _out_ffn(lo.reshape(b * t, c), ao.reshape(b * t, aw), x1, wol, woa, *ffn2)
        outs[0].append(hl.reshape(b, c))
        outs[1].append(cs)
        outs[2].append(pk.reshape(b, WINDOW, N_KV_HEADS, HEAD_DIM))
        outs[3].append(pv.reshape(b, WINDOW, N_KV_HEADS, HEAD_DIM))

        x1s, pl_s, pa_s = _ffn_in(ys, ffn1[0], ffn1[1], ffn1[2], ffn1[3], _row(mix_norm[l]), winl, wina)
        x_tm = pl_s.reshape(s, ts, 2 * c).transpose(1, 0, 2)
        cst_tm = state_conv[l].transpose(1, 0, 2)
        lo_tm, hls, cs_tm = _lru_sample(x_tm, cst_tm, state_lru_h[l], *lru_p)
        q_s, k_s = _qk_prep(pa_s, cos_s, sin_s, qn, kn, ones, aw)
        v_s = pa_s[:, aw + kvw:]
        aos, nk, nv = _attn_cached(q_s.reshape(s, ts, aw), k_s.reshape(s, ts, kvw), v_s.reshape(s, ts, kvw),
                                   cache_k[l].reshape(s, win, kvw), cache_v[l].reshape(s, win, kvw), sk, a_on)
        ys = _out_ffn(lo_tm.transpose(1, 0, 2).reshape(s * ts, c), aos.reshape(s * ts, aw), x1s, wol, woa, *ffn2)
        outs[4].append(hls)
        outs[5].append(cs_tm.transpose(1, 0, 2))
        outs[6].append(nk.reshape(s, win, N_KV_HEADS, HEAD_DIM))
        outs[7].append(nv.reshape(s, win, N_KV_HEADS, HEAD_DIM))

    st = [jnp.stack(o) for o in outs]
    return (yp.reshape(b, t, d), ys.reshape(s, ts, d), st[0], st[1], st[2], st[3], st[4], st[5], st[6], st[7])
```

```python
import functools

import jax
import jax.numpy as jnp
from jax import lax
from jax.experimental import pallas as pl
from jax.experimental.pallas import tpu as pltpu

F32 = jnp.float32
BF16 = jnp.bfloat16

NORM_EPS = 1e-6
LRU_C = 8.0
CONV_W = 4
HEAD_DIM = 64
N_KV_HEADS = 2
WINDOW = 128
ROPE_THETA = 10000.0
PAST_LEN = 16384

V7X_LANES = 128
V7X_SUBLANES = 8
V7X_MXU_DIM = 256
V7X_VMEM_BYTES = 64 * 1024 * 1024

TOKEN_TILE = 512
TIME_TILE = 512
SAMPLE_SEQ_TILE = 16
FF_CHUNK = V7X_MXU_DIM


def _vmem_limit(nbytes):
    return int(min(nbytes, V7X_VMEM_BYTES - 4 * 1024 * 1024))


def _resident(shape):
    nd = len(shape)
    return pl.BlockSpec(shape, lambda *_: (0,) * nd, pipeline_mode=pl.Buffered(1))


def _rms(x, g):
    ms = jnp.mean(x * x, axis=-1, keepdims=True)
    return (x * lax.rsqrt(ms + NORM_EPS)) * g


def _ffn_accumulate(xn_ref, wg_ref, wu_ref, wd_ref, acc_ref):
    acc_ref[...] = jnp.zeros_like(acc_ref)

    def chunk(c, carry):
        xn = xn_ref[...]
        g = jnp.dot(xn, wg_ref[c], preferred_element_type=F32)
        u = jnp.dot(xn, wu_ref[c], preferred_element_type=F32)
        a = (jax.nn.silu(g) * u).astype(BF16)
        acc_ref[...] += jnp.dot(a, wd_ref[c], preferred_element_type=F32)
        return carry

    lax.fori_loop(0, wg_ref.shape[0], chunk, 0, unroll=True)


def _ffn_in_kernel(x_ref, g1_ref, wg_ref, wu_ref, wd_ref, gm_ref, winl_ref, wina_ref,
                   x1_ref, pl_ref, pa_ref, xn_ref, acc_ref):
    xn_ref[...] = _rms(x_ref[...], g1_ref[...]).astype(BF16)
    _ffn_accumulate(xn_ref, wg_ref, wu_ref, wd_ref, acc_ref)
    x1 = x_ref[...] + 0.5 * acc_ref[...]
    x1_ref[...] = x1
    xn_ref[...] = _rms(x1, gm_ref[...]).astype(BF16)
    pl_ref[...] = jnp.dot(xn_ref[...], winl_ref[...], preferred_element_type=F32)
    pa_ref[...] = jnp.dot(xn_ref[...], wina_ref[...], preferred_element_type=F32)


def _ffn_in(x, g1, wg, wu, wd, gm, winl, wina):
    n, d = x.shape
    tm = min(TOKEN_TILE, n)
    wl, wa = winl.shape[1], wina.shape[1]
    row = lambda w: pl.BlockSpec((tm, w), lambda i: (i, 0))
    weights = 2 * (wg.size + wu.size + wd.size + winl.size + wina.size)
    tiles = 2 * 4 * tm * (2 * d + wl + wa) + tm * d * (2 + 4) + 4 * 4 * tm * max(FF_CHUNK, wl)
    return pl.pallas_call(
        _ffn_in_kernel,
        grid=(n // tm,),
        in_specs=[row(d), _resident(g1.shape), _resident(wg.shape), _resident(wu.shape), _resident(wd.shape),
                  _resident(gm.shape), _resident(winl.shape), _resident(wina.shape)],
        out_specs=[row(d), row(wl), row(wa)],
        out_shape=[jax.ShapeDtypeStruct((n, d), F32), jax.ShapeDtypeStruct((n, wl), F32),
                   jax.ShapeDtypeStruct((n, wa), F32)],
        scratch_shapes=[pltpu.VMEM((tm, d), BF16), pltpu.VMEM((tm, d), F32)],
        compiler_params=pltpu.CompilerParams(dimension_semantics=("arbitrary",),
                                             vmem_limit_bytes=_vmem_limit(weights + tiles + (8 << 20))),
        name="ffn_in",
    )(x, g1, wg, wu, wd, gm, winl, wina)


def _out_ffn_kernel(lo_ref, ao_ref, x1_ref, wol_ref, woa_ref, g2_ref, wg_ref, wu_ref, wd_ref,
                    out_ref, xn_ref, acc_ref):
    y = jnp.dot(lo_ref[...], wol_ref[...], preferred_element_type=F32)
    y = y + jnp.dot(ao_ref[...], woa_ref[...], preferred_element_type=F32)
    x2 = x1_ref[...] + y
    out_ref[...] = x2
    xn_ref[...] = _rms(x2, g2_ref[...]).astype(BF16)
    _ffn_accumulate(xn_ref, wg_ref, wu_ref, wd_ref, acc_ref)
    out_ref[...] = out_ref[...] + 0.5 * acc_ref[...]


def _out_ffn(lo, ao, x1, wol, woa, g2, wg, wu, wd):
    n, d = x1.shape
    tm = min(TOKEN_TILE, n)
    row = lambda w: pl.BlockSpec((tm, w), lambda i: (i, 0))
    weights = 2 * (wg.size + wu.size + wd.size + wol.size + woa.size)
    tiles = 2 * tm * (2 * lo.shape[1] + 2 * ao.shape[1] + 8 * d) + tm * d * (2 + 4) + 4 * 4 * tm * d
    return pl.pallas_call(
        _out_ffn_kernel,
        grid=(n // tm,),
        in_specs=[row(lo.shape[1]), row(ao.shape[1]), row(d), _resident(wol.shape), _resident(woa.shape),
                  _resident(g2.shape), _resident(wg.shape), _resident(wu.shape), _resident(wd.shape)],
        out_specs=row(d),
        out_shape=jax.ShapeDtypeStruct((n, d), F32),
        scratch_shapes=[pltpu.VMEM((tm, d), BF16), pltpu.VMEM((tm, d), F32)],
        compiler_params=pltpu.CompilerParams(dimension_semantics=("arbitrary",),
                                             vmem_limit_bytes=_vmem_limit(weights + tiles + (8 << 20))),
        name="out_ffn",
    )(lo, ao, x1, wol, woa, g2, wg, wu, wd)


def _lru_gates(xc, gate_w, lam_ref):
    wa_ref, ba_ref, wx_ref, bx_ref = gate_w
    xcb = xc.astype(BF16)
    r = jax.nn.sigmoid(jnp.dot(xcb, wa_ref[...], preferred_element_type=F32) + ba_ref[...])
    i = jax.nn.sigmoid(jnp.dot(xcb, wx_ref[...], preferred_element_type=F32) + bx_ref[...])
    log_a = (-LRU_C * jax.nn.softplus(-lam_ref[...])) * r
    a = jnp.exp(log_a)
    u = jnp.sqrt(-jnp.tanh(log_a) * (a * a + 1.0)) * (i * xc)
    return a, u


def _lru_kernel(p_ref, cw_ref, cb_ref, wa_ref, ba_ref, wx_ref, bx_ref, lam_ref, on_ref,
                out_ref, hl_ref, cs_ref, xh_ref, h_ref, a_ref, u_ref):
    ti = pl.program_id(1)
    tt = p_ref.shape[1]
    c = out_ref.shape[2]
    hist = V7X_SUBLANES

    @pl.when(ti == 0)
    def _():
        xh_ref[0:hist, :] = jnp.zeros((hist, c), F32)
        h_ref[...] = jnp.zeros_like(h_ref)

    x = p_ref[0, :, 0:c]
    xh_ref[hist:hist + tt, :] = x
    w = cw_ref[...]
    xc = cb_ref[...]
    for j in range(CONV_W):
        off = hist - (CONV_W - 1) + j
        xc = xc + xh_ref[off:off + tt, :] * w[j:j + 1, :]
    xh_ref[0:hist, :] = x[tt - hist:tt, :]

    a, u = _lru_gates(xc, (wa_ref, ba_ref, wx_ref, bx_ref), lam_ref)

    groups = tt // V7X_SUBLANES
    a3 = a.reshape(groups, V7X_SUBLANES, c)
    u3 = u.reshape(groups, V7X_SUBLANES, c)
    sub = lax.broadcasted_iota(jnp.int32, a3.shape, 1)
    k = 1
    while k < V7X_SUBLANES:
        m = sub >= k
        a_prev = pltpu.roll(a3, k, 1)
        u_prev = pltpu.roll(u3, k, 1)
        u3 = jnp.where(m, a3 * u_prev + u3, u3)
        a3 = jnp.where(m, a3 * a_prev, a3)
        k *= 2
    a_ref[...] = a3
    u_ref[...] = u3

    def step(g, h):
        hr = a_ref[g] * h + u_ref[g]
        u_ref[g] = hr
        return hr[V7X_SUBLANES - 1:V7X_SUBLANES, :]

    h = lax.fori_loop(0, groups, step, h_ref[...], unroll=8)
    h_ref[...] = h

    hs = u_ref[...].reshape(tt, c)
    y = hs * jax.nn.gelu(p_ref[0, :, c:2 * c])
    out_ref[0] = _rms(y, on_ref[...]).astype(out_ref.dtype)

    @pl.when(ti == pl.num_programs(1) - 1)
    def _():
        hl_ref[0] = h
        cs_ref[0] = x[tt - (CONV_W - 1):tt, :]


def _lru_prompt(proj_l, cw, cb, wa, ba, wx, bx, lam, on):
    b, t, w2 = proj_l.shape
    c = w2 // 2
    tt = min(TIME_TILE, t)
    params = [cw, cb, wa, ba, wx, bx, lam, on]
    return pl.pallas_call(
        _lru_kernel,
        grid=(b, t // tt),
        in_specs=[pl.BlockSpec((1, tt, w2), lambda i, j: (i, j, 0))] + [_resident(p.shape) for p in params],
        out_specs=[pl.BlockSpec((1, tt, c), lambda i, j: (i, j, 0)),
                   pl.BlockSpec((1, 1, c), lambda i, j: (i, 0, 0)),
                   pl.BlockSpec((1, CONV_W - 1, c), lambda i, j: (i, 0, 0))],
        out_shape=[jax.ShapeDtypeStruct((b, t, c), BF16), jax.ShapeDtypeStruct((b, 1, c), F32),
                   jax.ShapeDtypeStruct((b, CONV_W - 1, c), F32)],
        scratch_shapes=[pltpu.VMEM((tt + V7X_SUBLANES, c), F32), pltpu.VMEM((1, c), F32),
                        pltpu.VMEM((tt // V7X_SUBLANES, V7X_SUBLANES, c), F32),
                        pltpu.VMEM((tt // V7X_SUBLANES, V7X_SUBLANES, c), F32)],
        compiler_params=pltpu.CompilerParams(dimension_semantics=("arbitrary", "arbitrary"),
                                             vmem_limit_bytes=_vmem_limit(40 << 20)),
        name="lru_prompt",
    )(proj_l, *params)


def _lru_sample_kernel(x_ref, cst_ref, h0_ref, cw_ref, cb_ref, wa_ref, ba_ref, wx_ref, bx_ref, lam_ref, on_ref,
                       out_ref, hl_ref, cs_ref):
    t = x_ref.shape[0]
    c = out_ref.shape[2]
    w = cw_ref[...]
    rows = [cst_ref[j] for j in range(CONV_W - 1)] + [x_ref[j, :, 0:c] for j in range(t)]
    h = h0_ref[...]
    for s in range(t):
        xc = cb_ref[...]
        for j in range(CONV_W):
            xc = xc + rows[s + j] * w[j:j + 1, :]
        a, u = _lru_gates(xc, (wa_ref, ba_ref, wx_ref, bx_ref), lam_ref)
        h = a * h + u
        y = h * jax.nn.gelu(x_ref[s, :, c:2 * c])
        out_ref[s] = _rms(y, on_ref[...]).astype(out_ref.dtype)
    hl_ref[...] = h
    for j in range(CONV_W - 1):
        cs_ref[j] = rows[t + j]


def _lru_sample(x_tm, cst_tm, h0, cw, cb, wa, ba, wx, bx, lam, on):
    t, s, w2 = x_tm.shape
    c = w2 // 2
    return pl.pallas_call(
        _lru_sample_kernel,
        out_shape=[jax.ShapeDtypeStruct((t, s, c), BF16), jax.ShapeDtypeStruct((s, c), F32),
                   jax.ShapeDtypeStruct((CONV_W - 1, s, c), F32)],
        compiler_params=pltpu.CompilerParams(vmem_limit_bytes=_vmem_limit(32 << 20)),
        name="lru_sample",
    )(x_tm, cst_tm, h0, cw, cb, wa, ba, wx, bx, lam, on)


def _head_norm_rope(xc, gain, cos, sin, ones_ref):
    x2 = xc * xc
    hi = x2.astype(BF16)
    lo = (x2 - hi.astype(F32)).astype(BF16)
    ss = jnp.dot(hi, ones_ref[...], preferred_element_type=F32) + jnp.dot(lo, ones_ref[...], preferred_element_type=F32)
    xn = (xc * lax.rsqrt(ss * (1.0 / HEAD_DIM) + NORM_EPS)) * gain
    lane = lax.broadcasted_iota(jnp.int32, xn.shape, xn.ndim - 1)
    first_half = (lane & (HEAD_DIM // 2)) == 0
    ax = xn.ndim - 1
    swapped = jnp.where(first_half, pltpu.roll(xn, V7X_LANES - HEAD_DIM // 2, ax), pltpu.roll(xn, HEAD_DIM // 2, ax))
    return xn * cos + swapped * sin


def _group_lhs(qcols, g, lane_lo, axis):
    zero = jnp.zeros_like(qcols[0])
    parts = [jnp.where(lane_lo, qcols[2 * g], zero), jnp.where(lane_lo, qcols[2 * g + 1], zero),
             jnp.where(lane_lo, zero, qcols[2 * g]), jnp.where(lane_lo, zero, qcols[2 * g + 1])]
    return jnp.concatenate(parts, axis=axis).astype(BF16)


def _attn_kernel(pa_ref, cos_ref, sin_ref, qn_ref, kn_ref, sk_ref, ones_ref, on_ref,
                 out_ref, pk_ref, pv_ref, k2_ref, vt_ref):
    ti = pl.program_id(1)
    tq = pa_ref.shape[1]
    blk = WINDOW
    nblk = tq // blk
    lanes = V7X_LANES
    aw = out_ref.shape[2]
    ncol = aw // lanes
    heads_per_group = 2 * ncol // N_KV_HEADS

    @pl.when(ti == 0)
    def _():
        k2_ref[:, 0:blk, :] = jnp.zeros((N_KV_HEADS, blk, lanes), BF16)
        vt_ref[:, 0:blk] = jnp.zeros((lanes, blk), BF16)

    cos = cos_ref[...]
    sin = sin_ref[...]
    cols = [pa_ref[0, :, cc * lanes:(cc + 1) * lanes] for cc in range(ncol + 1)]
    sq = [x * x for x in cols]
    hi = [x.astype(BF16) for x in sq]
    lo = [(x - h.astype(F32)).astype(BF16) for x, h in zip(sq, hi)]
    ss = (jnp.dot(jnp.concatenate(hi, axis=0), ones_ref[...], preferred_element_type=F32)
          + jnp.dot(jnp.concatenate(lo, axis=0), ones_ref[...], preferred_element_type=F32))
    lane = lax.broadcasted_iota(jnp.int32, (tq, lanes), 1)
    first_half = (lane & (HEAD_DIM // 2)) == 0
    lane_lo = lane < HEAD_DIM
    rot = []
    for cc, x in enumerate(cols):
        gain = qn_ref[...] if cc < ncol else kn_ref[...]
        xn = (x * lax.rsqrt(ss[cc * tq:(cc + 1) * tq] * (1.0 / HEAD_DIM) + NORM_EPS)) * gain
        swapped = jnp.where(first_half, pltpu.roll(xn, lanes - HEAD_DIM // 2, 1), pltpu.roll(xn, HEAD_DIM // 2, 1))
        rot.append(xn * cos + swapped * sin)
    k = rot[ncol]
    v = pa_ref[0, :, aw + lanes:aw + 2 * lanes]

    zero = jnp.zeros((tq, lanes), F32)
    qh = []
    for cc in range(ncol):
        qs = rot[cc] * (HEAD_DIM ** -0.5)
        qh.append(jnp.where(lane_lo, qs, zero).astype(BF16))
        qh.append(jnp.where(lane_lo, zero, qs).astype(BF16))

    k_sw = pltpu.roll(k, HEAD_DIM, 1)
    k2_ref[0, blk:blk + tq, :] = jnp.where(lane_lo, k, k_sw).astype(BF16)
    k2_ref[1, blk:blk + tq, :] = jnp.where(lane_lo, k_sw, k).astype(BF16)
    for jb in range(nblk):
        vt_ref[:, (jb + 1) * blk:(jb + 2) * blk] = v[jb * blk:(jb + 1) * blk].T.astype(BF16)

    ncolq = heads_per_group * blk
    kj = lax.broadcasted_iota(jnp.int32, (blk, ncolq), 0)
    qi = lax.broadcasted_iota(jnp.int32, (blk, ncolq), 1) % blk
    tri = kj <= qi
    neg_inf = jnp.float32(-jnp.inf)

    for jb in range(nblk):
        r0 = jb * blk
        ocols = []
        for g in range(N_KV_HEADS):
            lhs = jnp.concatenate([q[r0:r0 + blk] for q in qh[g * heads_per_group:(g + 1) * heads_per_group]], axis=0)
            st = lax.dot_general(k2_ref[g, r0:r0 + 2 * blk, :], lhs, (((1,), (1,)), ((), ())),
                                 preferred_element_type=F32)
            s_prev = st[0:blk]
            if jb == 0:
                s_prev = s_prev + jnp.where(ti > 0, jnp.float32(0.0), neg_inf)
            m = jnp.where(tri, st[blk:2 * blk], s_prev)
            sk = sk_ref[g:g + 1, :]
            mx = jnp.maximum(jnp.max(m, axis=0, keepdims=True), sk)
            p = jnp.exp(m - mx)
            den = jnp.sum(p, axis=0, keepdims=True) + jnp.exp(sk - mx)
            pz = jnp.zeros_like(p)
            pt = jnp.concatenate([jnp.where(tri, pz, p), jnp.where(tri, p, pz)], axis=0).astype(BF16)
            ot = jnp.dot(vt_ref[:, r0:r0 + 2 * blk], pt, preferred_element_type=F32)
            og = ot[g * HEAD_DIM:(g + 1) * HEAD_DIM, :] * (1.0 / den)
            for half in range(heads_per_group // 2):
                pair = jnp.concatenate([og[:, (2 * half) * blk:(2 * half + 1) * blk],
                                        og[:, (2 * half + 1) * blk:(2 * half + 2) * blk]], axis=0)
                ocols.append(pair.T)
        o = jnp.concatenate(ocols, axis=1)
        out_ref[0, r0:r0 + blk, :] = _rms(o, on_ref[...]).astype(out_ref.dtype)

    k2_ref[:, 0:blk, :] = k2_ref[:, tq:tq + blk, :]
    vt_ref[:, 0:blk] = vt_ref[:, tq:tq + blk]

    @pl.when(ti == pl.num_programs(1) - 1)
    def _():
        pk_ref[0] = k[tq - blk:tq]
        pv_ref[0] = v[tq - blk:tq]


def _attn_prompt(proj_a, cos, sin, qn, kn, sk, ones, on):
    b, t, w = proj_a.shape
    aw = on.shape[1]
    kvw = N_KV_HEADS * HEAD_DIM
    tq = min(TIME_TILE, t)
    assert tq % WINDOW == 0 and t % tq == 0
    params = [qn, kn, sk, ones, on]
    return pl.pallas_call(
        _attn_kernel,
        grid=(b, t // tq),
        in_specs=[pl.BlockSpec((1, tq, w), lambda i, j: (i, j, 0)),
                  pl.BlockSpec((tq, V7X_LANES), lambda i, j: (j, 0)),
                  pl.BlockSpec((tq, V7X_LANES), lambda i, j: (j, 0))] + [_resident(p.shape) for p in params],
        out_specs=[pl.BlockSpec((1, tq, aw), lambda i, j: (i, j, 0)),
                   pl.BlockSpec((1, WINDOW, kvw), lambda i, j: (i, 0, 0)),
                   pl.BlockSpec((1, WINDOW, kvw), lambda i, j: (i, 0, 0))],
        out_shape=[jax.ShapeDtypeStruct((b, t, aw), BF16), jax.ShapeDtypeStruct((b, WINDOW, kvw), F32),
                   jax.ShapeDtypeStruct((b, WINDOW, kvw), F32)],
        scratch_shapes=[pltpu.VMEM((N_KV_HEADS, WINDOW + tq, V7X_LANES), BF16),
                        pltpu.VMEM((V7X_LANES, WINDOW + tq), BF16)],
        compiler_params=pltpu.CompilerParams(dimension_semantics=("arbitrary", "arbitrary"),
                                             vmem_limit_bytes=_vmem_limit(40 << 20)),
        name="attn_prompt",
    )(proj_a, cos, sin, *params)


def _qk_prep_kernel(pa_ref, cos_ref, sin_ref, qn_ref, kn_ref, ones_ref, q_ref, k_ref):
    lanes = V7X_LANES
    aw = q_ref.shape[1]
    cos = cos_ref[...]
    sin = sin_ref[...]
    for cc in range(aw // lanes):
        q_ref[:, cc * lanes:(cc + 1) * lanes] = _head_norm_rope(
            pa_ref[:, cc * lanes:(cc + 1) * lanes], qn_ref[...], cos, sin, ones_ref) * (HEAD_DIM ** -0.5)
    k_ref[...] = _head_norm_rope(pa_ref[:, aw:aw + lanes], kn_ref[...], cos, sin, ones_ref)


def _qk_prep(pa, cos, sin, qn, kn, ones, aw):
    n = pa.shape[0]
    return pl.pallas_call(
        _qk_prep_kernel,
        out_shape=[jax.ShapeDtypeStruct((n, aw), F32), jax.ShapeDtypeStruct((n, N_KV_HEADS * HEAD_DIM), F32)],
        compiler_params=pltpu.CompilerParams(vmem_limit_bytes=_vmem_limit(32 << 20)),
        name="qk_prep_sample",
    )(pa, cos, sin, qn, kn, ones)


def _attn_cached_kernel(q_ref, kn_ref, vn_ref, ck_ref, cv_ref, sk_ref, on_ref,
                        out_ref, nk_ref, nv_ref, kk_ref, vv_ref):
    sb, t, aw = q_ref.shape
    w = ck_ref.shape[1]
    lanes = V7X_LANES
    jpad = kk_ref.shape[1]

    kk_ref[:, 0:w, :] = ck_ref[...]
    vv_ref[:, 0:w, :] = cv_ref[...]
    kk_ref[:, w:w + t, :] = kn_ref[...]
    vv_ref[:, w:w + t, :] = vn_ref[...]
    kk_ref[:, w + t:jpad, :] = jnp.zeros((sb, jpad - w - t, lanes), F32)
    vv_ref[:, w + t:jpad, :] = jnp.zeros((sb, jpad - w - t, lanes), F32)
    nk_ref[...] = kk_ref[:, t:t + w, :]
    nv_ref[...] = vv_ref[:, t:t + w, :]

    kk = kk_ref[...]
    vv = vv_ref[...]
    k_sw = pltpu.roll(kk, HEAD_DIM, 2)
    v_sw = pltpu.roll(vv, HEAD_DIM, 2)
    lane_k = lax.broadcasted_iota(jnp.int32, kk.shape, 2) < HEAD_DIM
    lane_q = lax.broadcasted_iota(jnp.int32, (sb, t, lanes), 2) < HEAD_DIM
    k2 = (jnp.where(lane_k, kk, k_sw).astype(BF16), jnp.where(lane_k, k_sw, kk).astype(BF16))
    v2 = (vv.astype(BF16), v_sw.astype(BF16))

    qcols = [q_ref[:, :, cc * lanes:(cc + 1) * lanes] for cc in range(aw // lanes)]
    tq = lax.broadcasted_iota(jnp.int32, (sb, 4 * t, jpad), 1) % t
    j = lax.broadcasted_iota(jnp.int32, (sb, 4 * t, jpad), 2)
    valid = jnp.logical_and(j > tq, j <= tq + w)
    neg_inf = jnp.float32(-jnp.inf)

    ocols = []
    for g in range(N_KV_HEADS):
        lhs = _group_lhs(qcols, g, lane_q, 1)
        s = jnp.einsum('bqd,bjd->bqj', lhs, k2[g], preferred_element_type=F32)
        s = jnp.where(valid, s, neg_inf)
        heads = (4 * g, 4 * g + 2, 4 * g + 1, 4 * g + 3)
        sk = jnp.concatenate([jnp.broadcast_to(sk_ref[h:h + 1, 0:1].reshape(1, 1, 1), (sb, t, 1)) for h in heads],
                             axis=1)
        mx = jnp.maximum(jnp.max(s, axis=-1, keepdims=True), sk)
        p = jnp.exp(s - mx)
        den = jnp.sum(p, axis=-1, keepdims=True) + jnp.exp(sk - mx)
        pb = p.astype(BF16)
        o_even = jnp.einsum('bqj,bjd->bqd', pb[:, 0:2 * t], v2[g], preferred_element_type=F32) / den[:, 0:2 * t]
        o_odd = jnp.einsum('bqj,bjd->bqd', pb[:, 2 * t:4 * t], v2[1 - g], preferred_element_type=F32) / den[:, 2 * t:4 * t]
        for half in range(2):
            ocols.append(jnp.where(lane_q, o_even[:, half * t:(half + 1) * t], o_odd[:, half * t:(half + 1) * t]))
    o = jnp.concatenate(ocols, axis=2)
    out_ref[...] = _rms(o, on_ref[...].reshape(1, 1, aw)).astype(out_ref.dtype)


def _attn_cached(q3, kn3, vn3, ck, cv, sk, on):
    s, t, aw = q3.shape
    w, kvw = ck.shape[1], ck.shape[2]
    sb = min(SAMPLE_SEQ_TILE, s)
    jpad = -(-(w + t) // V7X_SUBLANES) * V7X_SUBLANES
    seq = lambda a, c: pl.BlockSpec((sb, a, c), lambda i: (i, 0, 0))
    return pl.pallas_call(
        _attn_cached_kernel,
        grid=(s // sb,),
        in_specs=[seq(t, aw), seq(t, kvw), seq(t, kvw), seq(w, kvw), seq(w, kvw), _resident(sk.shape), _resident(on.shape)],
        out_specs=[seq(t, aw), seq(w, kvw), seq(w, kvw)],
        out_shape=[jax.ShapeDtypeStruct((s, t, aw), BF16), jax.ShapeDtypeStruct((s, w, kvw), F32),
                   jax.ShapeDtypeStruct((s, w, kvw), F32)],
        scratch_shapes=[pltpu.VMEM((sb, jpad, kvw), F32), pltpu.VMEM((sb, jpad, kvw), F32)],
        compiler_params=pltpu.CompilerParams(dimension_semantics=("arbitrary",),
                                             vmem_limit_bytes=_vmem_limit(40 << 20)),
        name="attn_sample",
    )(q3, kn3, vn3, ck, cv, sk, on)


def _rope_tables(pos):
    half = HEAD_DIM // 2
    inv_freq = ROPE_THETA ** (-jnp.arange(half, dtype=F32) / half)
    ang = pos.astype(F32)[:, None] * inv_freq[None, :]
    cos = jnp.cos(ang)
    sin = jnp.sin(ang)
    reps = V7X_LANES // HEAD_DIM
    return jnp.tile(jnp.concatenate([cos, cos], axis=1), (1, reps)), jnp.tile(jnp.concatenate([-sin, sin], axis=1), (1, reps))


def _block_diag(w):
    n, c, d = w.shape
    eye = jnp.eye(n, dtype=w.dtype)
    return (w[:, :, None, :] * eye[:, None, :, None]).reshape(n * c, n * d)


def _chunk_cols(w):
    d, f = w.shape
    return w.astype(BF16).reshape(d, f // FF_CHUNK, FF_CHUNK).transpose(1, 0, 2)


def _chunk_rows(w):
    f, d = w.shape
    return w.astype(BF16).reshape(f // FF_CHUNK, FF_CHUNK, d)


def _row(v):
    return v.reshape(1, -1).astype(F32)


def kernel(x_prompt, x_sample, state_lru_h, state_conv, cache_k, cache_v, ffn1_norm, ffn1_w_gate, ffn1_w_up, ffn1_w_down, mix_norm, w_in, conv_w, conv_b, lru_wa, lru_ba, lru_wx, lru_bx, lru_lambda, q_norm, k_norm, attn_sinks, lru_out_norm, attn_out_norm, w_out, ffn2_norm, ffn2_w_gate, ffn2_w_up, ffn2_w_down):
    depth = w_in.shape[0]
    b, t, d = x_prompt.shape
    s, ts, _ = x_sample.shape
    c = lru_lambda.shape[1]
    aw = attn_out_norm.shape[1]
    kvw = N_KV_HEADS * HEAD_DIM
    win = cache_k.shape[2]
    assert t >= WINDOW and win == WINDOW

    cos_p, sin_p = _rope_tables(jnp.arange(t, dtype=jnp.int32))
    cos_s, sin_s = _rope_tables(PAST_LEN + jnp.arange(ts, dtype=jnp.int32))
    cos_s, sin_s = jnp.tile(cos_s, (s, 1)), jnp.tile(sin_s, (s, 1))
    ones = _block_diag(jnp.ones((V7X_LANES // HEAD_DIM, HEAD_DIM, HEAD_DIM), BF16))

    yp = x_prompt.reshape(b * t, d)
    ys = x_sample.reshape(s * ts, d)
    outs = [[] for _ in range(8)]
    for l in range(depth):
        ffn1 = (_row(ffn1_norm[l]), _chunk_cols(ffn1_w_gate[l]), _chunk_cols(ffn1_w_up[l]), _chunk_rows(ffn1_w_down[l]))
        ffn2 = (_row(ffn2_norm[l]), _chunk_cols(ffn2_w_gate[l]), _chunk_cols(ffn2_w_up[l]), _chunk_rows(ffn2_w_down[l]))
        win_b = w_in[l].astype(BF16)
        winl, wina = win_b[:, :2 * c], win_b[:, 2 * c:]
        wout_b = w_out[l].astype(BF16)
        wol, woa = wout_b[:c], wout_b[c:]
        lru_p = (conv_w[l], _row(conv_b[l]), _block_diag(lru_wa[l]).astype(BF16), _row(lru_ba[l]),
                 _block_diag(lru_wx[l]).astype(BF16), _row(lru_bx[l]), _row(lru_lambda[l]), _row(lru_out_norm[l]))
        reps = V7X_LANES // HEAD_DIM
        qn = _row(jnp.tile(q_norm[l], reps))
        kn = _row(jnp.tile(k_norm[l], reps))
        sk = jnp.broadcast_to(attn_sinks[l].astype(F32)[:, None], (attn_sinks.shape[1], V7X_LANES))
        sk_rows = jnp.repeat(attn_sinks[l].astype(F32).reshape(N_KV_HEADS, -1), WINDOW, axis=1)
        a_on = _row(attn_out_norm[l])

        x1, pl_p, pa_p = _ffn_in(yp, ffn1[0], ffn1[1], ffn1[2], ffn1[3], _row(mix_norm[l]), winl, wina)
        lo, hl, cs = _lru_prompt(pl_p.reshape(b, t, 2 * c), *lru_p)
        ao, pk, pv = _attn_prompt(pa_p.reshape(b, t, aw + 2 * kvw), cos_p, sin_p, qn, kn, sk_rows, ones, a_on)
        yp = _out_ffn(lo.reshape(b * t, c), ao.reshape(b * t, aw), x1, wol, woa, *ffn2)
        outs[0].append(hl.reshape(b, c))
        outs[1].append(cs)
        outs[2].append(pk.reshape(b, WINDOW, N_KV_HEADS, HEAD_DIM))
        outs[3].append(pv.reshape(b, WINDOW, N_KV_HEADS, HEAD_DIM))

        x1s, pl_s, pa_s = _ffn_in(ys, ffn1[0], ffn1[1], ffn1[2], ffn1[3], _row(mix_norm[l]), winl, wina)
        x_tm = pl_s.reshape(s, ts, 2 * c).transpose(1, 0, 2)
        cst_tm = state_conv[l].transpose(1, 0, 2)
        lo_tm, hls, cs_tm = _lru_sample(x_tm, cst_tm, state_lru_h[l], *lru_p)
        q_s, k_s = _qk_prep(pa_s, cos_s, sin_s, qn, kn, ones, aw)
        v_s = pa_s[:, aw + kvw:]
        aos, nk, nv = _attn_cached(q_s.reshape(s, ts, aw), k_s.reshape(s, ts, kvw), v_s.reshape(s, ts, kvw),
                                   cache_k[l].reshape(s, win, kvw), cache_v[l].reshape(s, win, kvw), sk, a_on)
        ys = _out_ffn(lo_tm.transpose(1, 0, 2).reshape(s * ts, c), aos.reshape(s * ts, aw), x1s, wol, woa, *ffn2)
        outs[4].append(hls)
        outs[5].append(cs_tm.transpose(1, 0, 2))
        outs[6].append(nk.reshape(s, win, N_KV_HEADS, HEAD_DIM))
        outs[7].append(nv.reshape(s, win, N_KV_HEADS, HEAD_DIM))

    st = [jnp.stack(o) for o in outs]
    return (yp.reshape(b, t, d), ys.reshape(s, ts, d), st[0], st[1], st[2], st[3], st[4], st[5], st[6], st[7])
```

```python
import functools

import jax
import jax.numpy as jnp
from jax import lax
from jax.experimental import pallas as pl
from jax.experimental.pallas import tpu as pltpu

F32 = jnp.float32
BF16 = jnp.bfloat16

NORM_EPS = 1e-6
LRU_C = 8.0
CONV_W = 4
HEAD_DIM = 64
N_KV_HEADS = 2
WINDOW = 128
ROPE_THETA = 10000.0
PAST_LEN = 16384

V7X_LANES = 128
V7X_SUBLANES = 8
V7X_MXU_DIM = 256
V7X_VMEM_BYTES = 64 * 1024 * 1024

TOKEN_TILE = 512
TIME_TILE = 512
SAMPLE_SEQ_TILE = 16
FF_CHUNK = V7X_MXU_DIM


def _vmem_limit(nbytes):
    return int(min(nbytes, V7X_VMEM_BYTES - 4 * 1024 * 1024))


def _resident(shape):
    nd = len(shape)
    return pl.BlockSpec(shape, lambda *_: (0,) * nd, pipeline_mode=pl.Buffered(1))


def _rms(x, g):
    ms = jnp.mean(x * x, axis=-1, keepdims=True)
    return (x * lax.rsqrt(ms + NORM_EPS)) * g


def _ffn_accumulate(xn_ref, wg_ref, wu_ref, wd_ref, acc_ref):
    acc_ref[...] = jnp.zeros_like(acc_ref)
    for c0 in range(0, wg_ref.shape[1], FF_CHUNK):
        xn = xn_ref[...]
        g = jnp.dot(xn, wg_ref[:, c0:c0 + FF_CHUNK], preferred_element_type=F32)
        u = jnp.dot(xn, wu_ref[:, c0:c0 + FF_CHUNK], preferred_element_type=F32)
        a = (jax.nn.silu(g) * u).astype(BF16)
        acc_ref[...] += jnp.dot(a, wd_ref[c0:c0 + FF_CHUNK, :], preferred_element_type=F32)


def _ffn_in_kernel(x_ref, g1_ref, wg_ref, wu_ref, wd_ref, gm_ref, winl_ref, wina_ref,
                   x1_ref, pl_ref, pa_ref, xn_ref, acc_ref):
    xn_ref[...] = _rms(x_ref[...], g1_ref[...]).astype(BF16)
    _ffn_accumulate(xn_ref, wg_ref, wu_ref, wd_ref, acc_ref)
    x1 = x_ref[...] + 0.5 * acc_ref[...]
    x1_ref[...] = x1
    xn_ref[...] = _rms(x1, gm_ref[...]).astype(BF16)
    pl_ref[...] = jnp.dot(xn_ref[...], winl_ref[...], preferred_element_type=F32)
    pa_ref[...] = jnp.dot(xn_ref[...], wina_ref[...], preferred_element_type=F32)


def _ffn_in(x, g1, wg, wu, wd, gm, winl, wina):
    n, d = x.shape
    tm = min(TOKEN_TILE, n)
    wl, wa = winl.shape[1], wina.shape[1]
    row = lambda w: pl.BlockSpec((tm, w), lambda i: (i, 0))
    weights = 2 * (wg.size + wu.size + wd.size + winl.size + wina.size)
    tiles = 2 * 4 * tm * (2 * d + wl + wa) + tm * d * (2 + 4) + 4 * 4 * tm * max(FF_CHUNK, wl)
    return pl.pallas_call(
        _ffn_in_kernel,
        grid=(n // tm,),
        in_specs=[row(d), _resident(g1.shape), _resident(wg.shape), _resident(wu.shape), _resident(wd.shape),
                  _resident(gm.shape), _resident(winl.shape), _resident(wina.shape)],
        out_specs=[row(d), row(wl), row(wa)],
        out_shape=[jax.ShapeDtypeStruct((n, d), F32), jax.ShapeDtypeStruct((n, wl), F32),
                   jax.ShapeDtypeStruct((n, wa), F32)],
        scratch_shapes=[pltpu.VMEM((tm, d), BF16), pltpu.VMEM((tm, d), F32)],
        compiler_params=pltpu.CompilerParams(dimension_semantics=("arbitrary",),
                                             vmem_limit_bytes=_vmem_limit(weights + tiles + (8 << 20))),
        name="ffn_in",
    )(x, g1, wg, wu, wd, gm, winl, wina)


def _out_ffn_kernel(lo_ref, ao_ref, x1_ref, wol_ref, woa_ref, g2_ref, wg_ref, wu_ref, wd_ref,
                    out_ref, xn_ref, acc_ref):
    y = jnp.dot(lo_ref[...], wol_ref[...], preferred_element_type=F32)
    y = y + jnp.dot(ao_ref[...], woa_ref[...], preferred_element_type=F32)
    x2 = x1_ref[...] + y
    out_ref[...] = x2
    xn_ref[...] = _rms(x2, g2_ref[...]).astype(BF16)
    _ffn_accumulate(xn_ref, wg_ref, wu_ref, wd_ref, acc_ref)
    out_ref[...] = out_ref[...] + 0.5 * acc_ref[...]


def _out_ffn(lo, ao, x1, wol, woa, g2, wg, wu, wd):
    n, d = x1.shape
    tm = min(TOKEN_TILE, n)
    row = lambda w: pl.BlockSpec((tm, w), lambda i: (i, 0))
    weights = 2 * (wg.size + wu.size + wd.size + wol.size + woa.size)
    tiles = 2 * tm * (2 * lo.shape[1] + 2 * ao.shape[1] + 8 * d) + tm * d * (2 + 4) + 4 * 4 * tm * d
    return pl.pallas_call(
        _out_ffn_kernel,
        grid=(n // tm,),
        in_specs=[row(lo.shape[1]), row(ao.shape[1]), row(d), _resident(wol.shape), _resident(woa.shape),
                  _resident(g2.shape), _resident(wg.shape), _resident(wu.shape), _resident(wd.shape)],
        out_specs=row(d),
        out_shape=jax.ShapeDtypeStruct((n, d), F32),
        scratch_shapes=[pltpu.VMEM((tm, d), BF16), pltpu.VMEM((tm, d), F32)],
        compiler_params=pltpu.CompilerParams(dimension_semantics=("arbitrary",),
                                             vmem_limit_bytes=_vmem_limit(weights + tiles + (8 << 20))),
        name="out_ffn",
    )(lo, ao, x1, wol, woa, g2, wg, wu, wd)


def _lru_gates(xc, gate_w, lam_ref):
    wa_ref, ba_ref, wx_ref, bx_ref = gate_w
    xcb = xc.astype(BF16)
    r = 0.5 * jnp.tanh(jnp.dot(xcb, wa_ref[...], preferred_element_type=F32) + ba_ref[...]) + 0.5
    i = 0.5 * jnp.tanh(jnp.dot(xcb, wx_ref[...], preferred_element_type=F32) + bx_ref[...]) + 0.5
    log_a = (-LRU_C * jax.nn.softplus(-lam_ref[...])) * r
    a = jnp.exp(log_a)
    w = -jnp.tanh(log_a) * (a * a + 1.0)
    root = jnp.where(w > 0.0, w * lax.rsqrt(w), 0.0)
    u = root * (i * xc)
    return a, u


def _lru_kernel(p_ref, cw_ref, cb_ref, wa_ref, ba_ref, wx_ref, bx_ref, lam_ref, on_ref,
                out_ref, hl_ref, cs_ref, xh_ref, h_ref, a_ref, u_ref):
    ti = pl.program_id(1)
    tt = p_ref.shape[1]
    c = out_ref.shape[2]
    hist = V7X_SUBLANES

    @pl.when(ti == 0)
    def _():
        xh_ref[0:hist, :] = jnp.zeros((hist, c), F32)
        h_ref[...] = jnp.zeros_like(h_ref)

    x = p_ref[0, :, 0:c]
    xh_ref[hist:hist + tt, :] = x
    w = cw_ref[...]
    xc = cb_ref[...]
    for j in range(CONV_W):
        off = hist - (CONV_W - 1) + j
        xc = xc + xh_ref[off:off + tt, :] * w[j:j + 1, :]
    xh_ref[0:hist, :] = x[tt - hist:tt, :]

    a, u = _lru_gates(xc, (wa_ref, ba_ref, wx_ref, bx_ref), lam_ref)

    groups = tt // V7X_SUBLANES
    a3 = a.reshape(groups, V7X_SUBLANES, c)
    u3 = u.reshape(groups, V7X_SUBLANES, c)
    sub = lax.broadcasted_iota(jnp.int32, a3.shape, 1)
    k = 1
    while k < V7X_SUBLANES:
        m = sub >= k
        a_prev = pltpu.roll(a3, k, 1)
        u_prev = pltpu.roll(u3, k, 1)
        u3 = jnp.where(m, a3 * u_prev + u3, u3)
        a3 = jnp.where(m, a3 * a_prev, a3)
        k *= 2
    a_ref[...] = a3
    u_ref[...] = u3

    def step(g, h):
        hr = a_ref[g] * h + u_ref[g]
        u_ref[g] = hr
        return hr[V7X_SUBLANES - 1:V7X_SUBLANES, :]

    h = lax.fori_loop(0, groups, step, h_ref[...], unroll=8)
    h_ref[...] = h

    hs = u_ref[...].reshape(tt, c)
    y = hs * jax.nn.gelu(p_ref[0, :, c:2 * c])
    out_ref[0] = _rms(y, on_ref[...]).astype(out_ref.dtype)

    @pl.when(ti == pl.num_programs(1) - 1)
    def _():
        hl_ref[0] = h
        cs_ref[0] = x[tt - (CONV_W - 1):tt, :]


def _lru_prompt(proj_l, cw, cb, wa, ba, wx, bx, lam, on):
    b, t, w2 = proj_l.shape
    c = w2 // 2
    tt = min(TIME_TILE, t)
    params = [cw, cb, wa, ba, wx, bx, lam, on]
    return pl.pallas_call(
        _lru_kernel,
        grid=(b, t // tt),
        in_specs=[pl.BlockSpec((1, tt, w2), lambda i, j: (i, j, 0))] + [_resident(p.shape) for p in params],
        out_specs=[pl.BlockSpec((1, tt, c), lambda i, j: (i, j, 0)),
                   pl.BlockSpec((1, 1, c), lambda i, j: (i, 0, 0)),
                   pl.BlockSpec((1, CONV_W - 1, c), lambda i, j: (i, 0, 0))],
        out_shape=[jax.ShapeDtypeStruct((b, t, c), BF16), jax.ShapeDtypeStruct((b, 1, c), F32),
                   jax.ShapeDtypeStruct((b, CONV_W - 1, c), F32)],
        scratch_shapes=[pltpu.VMEM((tt + V7X_SUBLANES, c), F32), pltpu.VMEM((1, c), F32),
                        pltpu.VMEM((tt // V7X_SUBLANES, V7X_SUBLANES, c), F32),
                        pltpu.VMEM((tt // V7X_SUBLANES, V7X_SUBLANES, c), F32)],
        compiler_params=pltpu.CompilerParams(dimension_semantics=("arbitrary", "arbitrary"),
                                             vmem_limit_bytes=_vmem_limit(40 << 20)),
        name="lru_prompt",
    )(proj_l, *params)


def _lru_sample_kernel(x_ref, cst_ref, h0_ref, cw_ref, cb_ref, wa_ref, ba_ref, wx_ref, bx_ref, lam_ref, on_ref,
                       out_ref, hl_ref, cs_ref):
    t = x_ref.shape[0]
    c = out_ref.shape[2]
    w = cw_ref[...]
    rows = [cst_ref[j] for j in range(CONV_W - 1)] + [x_ref[j, :, 0:c] for j in range(t)]
    h = h0_ref[...]
    for s in range(t):
        xc = cb_ref[...]
        for j in range(CONV_W):
            xc = xc + rows[s + j] * w[j:j + 1, :]
        a, u = _lru_gates(xc, (wa_ref, ba_ref, wx_ref, bx_ref), lam_ref)
        h = a * h + u
        y = h * jax.nn.gelu(x_ref[s, :, c:2 * c])
        out_ref[s] = _rms(y, on_ref[...]).astype(out_ref.dtype)
    hl_ref[...] = h
    for j in range(CONV_W - 1):
        cs_ref[j] = rows[t + j]


def _lru_sample(x_tm, cst_tm, h0, cw, cb, wa, ba, wx, bx, lam, on):
    t, s, w2 = x_tm.shape
    c = w2 // 2
    return pl.pallas_call(
        _lru_sample_kernel,
        out_shape=[jax.ShapeDtypeStruct((t, s, c), BF16), jax.ShapeDtypeStruct((s, c), F32),
                   jax.ShapeDtypeStruct((CONV_W - 1, s, c), F32)],
        compiler_params=pltpu.CompilerParams(vmem_limit_bytes=_vmem_limit(32 << 20)),
        name="lru_sample",
    )(x_tm, cst_tm, h0, cw, cb, wa, ba, wx, bx, lam, on)


def _head_norm_rope(xc, gain, cos, sin, ones_ref):
    x2 = xc * xc
    hi = x2.astype(BF16)
    lo = (x2 - hi.astype(F32)).astype(BF16)
    ss = jnp.dot(hi, ones_ref[...], preferred_element_type=F32) + jnp.dot(lo, ones_ref[...], preferred_element_type=F32)
    xn = (xc * lax.rsqrt(ss * (1.0 / HEAD_DIM) + NORM_EPS)) * gain
    lane = lax.broadcasted_iota(jnp.int32, xn.shape, xn.ndim - 1)
    first_half = (lane & (HEAD_DIM // 2)) == 0
    ax = xn.ndim - 1
    swapped = jnp.where(first_half, pltpu.roll(xn, V7X_LANES - HEAD_DIM // 2, ax), pltpu.roll(xn, HEAD_DIM // 2, ax))
    return xn * cos + swapped * sin


def _group_lhs(qcols, g, lane_lo, axis):
    zero = jnp.zeros_like(qcols[0])
    parts = [jnp.where(lane_lo, qcols[2 * g], zero), jnp.where(lane_lo, qcols[2 * g + 1], zero),
             jnp.where(lane_lo, zero, qcols[2 * g]), jnp.where(lane_lo, zero, qcols[2 * g + 1])]
    return jnp.concatenate(parts, axis=axis).astype(BF16)


def _attn_kernel(pa_ref, cos_ref, sin_ref, qn_ref, kn_ref, sk_ref, ones_ref, on_ref,
                 out_ref, pk_ref, pv_ref, k2_ref, vt_ref):
    ti = pl.program_id(1)
    tq = pa_ref.shape[1]
    blk = WINDOW
    nblk = tq // blk
    lanes = V7X_LANES
    aw = out_ref.shape[2]
    ncol = aw // lanes
    heads_per_group = 2 * ncol // N_KV_HEADS

    @pl.when(ti == 0)
    def _():
        k2_ref[:, 0:blk, :] = jnp.zeros((N_KV_HEADS, blk, lanes), BF16)
        vt_ref[:, 0:blk] = jnp.zeros((lanes, blk), BF16)

    cos = cos_ref[...]
    sin = sin_ref[...]
    cols = [pa_ref[0, :, cc * lanes:(cc + 1) * lanes] for cc in range(ncol + 1)]
    sq = [x * x for x in cols]
    hi = [x.astype(BF16) for x in sq]
    lo = [(x - h.astype(F32)).astype(BF16) for x, h in zip(sq, hi)]
    ss = (jnp.dot(jnp.concatenate(hi, axis=0), ones_ref[...], preferred_element_type=F32)
          + jnp.dot(jnp.concatenate(lo, axis=0), ones_ref[...], preferred_element_type=F32))
    lane = lax.broadcasted_iota(jnp.int32, (tq, lanes), 1)
    first_half = (lane & (HEAD_DIM // 2)) == 0
    lane_lo = lane < HEAD_DIM
    rot = []
    for cc, x in enumerate(cols):
        gain = qn_ref[...] if cc < ncol else kn_ref[...]
        xn = (x * lax.rsqrt(ss[cc * tq:(cc + 1) * tq] * (1.0 / HEAD_DIM) + NORM_EPS)) * gain
        swapped = jnp.where(first_half, pltpu.roll(xn, lanes - HEAD_DIM // 2, 1), pltpu.roll(xn, HEAD_DIM // 2, 1))
        rot.append(xn * cos + swapped * sin)
    k = rot[ncol]
    v = pa_ref[0, :, aw + lanes:aw + 2 * lanes]

    zero = jnp.zeros((tq, lanes), F32)
    qh = []
    for cc in range(ncol):
        qs = rot[cc] * (HEAD_DIM ** -0.5)
        qh.append(jnp.where(lane_lo, qs, zero).astype(BF16))
        qh.append(jnp.where(lane_lo, zero, qs).astype(BF16))

    k_sw = pltpu.roll(k, HEAD_DIM, 1)
    k2_ref[0, blk:blk + tq, :] = jnp.where(lane_lo, k, k_sw).astype(BF16)
    k2_ref[1, blk:blk + tq, :] = jnp.where(lane_lo, k_sw, k).astype(BF16)
    for jb in range(nblk):
        vt_ref[:, (jb + 1) * blk:(jb + 2) * blk] = v[jb * blk:(jb + 1) * blk].T.astype(BF16)

    ncolq = heads_per_group * blk
    kj = lax.broadcasted_iota(jnp.int32, (blk, ncolq), 0)
    qi = lax.broadcasted_iota(jnp.int32, (blk, ncolq), 1) % blk
    tri = kj <= qi
    neg_inf = jnp.float32(-jnp.inf)

    for jb in range(nblk):
        r0 = jb * blk
        ocols = []
        for g in range(N_KV_HEADS):
            lhs = jnp.concatenate([q[r0:r0 + blk] for q in qh[g * heads_per_group:(g + 1) * heads_per_group]], axis=0)
            st = lax.dot_general(k2_ref[g, r0:r0 + 2 * blk, :], lhs, (((1,), (1,)), ((), ())),
                                 preferred_element_type=F32)
            s_prev = st[0:blk]
            if jb == 0:
                s_prev = s_prev + jnp.where(ti > 0, jnp.float32(0.0), neg_inf)
            m = jnp.where(tri, st[blk:2 * blk], s_prev)
            sk = sk_ref[g:g + 1, :]
            mx = jnp.maximum(jnp.max(m, axis=0, keepdims=True), sk)
            p = jnp.exp(m - mx)
            den = jnp.sum(p, axis=0, keepdims=True) + jnp.exp(sk - mx)
            pz = jnp.zeros_like(p)
            pt = jnp.concatenate([jnp.where(tri, pz, p), jnp.where(tri, p, pz)], axis=0).astype(BF16)
            ot = jnp.dot(vt_ref[:, r0:r0 + 2 * blk], pt, preferred_element_type=F32)
            og = ot[g * HEAD_DIM:(g + 1) * HEAD_DIM, :] * (1.0 / den)
            for half in range(heads_per_group // 2):
                pair = jnp.concatenate([og[:, (2 * half) * blk:(2 * half + 1) * blk],
                                        og[:, (2 * half + 1) * blk:(2 * half + 2) * blk]], axis=0)
                ocols.append(pair.T)
        o = jnp.concatenate(ocols, axis=1)
        out_ref[0, r0:r0 + blk, :] = _rms(o, on_ref[...]).astype(out_ref.dtype)

    k2_ref[:, 0:blk, :] = k2_ref[:, tq:tq + blk, :]
    vt_ref[:, 0:blk] = vt_ref[:, tq:tq + blk]

    @pl.when(ti == pl.num_programs(1) - 1)
    def _():
        pk_ref[0] = k[tq - blk:tq]
        pv_ref[0] = v[tq - blk:tq]


def _attn_prompt(proj_a, cos, sin, qn, kn, sk, ones, on):
    b, t, w = proj_a.shape
    aw = on.shape[1]
    kvw = N_KV_HEADS * HEAD_DIM
    tq = min(TIME_TILE, t)
    assert tq % WINDOW == 0 and t % tq == 0
    params = [qn, kn, sk, ones, on]
    return pl.pallas_call(
        _attn_kernel,
        grid=(b, t // tq),
        in_specs=[pl.BlockSpec((1, tq, w), lambda i, j: (i, j, 0)),
                  pl.BlockSpec((tq, V7X_LANES), lambda i, j: (j, 0)),
                  pl.BlockSpec((tq, V7X_LANES), lambda i, j: (j, 0))] + [_resident(p.shape) for p in params],
        out_specs=[pl.BlockSpec((1, tq, aw), lambda i, j: (i, j, 0)),
                   pl.BlockSpec((1, WINDOW, kvw), lambda i, j: (i, 0, 0)),
                   pl.BlockSpec((1, WINDOW, kvw), lambda i, j: (i, 0, 0))],
        out_shape=[jax.ShapeDtypeStruct((b, t, aw), BF16), jax.ShapeDtypeStruct((b, WINDOW, kvw), F32),
                   jax.ShapeDtypeStruct((b, WINDOW, kvw), F32)],
        scratch_shapes=[pltpu.VMEM((N_KV_HEADS, WINDOW + tq, V7X_LANES), BF16),
                        pltpu.VMEM((V7X_LANES, WINDOW + tq), BF16)],
        compiler_params=pltpu.CompilerParams(dimension_semantics=("arbitrary", "arbitrary"),
                                             vmem_limit_bytes=_vmem_limit(40 << 20)),
        name="attn_prompt",
    )(proj_a, cos, sin, *params)


def _qk_prep_kernel(pa_ref, cos_ref, sin_ref, qn_ref, kn_ref, ones_ref, q_ref, k_ref):
    lanes = V7X_LANES
    aw = q_ref.shape[1]
    cos = cos_ref[...]
    sin = sin_ref[...]
    for cc in range(aw // lanes):
        q_ref[:, cc * lanes:(cc + 1) * lanes] = _head_norm_rope(
            pa_ref[:, cc * lanes:(cc + 1) * lanes], qn_ref[...], cos, sin, ones_ref) * (HEAD_DIM ** -0.5)
    k_ref[...] = _head_norm_rope(pa_ref[:, aw:aw + lanes], kn_ref[...], cos, sin, ones_ref)


def _qk_prep(pa, cos, sin, qn, kn, ones, aw):
    n = pa.shape[0]
    return pl.pallas_call(
        _qk_prep_kernel,
        out_shape=[jax.ShapeDtypeStruct((n, aw), F32), jax.ShapeDtypeStruct((n, N_KV_HEADS * HEAD_DIM), F32)],
        compiler_params=pltpu.CompilerParams(vmem_limit_bytes=_vmem_limit(32 << 20)),
        name="qk_prep_sample",
    )(pa, cos, sin, qn, kn, ones)


def _attn_cached_kernel(q_ref, kn_ref, vn_ref, ck_ref, cv_ref, sk_ref, on_ref,
                        out_ref, nk_ref, nv_ref, kk_ref, vv_ref):
    sb, t, aw = q_ref.shape
    w = ck_ref.shape[1]
    lanes = V7X_LANES
    jpad = kk_ref.shape[1]

    kk_ref[:, 0:w, :] = ck_ref[...]
    vv_ref[:, 0:w, :] = cv_ref[...]
    kk_ref[:, w:w + t, :] = kn_ref[...]
    vv_ref[:, w:w + t, :] = vn_ref[...]
    kk_ref[:, w + t:jpad, :] = jnp.zeros((sb, jpad - w - t, lanes), F32)
    vv_ref[:, w + t:jpad, :] = jnp.zeros((sb, jpad - w - t, lanes), F32)
    nk_ref[...] = kk_ref[:, t:t + w, :]
    nv_ref[...] = vv_ref[:, t:t + w, :]

    kk = kk_ref[...]
    vv = vv_ref[...]
    k_sw = pltpu.roll(kk, HEAD_DIM, 2)
    v_sw = pltpu.roll(vv, HEAD_DIM, 2)
    lane_k = lax.broadcasted_iota(jnp.int32, kk.shape, 2) < HEAD_DIM
    lane_q = lax.broadcasted_iota(jnp.int32, (sb, t, lanes), 2) < HEAD_DIM
    k2 = (jnp.where(lane_k, kk, k_sw).astype(BF16), jnp.where(lane_k, k_sw, kk).astype(BF16))
    v2 = (vv.astype(BF16), v_sw.astype(BF16))

    qcols = [q_ref[:, :, cc * lanes:(cc + 1) * lanes] for cc in range(aw // lanes)]
    tq = lax.broadcasted_iota(jnp.int32, (sb, 4 * t, jpad), 1) % t
    j = lax.broadcasted_iota(jnp.int32, (sb, 4 * t, jpad), 2)
    valid = jnp.logical_and(j > tq, j <= tq + w)
    neg_inf = jnp.float32(-jnp.inf)

    ocols = []
    for g in range(N_KV_HEADS):
        lhs = _group_lhs(qcols, g, lane_q, 1)
        s = jnp.einsum('bqd,bjd->bqj', lhs, k2[g], preferred_element_type=F32)
        s = jnp.where(valid, s, neg_inf)
        heads = (4 * g, 4 * g + 2, 4 * g + 1, 4 * g + 3)
        sk = jnp.concatenate([jnp.broadcast_to(sk_ref[h:h + 1, 0:1].reshape(1, 1, 1), (sb, t, 1)) for h in heads],
                             axis=1)
        mx = jnp.maximum(jnp.max(s, axis=-1, keepdims=True), sk)
        p = jnp.exp(s - mx)
        den = jnp.sum(p, axis=-1, keepdims=True) + jnp.exp(sk - mx)
        pb = p.astype(BF16)
        o_even = jnp.einsum('bqj,bjd->bqd', pb[:, 0:2 * t], v2[g], preferred_element_type=F32) / den[:, 0:2 * t]
        o_odd = jnp.einsum('bqj,bjd->bqd', pb[:, 2 * t:4 * t], v2[1 - g], preferred_element_type=F32) / den[:, 2 * t:4 * t]
        for half in range(2):
            ocols.append(jnp.where(lane_q, o_even[:, half * t:(half + 1) * t], o_odd[:, half * t:(half + 1) * t]))
    o = jnp.concatenate(ocols, axis=2)
    out_ref[...] = _rms(o, on_ref[...].reshape(1, 1, aw)).astype(out_ref.dtype)


def _attn_cached(q3, kn3, vn3, ck, cv, sk, on):
    s, t, aw = q3.shape
    w, kvw = ck.shape[1], ck.shape[2]
    sb = min(SAMPLE_SEQ_TILE, s)
    jpad = -(-(w + t) // V7X_SUBLANES) * V7X_SUBLANES
    seq = lambda a, c: pl.BlockSpec((sb, a, c), lambda i: (i, 0, 0))
    return pl.pallas_call(
        _attn_cached_kernel,
        grid=(s // sb,),
        in_specs=[seq(t, aw), seq(t, kvw), seq(t, kvw), seq(w, kvw), seq(w, kvw), _resident(sk.shape), _resident(on.shape)],
        out_specs=[seq(t, aw), seq(w, kvw), seq(w, kvw)],
        out_shape=[jax.ShapeDtypeStruct((s, t, aw), BF16), jax.ShapeDtypeStruct((s, w, kvw), F32),
                   jax.ShapeDtypeStruct((s, w, kvw), F32)],
        scratch_shapes=[pltpu.VMEM((sb, jpad, kvw), F32), pltpu.VMEM((sb, jpad, kvw), F32)],
        compiler_params=pltpu.CompilerParams(dimension_semantics=("arbitrary",),
                                             vmem_limit_bytes=_vmem_limit(40 << 20)),
        name="attn_sample",
    )(q3, kn3, vn3, ck, cv, sk, on)


def _rope_tables(pos):
    half = HEAD_DIM // 2
    inv_freq = ROPE_THETA ** (-jnp.arange(half, dtype=F32) / half)
    ang = pos.astype(F32)[:, None] * inv_freq[None, :]
    cos = jnp.cos(ang)
    sin = jnp.sin(ang)
    reps = V7X_LANES // HEAD_DIM
    return jnp.tile(jnp.concatenate([cos, cos], axis=1), (1, reps)), jnp.tile(jnp.concatenate([-sin, sin], axis=1), (1, reps))


def _block_diag(w):
    n, c, d = w.shape
    eye = jnp.eye(n, dtype=w.dtype)
    return (w[:, :, None, :] * eye[:, None, :, None]).reshape(n * c, n * d)


def _row(v):
    return v.reshape(1, -1).astype(F32)


def kernel(x_prompt, x_sample, state_lru_h, state_conv, cache_k, cache_v, ffn1_norm, ffn1_w_gate, ffn1_w_up, ffn1_w_down, mix_norm, w_in, conv_w, conv_b, lru_wa, lru_ba, lru_wx, lru_bx, lru_lambda, q_norm, k_norm, attn_sinks, lru_out_norm, attn_out_norm, w_out, ffn2_norm, ffn2_w_gate, ffn2_w_up, ffn2_w_down):
    depth = w_in.shape[0]
    b, t, d = x_prompt.shape
    s, ts, _ = x_sample.shape
    c = lru_lambda.shape[1]
    aw = attn_out_norm.shape[1]
    kvw = N_KV_HEADS * HEAD_DIM
    win = cache_k.shape[2]
    assert t >= WINDOW and win == WINDOW

    cos_p, sin_p = _rope_tables(jnp.arange(t, dtype=jnp.int32))
    cos_s, sin_s = _rope_tables(PAST_LEN + jnp.arange(ts, dtype=jnp.int32))
    cos_s, sin_s = jnp.tile(cos_s, (s, 1)), jnp.tile(sin_s, (s, 1))
    ones = _block_diag(jnp.ones((V7X_LANES // HEAD_DIM, HEAD_DIM, HEAD_DIM), BF16))

    yp = x_prompt.reshape(b * t, d)
    ys = x_sample.reshape(s * ts, d)
    outs = [[] for _ in range(8)]
    for l in range(depth):
        ffn1 = (_row(ffn1_norm[l]), ffn1_w_gate[l].astype(BF16), ffn1_w_up[l].astype(BF16), ffn1_w_down[l].astype(BF16))
        ffn2 = (_row(ffn2_norm[l]), ffn2_w_gate[l].astype(BF16), ffn2_w_up[l].astype(BF16), ffn2_w_down[l].astype(BF16))
        win_b = w_in[l].astype(BF16)
        winl, wina = win_b[:, :2 * c], win_b[:, 2 * c:]
        wout_b = w_out[l].astype(BF16)
        wol, woa = wout_b[:c], wout_b[c:]
        lru_p = (conv_w[l], _row(conv_b[l]), (0.5 * _block_diag(lru_wa[l])).astype(BF16), _row(0.5 * lru_ba[l]),
                 (0.5 * _block_diag(lru_wx[l])).astype(BF16), _row(0.5 * lru_bx[l]), _row(lru_lambda[l]),
                 _row(lru_out_norm[l]))
        reps = V7X_LANES // HEAD_DIM
        qn = _row(jnp.tile(q_norm[l], reps))
        kn = _row(jnp.tile(k_norm[l], reps))
        sk = jnp.broadcast_to(attn_sinks[l].astype(F32)[:, None], (attn_sinks.shape[1], V7X_LANES))
        sk_rows = jnp.repeat(attn_sinks[l].astype(F32).reshape(N_KV_HEADS, -1), WINDOW, axis=1)
        a_on = _row(attn_out_norm[l])

        x1, pl_p, pa_p = _ffn_in(yp, ffn1[0], ffn1[1], ffn1[2], ffn1[3], _row(mix_norm[l]), winl, wina)
        lo, hl, cs = _lru_prompt(pl_p.reshape(b, t, 2 * c), *lru_p)
        ao, pk, pv = _attn_prompt(pa_p.reshape(b, t, aw + 2 * kvw), cos_p, sin_p, qn, kn, sk_rows, ones, a_on)
        yp = _out_ffn(lo.reshape(b * t, c), ao.reshape(b * t, aw), x1, wol, woa, *ffn2)
        outs[0].append(hl.reshape(b, c))
        outs[1].append(cs)
        outs[2].append(pk.reshape(b, WINDOW, N_KV_HEADS, HEAD_DIM))
        outs[3].append(pv.reshape(b, WINDOW, N_KV_HEADS, HEAD_DIM))

        x1s, pl_s, pa_s = _ffn_in(ys, ffn1[0], ffn1[1], ffn1[2], ffn1[3], _row(mix_norm[l]), winl, wina)
        x_tm = pl_s.reshape(s, ts, 2 * c).transpose(1, 0, 2)
        cst_tm = state_conv[l].transpose(1, 0, 2)
        lo_tm, hls, cs_tm = _lru_sample(x_tm, cst_tm, state_lru_h[l], *lru_p)
        q_s, k_s = _qk_prep(pa_s, cos_s, sin_s, qn, kn, ones, aw)
        v_s = pa_s[:, aw + kvw:]
        aos, nk, nv = _attn_cached(q_s.reshape(s, ts, aw), k_s.reshape(s, ts, kvw), v_s.reshape(s, ts, kvw),
                                   cache_k[l].reshape(s, win, kvw), cache_v[l].reshape(s, win, kvw), sk, a_on)
        ys = _out_ffn(lo_tm.transpose(1, 0, 2).reshape(s * ts, c), aos.reshape(s * ts, aw), x1s, wol, woa, *ffn2)
        outs[4].append(hls)
        outs[5].append(cs_tm.transpose(1, 0, 2))
        outs[6].append(nk.reshape(s, win, N_KV_HEADS, HEAD_DIM))
        outs[7].append(nv.reshape(s, win, N_KV_HEADS, HEAD_DIM))

    st = [jnp.stack(o) for o in outs]
    return (yp.reshape(b, t, d), ys.reshape(s, ts, d), st[0], st[1], st[2], st[3], st[4], st[5], st[6], st[7])
```

```python
import functools

import jax
import jax.numpy as jnp
from jax import lax
from jax.experimental import pallas as pl
from jax.experimental.pallas import tpu as pltpu

F32 = jnp.float32
BF16 = jnp.bfloat16

NORM_EPS = 1e-6
LRU_C = 8.0
CONV_W = 4
HEAD_DIM = 64
N_KV_HEADS = 2
WINDOW = 128
ROPE_THETA = 10000.0
PAST_LEN = 16384

V7X_LANES = 128
V7X_SUBLANES = 8
V7X_MXU_DIM = 256
V7X_VMEM_BYTES = 64 * 1024 * 1024

TOKEN_TILE = 512
SAMPLE_SEQ_TILE = 16
FF_CHUNK = V7X_MXU_DIM


def _vmem_limit(nbytes):
    return int(min(nbytes, V7X_VMEM_BYTES - 4 * 1024 * 1024))


def _resident(shape):
    nd = len(shape)
    return pl.BlockSpec(shape, lambda *_: (0,) * nd, pipeline_mode=pl.Buffered(1))


def _rms(x, g):
    ms = jnp.mean(x * x, axis=-1, keepdims=True)
    return (x * lax.rsqrt(ms + NORM_EPS)) * g


def _ffn_accumulate(xn_ref, wg_ref, wu_ref, wd_ref, acc_ref):
    acc_ref[...] = jnp.zeros_like(acc_ref)
    for c0 in range(0, wg_ref.shape[1], FF_CHUNK):
        xn = xn_ref[...]
        g = jnp.dot(xn, wg_ref[:, c0:c0 + FF_CHUNK], preferred_element_type=F32)
        u = jnp.dot(xn, wu_ref[:, c0:c0 + FF_CHUNK], preferred_element_type=F32)
        a = (jax.nn.silu(g) * u).astype(BF16)
        acc_ref[...] += jnp.dot(a, wd_ref[c0:c0 + FF_CHUNK, :], preferred_element_type=F32)


def _ffn_in_tile(x_ref, ffn_refs, gm_ref, winl_ref, wina_ref, x1_ref, pl_ref, pa_ref, xn_ref, acc_ref):
    g1_ref, wg_ref, wu_ref, wd_ref = ffn_refs
    xn_ref[...] = _rms(x_ref[...], g1_ref[...]).astype(BF16)
    _ffn_accumulate(xn_ref, wg_ref, wu_ref, wd_ref, acc_ref)
    x1 = x_ref[...] + 0.5 * acc_ref[...]
    x1_ref[...] = x1
    xn_ref[...] = _rms(x1, gm_ref[...]).astype(BF16)
    pl_ref[...] = jnp.dot(xn_ref[...], winl_ref[...], preferred_element_type=F32)
    pa_ref[...] = jnp.dot(xn_ref[...], wina_ref[...], preferred_element_type=F32)


def _lru_gates(xc, gate_w, lam_ref):
    wa_ref, ba_ref, wx_ref, bx_ref = gate_w
    xcb = xc.astype(BF16)
    r = 0.5 * jnp.tanh(jnp.dot(xcb, wa_ref[...], preferred_element_type=F32) + ba_ref[...]) + 0.5
    i = 0.5 * jnp.tanh(jnp.dot(xcb, wx_ref[...], preferred_element_type=F32) + bx_ref[...]) + 0.5
    log_a = (-LRU_C * jax.nn.softplus(-lam_ref[...])) * r
    a = jnp.exp(log_a)
    w = -jnp.tanh(log_a) * (a * a + 1.0)
    root = jnp.where(w > 0.0, w * lax.rsqrt(w), 0.0)
    u = root * (i * xc)
    return a, u


def _lru_tile(p_ref, first, lru_refs, out_ref, hl_ref, cs_ref, xh_ref, h_ref, a_ref, u_ref):
    cw_ref, cb_ref, wa_ref, ba_ref, wx_ref, bx_ref, lam_ref, on_ref = lru_refs
    tt = p_ref.shape[0]
    c = out_ref.shape[1]
    hist = V7X_SUBLANES

    xh_ref[0:hist, :] = jnp.where(first, jnp.zeros((hist, c), F32), xh_ref[0:hist, :])
    x = p_ref[:, 0:c]
    xh_ref[hist:hist + tt, :] = x
    w = cw_ref[...]
    xc = cb_ref[...]
    for j in range(CONV_W):
        off = hist - (CONV_W - 1) + j
        xc = xc + xh_ref[off:off + tt, :] * w[j:j + 1, :]
    xh_ref[0:hist, :] = x[tt - hist:tt, :]

    a, u = _lru_gates(xc, (wa_ref, ba_ref, wx_ref, bx_ref), lam_ref)

    groups = tt // V7X_SUBLANES
    a3 = a.reshape(groups, V7X_SUBLANES, c)
    u3 = u.reshape(groups, V7X_SUBLANES, c)
    sub = lax.broadcasted_iota(jnp.int32, a3.shape, 1)
    k = 1
    while k < V7X_SUBLANES:
        m = sub >= k
        a_prev = pltpu.roll(a3, k, 1)
        u_prev = pltpu.roll(u3, k, 1)
        u3 = jnp.where(m, a3 * u_prev + u3, u3)
        a3 = jnp.where(m, a3 * a_prev, a3)
        k *= 2
    a_ref[...] = a3
    u_ref[...] = u3

    h = jnp.where(first, jnp.zeros((1, c), F32), h_ref[...])
    for g in range(groups):
        hr = a_ref[g] * h + u_ref[g]
        u_ref[g] = hr
        h = hr[V7X_SUBLANES - 1:V7X_SUBLANES, :]
    h_ref[...] = h

    hs = u_ref[...].reshape(tt, c)
    y = hs * jax.nn.gelu(p_ref[:, c:2 * c])
    out_ref[...] = _rms(y, on_ref[...]).astype(out_ref.dtype)
    hl_ref[0] = h
    cs_ref[0] = x[tt - (CONV_W - 1):tt, :]


def _lru_sample_kernel(x_ref, cst_ref, h0_ref, cw_ref, cb_ref, wa_ref, ba_ref, wx_ref, bx_ref, lam_ref, on_ref,
                       out_ref, hl_ref, cs_ref):
    t = x_ref.shape[0]
    c = out_ref.shape[2]
    w = cw_ref[...]
    rows = [cst_ref[j] for j in range(CONV_W - 1)] + [x_ref[j, :, 0:c] for j in range(t)]
    h = h0_ref[...]
    for s in range(t):
        xc = cb_ref[...]
        for j in range(CONV_W):
            xc = xc + rows[s + j] * w[j:j + 1, :]
        a, u = _lru_gates(xc, (wa_ref, ba_ref, wx_ref, bx_ref), lam_ref)
        h = a * h + u
        y = h * jax.nn.gelu(x_ref[s, :, c:2 * c])
        out_ref[s] = _rms(y, on_ref[...]).astype(out_ref.dtype)
    hl_ref[...] = h
    for j in range(CONV_W - 1):
        cs_ref[j] = rows[t + j]


def _lru_sample(x_tm, cst_tm, h0, cw, cb, wa, ba, wx, bx, lam, on):
    t, s, w2 = x_tm.shape
    c = w2 // 2
    return pl.pallas_call(
        _lru_sample_kernel,
        out_shape=[jax.ShapeDtypeStruct((t, s, c), BF16), jax.ShapeDtypeStruct((s, c), F32),
                   jax.ShapeDtypeStruct((CONV_W - 1, s, c), F32)],
        compiler_params=pltpu.CompilerParams(vmem_limit_bytes=_vmem_limit(32 << 20)),
        name="lru_sample",
    )(x_tm, cst_tm, h0, cw, cb, wa, ba, wx, bx, lam, on)


def _head_norm_rope(xc, gain, cos, sin, ones_ref):
    x2 = xc * xc
    hi = x2.astype(BF16)
    lo = (x2 - hi.astype(F32)).astype(BF16)
    ss = jnp.dot(hi, ones_ref[...], preferred_element_type=F32) + jnp.dot(lo, ones_ref[...], preferred_element_type=F32)
    xn = (xc * lax.rsqrt(ss * (1.0 / HEAD_DIM) + NORM_EPS)) * gain
    lane = lax.broadcasted_iota(jnp.int32, xn.shape, xn.ndim - 1)
    first_half = (lane & (HEAD_DIM // 2)) == 0
    ax = xn.ndim - 1
    swapped = jnp.where(first_half, pltpu.roll(xn, V7X_LANES - HEAD_DIM // 2, ax), pltpu.roll(xn, HEAD_DIM // 2, ax))
    return xn * cos + swapped * sin


def _group_lhs(qcols, g, lane_lo, axis):
    zero = jnp.zeros_like(qcols[0])
    parts = [jnp.where(lane_lo, qcols[2 * g], zero), jnp.where(lane_lo, qcols[2 * g + 1], zero),
             jnp.where(lane_lo, zero, qcols[2 * g]), jnp.where(lane_lo, zero, qcols[2 * g + 1])]
    return jnp.concatenate(parts, axis=axis).astype(BF16)


def _attn_tile(pa_ref, first, cos_ref, sin_ref, attn_refs, out_ref, pk_ref, pv_ref, k2_ref, vt_ref):
    qn_ref, kn_ref, sk_ref, ones_ref, on_ref = attn_refs
    tq = pa_ref.shape[0]
    blk = WINDOW
    nblk = tq // blk
    lanes = V7X_LANES
    aw = out_ref.shape[1]
    ncol = aw // lanes
    heads_per_group = 2 * ncol // N_KV_HEADS

    cos = cos_ref[...]
    sin = sin_ref[...]
    cols = [pa_ref[:, cc * lanes:(cc + 1) * lanes] for cc in range(ncol + 1)]
    sq = [x * x for x in cols]
    hi = [x.astype(BF16) for x in sq]
    lo = [(x - h.astype(F32)).astype(BF16) for x, h in zip(sq, hi)]
    ss = (jnp.dot(jnp.concatenate(hi, axis=0), ones_ref[...], preferred_element_type=F32)
          + jnp.dot(jnp.concatenate(lo, axis=0), ones_ref[...], preferred_element_type=F32))
    lane = lax.broadcasted_iota(jnp.int32, (tq, lanes), 1)
    first_half = (lane & (HEAD_DIM // 2)) == 0
    lane_lo = lane < HEAD_DIM
    rot = []
    for cc, x in enumerate(cols):
        gain = qn_ref[...] if cc < ncol else kn_ref[...]
        xn = (x * lax.rsqrt(ss[cc * tq:(cc + 1) * tq] * (1.0 / HEAD_DIM) + NORM_EPS)) * gain
        swapped = jnp.where(first_half, pltpu.roll(xn, lanes - HEAD_DIM // 2, 1), pltpu.roll(xn, HEAD_DIM // 2, 1))
        rot.append(xn * cos + swapped * sin)
    k = rot[ncol]
    v = pa_ref[:, aw + lanes:aw + 2 * lanes]

    zero = jnp.zeros((tq, lanes), F32)
    qh = []
    for cc in range(ncol):
        qs = rot[cc] * (HEAD_DIM ** -0.5)
        qh.append(jnp.where(lane_lo, qs, zero).astype(BF16))
        qh.append(jnp.where(lane_lo, zero, qs).astype(BF16))

    k_sw = pltpu.roll(k, HEAD_DIM, 1)
    k2_ref[0, blk:blk + tq, :] = jnp.where(lane_lo, k, k_sw).astype(BF16)
    k2_ref[1, blk:blk + tq, :] = jnp.where(lane_lo, k_sw, k).astype(BF16)
    for jb in range(nblk):
        vt_ref[:, (jb + 1) * blk:(jb + 2) * blk] = v[jb * blk:(jb + 1) * blk].T.astype(BF16)

    ncolq = heads_per_group * blk
    kj = lax.broadcasted_iota(jnp.int32, (blk, ncolq), 0)
    qi = lax.broadcasted_iota(jnp.int32, (blk, ncolq), 1) % blk
    tri = kj <= qi
    no_prev = jnp.where(first, jnp.float32(-jnp.inf), jnp.float32(0.0))

    for jb in range(nblk):
        r0 = jb * blk
        ocols = []
        for g in range(N_KV_HEADS):
            lhs = jnp.concatenate([q[r0:r0 + blk] for q in qh[g * heads_per_group:(g + 1) * heads_per_group]], axis=0)
            st = lax.dot_general(k2_ref[g, r0:r0 + 2 * blk, :], lhs, (((1,), (1,)), ((), ())),
                                 preferred_element_type=F32)
            s_prev = st[0:blk]
            if jb == 0:
                s_prev = s_prev + no_prev
            m = jnp.where(tri, st[blk:2 * blk], s_prev)
            sk = sk_ref[g:g + 1, :]
            mx = jnp.maximum(jnp.max(m, axis=0, keepdims=True), sk)
            p = jnp.exp(m - mx)
            den = jnp.sum(p, axis=0, keepdims=True) + jnp.exp(sk - mx)
            pz = jnp.zeros_like(p)
            pt = jnp.concatenate([jnp.where(tri, pz, p), jnp.where(tri, p, pz)], axis=0).astype(BF16)
            ot = jnp.dot(vt_ref[:, r0:r0 + 2 * blk], pt, preferred_element_type=F32)
            og = ot[g * HEAD_DIM:(g + 1) * HEAD_DIM, :] * (1.0 / den)
            for half in range(heads_per_group // 2):
                pair = jnp.concatenate([og[:, (2 * half) * blk:(2 * half + 1) * blk],
                                        og[:, (2 * half + 1) * blk:(2 * half + 2) * blk]], axis=0)
                ocols.append(pair.T)
        o = jnp.concatenate(ocols, axis=1)
        out_ref[r0:r0 + blk, :] = _rms(o, on_ref[...]).astype(out_ref.dtype)

    k2_ref[:, 0:blk, :] = k2_ref[:, tq:tq + blk, :]
    vt_ref[:, 0:blk] = vt_ref[:, tq:tq + blk]
    pk_ref[0] = k[tq - blk:tq]
    pv_ref[0] = v[tq - blk:tq]


N_FFN_IN = 7
N_LRU = 8
N_ATTN = 5


def _mixer_in_kernel(*refs, tiles_per_seq):
    x_ref, cos_ref, sin_ref = refs[0:3]
    p0 = 3
    g1_ref, wg_ref, wu_ref, wd_ref, gm_ref, winl_ref, wina_ref = refs[p0:p0 + N_FFN_IN]
    lru_refs = refs[p0 + N_FFN_IN:p0 + N_FFN_IN + N_LRU]
    attn_refs = refs[p0 + N_FFN_IN + N_LRU:p0 + N_FFN_IN + N_LRU + N_ATTN]
    o0 = p0 + N_FFN_IN + N_LRU + N_ATTN
    x1_ref, lo_ref, ao_ref, hl_ref, cs_ref, pk_ref, pv_ref = refs[o0:o0 + 7]
    xn_ref, acc_ref, pl_ref, pa_ref, xh_ref, h_ref, a_ref, u_ref, k2_ref, vt_ref = refs[o0 + 7:]

    i = pl.program_id(0)

    @pl.when(i == 0)
    def _():
        pl_ref[...] = jnp.zeros_like(pl_ref)
        pa_ref[...] = jnp.zeros_like(pa_ref)
        xh_ref[...] = jnp.zeros_like(xh_ref)
        h_ref[...] = jnp.zeros_like(h_ref)
        k2_ref[...] = jnp.zeros_like(k2_ref)
        vt_ref[...] = jnp.zeros_like(vt_ref)

    first = lax.rem(jnp.maximum(i - 1, 0), tiles_per_seq) == 0
    _lru_tile(pl_ref, first, lru_refs, lo_ref, hl_ref, cs_ref, xh_ref, h_ref, a_ref, u_ref)
    _attn_tile(pa_ref, first, cos_ref, sin_ref, attn_refs, ao_ref, pk_ref, pv_ref, k2_ref, vt_ref)

    _ffn_in_tile(x_ref, (g1_ref, wg_ref, wu_ref, wd_ref), gm_ref, winl_ref, wina_ref,
                 x1_ref, pl_ref, pa_ref, xn_ref, acc_ref)


def _mixer_in(x, seq_len, cos, sin, ffn_in_p, lru_p, attn_p):
    n, d = x.shape
    tm = TOKEN_TILE
    assert seq_len % tm == 0 and n % seq_len == 0 and tm % WINDOW == 0
    nseq = n // seq_len
    tps = seq_len // tm
    ntile = n // tm
    winl, wina = ffn_in_p[5], ffn_in_p[6]
    w2, wa = winl.shape[1], wina.shape[1]
    c = w2 // 2
    aw = attn_p[4].shape[1]
    kvw = N_KV_HEADS * HEAD_DIM
    cur = lambda i: jnp.minimum(i, ntile - 1)
    prev = lambda i: jnp.maximum(i - 1, 0)
    params = list(ffn_in_p) + list(lru_p) + list(attn_p)
    weights = sum(p.size * p.dtype.itemsize for p in params)
    tiles = 2 * 4 * tm * 2 * d + 2 * 2 * tm * (c + aw) + 4 * tm * (w2 + wa) + tm * d * (2 + 4) + 3 * 4 * tm * c
    return pl.pallas_call(
        functools.partial(_mixer_in_kernel, tiles_per_seq=tps),
        grid=(ntile + 1,),
        in_specs=[pl.BlockSpec((tm, d), lambda i: (cur(i), 0)),
                  pl.BlockSpec((tm, V7X_LANES), lambda i: (lax.rem(prev(i), tps), 0)),
                  pl.BlockSpec((tm, V7X_LANES), lambda i: (lax.rem(prev(i), tps), 0))]
                 + [_resident(p.shape) for p in params],
        out_specs=[pl.BlockSpec((tm, d), lambda i: (cur(i), 0)),
                   pl.BlockSpec((tm, c), lambda i: (prev(i), 0)),
                   pl.BlockSpec((tm, aw), lambda i: (prev(i), 0)),
                   pl.BlockSpec((1, 1, c), lambda i: (prev(i) // tps, 0, 0)),
                   pl.BlockSpec((1, CONV_W - 1, c), lambda i: (prev(i) // tps, 0, 0)),
                   pl.BlockSpec((1, WINDOW, kvw), lambda i: (prev(i) // tps, 0, 0)),
                   pl.BlockSpec((1, WINDOW, kvw), lambda i: (prev(i) // tps, 0, 0))],
        out_shape=[jax.ShapeDtypeStruct((n, d), F32), jax.ShapeDtypeStruct((n, c), BF16),
                   jax.ShapeDtypeStruct((n, aw), BF16), jax.ShapeDtypeStruct((nseq, 1, c), F32),
                   jax.ShapeDtypeStruct((nseq, CONV_W - 1, c), F32), jax.ShapeDtypeStruct((nseq, WINDOW, kvw), F32),
                   jax.ShapeDtypeStruct((nseq, WINDOW, kvw), F32)],
        scratch_shapes=[pltpu.VMEM((tm, d), BF16), pltpu.VMEM((tm, d), F32),
                        pltpu.VMEM((tm, w2), F32), pltpu.VMEM((tm, wa), F32),
                        pltpu.VMEM((tm + V7X_SUBLANES, c), F32), pltpu.VMEM((1, c), F32),
                        pltpu.VMEM((tm // V7X_SUBLANES, V7X_SUBLANES, c), F32),
                        pltpu.VMEM((tm // V7X_SUBLANES, V7X_SUBLANES, c), F32),
                        pltpu.VMEM((N_KV_HEADS, WINDOW + tm, V7X_LANES), BF16),
                        pltpu.VMEM((V7X_LANES, WINDOW + tm), BF16)],
        compiler_params=pltpu.CompilerParams(dimension_semantics=("arbitrary",),
                                             vmem_limit_bytes=_vmem_limit(weights + tiles + (20 << 20))),
        name="mixer_in",
    )(x, cos, sin, *params)


def _ffn_in_kernel(x_ref, g1_ref, wg_ref, wu_ref, wd_ref, gm_ref, winl_ref, wina_ref,
                   x1_ref, pl_ref, pa_ref, xn_ref, acc_ref):
    _ffn_in_tile(x_ref, (g1_ref, wg_ref, wu_ref, wd_ref), gm_ref, winl_ref, wina_ref,
                 x1_ref, pl_ref, pa_ref, xn_ref, acc_ref)


def _ffn_in(x, g1, wg, wu, wd, gm, winl, wina):
    n, d = x.shape
    tm = min(TOKEN_TILE, n)
    wl, wa = winl.shape[1], wina.shape[1]
    row = lambda w: pl.BlockSpec((tm, w), lambda i: (i, 0))
    weights = 2 * (wg.size + wu.size + wd.size + winl.size + wina.size)
    tiles = 2 * 4 * tm * (2 * d + wl + wa) + tm * d * (2 + 4) + 4 * 4 * tm * max(FF_CHUNK, wl)
    return pl.pallas_call(
        _ffn_in_kernel,
        grid=(n // tm,),
        in_specs=[row(d), _resident(g1.shape), _resident(wg.shape), _resident(wu.shape), _resident(wd.shape),
                  _resident(gm.shape), _resident(winl.shape), _resident(wina.shape)],
        out_specs=[row(d), row(wl), row(wa)],
        out_shape=[jax.ShapeDtypeStruct((n, d), F32), jax.ShapeDtypeStruct((n, wl), F32),
                   jax.ShapeDtypeStruct((n, wa), F32)],
        scratch_shapes=[pltpu.VMEM((tm, d), BF16), pltpu.VMEM((tm, d), F32)],
        compiler_params=pltpu.CompilerParams(dimension_semantics=("arbitrary",),
                                             vmem_limit_bytes=_vmem_limit(weights + tiles + (8 << 20))),
        name="ffn_in",
    )(x, g1, wg, wu, wd, gm, winl, wina)


def _out_ffn_kernel(lo_ref, ao_ref, x1_ref, wol_ref, woa_ref, g2_ref, wg_ref, wu_ref, wd_ref,
                    out_ref, xn_ref, acc_ref):
    y = jnp.dot(lo_ref[...], wol_ref[...], preferred_element_type=F32)
    y = y + jnp.dot(ao_ref[...], woa_ref[...], preferred_element_type=F32)
    x2 = x1_ref[...] + y
    out_ref[...] = x2
    xn_ref[...] = _rms(x2, g2_ref[...]).astype(BF16)
    _ffn_accumulate(xn_ref, wg_ref, wu_ref, wd_ref, acc_ref)
    out_ref[...] = out_ref[...] + 0.5 * acc_ref[...]


def _out_ffn(lo, ao, x1, wol, woa, g2, wg, wu, wd):
    n, d = x1.shape
    tm = min(TOKEN_TILE, n)
    row = lambda w: pl.BlockSpec((tm, w), lambda i: (i, 0))
    weights = 2 * (wg.size + wu.size + wd.size + wol.size + woa.size)
    tiles = 2 * tm * (2 * lo.shape[1] + 2 * ao.shape[1] + 8 * d) + tm * d * (2 + 4) + 4 * 4 * tm * d
    return pl.pallas_call(
        _out_ffn_kernel,
        grid=(n // tm,),
        in_specs=[row(lo.shape[1]), row(ao.shape[1]), row(d), _resident(wol.shape), _resident(woa.shape),
                  _resident(g2.shape), _resident(wg.shape), _resident(wu.shape), _resident(wd.shape)],
        out_specs=row(d),
        out_shape=jax.ShapeDtypeStruct((n, d), F32),
        scratch_shapes=[pltpu.VMEM((tm, d), BF16), pltpu.VMEM((tm, d), F32)],
        compiler_params=pltpu.CompilerParams(dimension_semantics=("arbitrary",),
                                             vmem_limit_bytes=_vmem_limit(weights + tiles + (8 << 20))),
        name="out_ffn",
    )(lo, ao, x1, wol, woa, g2, wg, wu, wd)


def _qk_prep_kernel(pa_ref, cos_ref, sin_ref, qn_ref, kn_ref, ones_ref, q_ref, k_ref):
    lanes = V7X_LANES
    aw = q_ref.shape[1]
    cos = cos_ref[...]
    sin = sin_ref[...]
    for cc in range(aw // lanes):
        q_ref[:, cc * lanes:(cc + 1) * lanes] = _head_norm_rope(
            pa_ref[:, cc * lanes:(cc + 1) * lanes], qn_ref[...], cos, sin, ones_ref) * (HEAD_DIM ** -0.5)
    k_ref[...] = _head_norm_rope(pa_ref[:, aw:aw + lanes], kn_ref[...], cos, sin, ones_ref)


def _qk_prep(pa, cos, sin, qn, kn, ones, aw):
    n = pa.shape[0]
    return pl.pallas_call(
        _qk_prep_kernel,
        out_shape=[jax.ShapeDtypeStruct((n, aw), F32), jax.ShapeDtypeStruct((n, N_KV_HEADS * HEAD_DIM), F32)],
        compiler_params=pltpu.CompilerParams(vmem_limit_bytes=_vmem_limit(32 << 20)),
        name="qk_prep_sample",
    )(pa, cos, sin, qn, kn, ones)


def _attn_cached_kernel(q_ref, kn_ref, vn_ref, ck_ref, cv_ref, sk_ref, on_ref,
                        out_ref, nk_ref, nv_ref, kk_ref, vv_ref):
    sb, t, aw = q_ref.shape
    w = ck_ref.shape[1]
    lanes = V7X_LANES
    jpad = kk_ref.shape[1]

    kk_ref[:, 0:w, :] = ck_ref[...]
    vv_ref[:, 0:w, :] = cv_ref[...]
    kk_ref[:, w:w + t, :] = kn_ref[...]
    vv_ref[:, w:w + t, :] = vn_ref[...]
    kk_ref[:, w + t:jpad, :] = jnp.zeros((sb, jpad - w - t, lanes), F32)
    vv_ref[:, w + t:jpad, :] = jnp.zeros((sb, jpad - w - t, lanes), F32)
    nk_ref[...] = kk_ref[:, t:t + w, :]
    nv_ref[...] = vv_ref[:, t:t + w, :]

    kk = kk_ref[...]
    vv = vv_ref[...]
    k_sw = pltpu.roll(kk, HEAD_DIM, 2)
    v_sw = pltpu.roll(vv, HEAD_DIM, 2)
    lane_k = lax.broadcasted_iota(jnp.int32, kk.shape, 2) < HEAD_DIM
    lane_q = lax.broadcasted_iota(jnp.int32, (sb, t, lanes), 2) < HEAD_DIM
    k2 = (jnp.where(lane_k, kk, k_sw).astype(BF16), jnp.where(lane_k, k_sw, kk).astype(BF16))
    v2 = (vv.astype(BF16), v_sw.astype(BF16))

    qcols = [q_ref[:, :, cc * lanes:(cc + 1) * lanes] for cc in range(aw // lanes)]
    tq = lax.broadcasted_iota(jnp.int32, (sb, 4 * t, jpad), 1) % t
    j = lax.broadcasted_iota(jnp.int32, (sb, 4 * t, jpad), 2)
    valid = jnp.logical_and(j > tq, j <= tq + w)
    neg_inf = jnp.float32(-jnp.inf)

    ocols = []
    for g in range(N_KV_HEADS):
        lhs = _group_lhs(qcols, g, lane_q, 1)
        s = jnp.einsum('bqd,bjd->bqj', lhs, k2[g], preferred_element_type=F32)
        s = jnp.where(valid, s, neg_inf)
        heads = (4 * g, 4 * g + 2, 4 * g + 1, 4 * g + 3)
        sk = jnp.concatenate([jnp.broadcast_to(sk_ref[h:h + 1, 0:1].reshape(1, 1, 1), (sb, t, 1)) for h in heads],
                             axis=1)
        mx = jnp.maximum(jnp.max(s, axis=-1, keepdims=True), sk)
        p = jnp.exp(s - mx)
        den = jnp.sum(p, axis=-1, keepdims=True) + jnp.exp(sk - mx)
        pb = p.astype(BF16)
        o_even = jnp.einsum('bqj,bjd->bqd', pb[:, 0:2 * t], v2[g], preferred_element_type=F32) / den[:, 0:2 * t]
        o_odd = jnp.einsum('bqj,bjd->bqd', pb[:, 2 * t:4 * t], v2[1 - g], preferred_element_type=F32) / den[:, 2 * t:4 * t]
        for half in range(2):
            ocols.append(jnp.where(lane_q, o_even[:, half * t:(half + 1) * t], o_odd[:, half * t:(half + 1) * t]))
    o = jnp.concatenate(ocols, axis=2)
    out_ref[...] = _rms(o, on_ref[...].reshape(1, 1, aw)).astype(out_ref.dtype)


def _attn_cached(q3, kn3, vn3, ck, cv, sk, on):
    s, t, aw = q3.shape
    w, kvw = ck.shape[1], ck.shape[2]
    sb = min(SAMPLE_SEQ_TILE, s)
    jpad = -(-(w + t) // V7X_SUBLANES) * V7X_SUBLANES
    seq = lambda a, c: pl.BlockSpec((sb, a, c), lambda i: (i, 0, 0))
    return pl.pallas_call(
        _attn_cached_kernel,
        grid=(s // sb,),
        in_specs=[seq(t, aw), seq(t, kvw), seq(t, kvw), seq(w, kvw), seq(w, kvw), _resident(sk.shape), _resident(on.shape)],
        out_specs=[seq(t, aw), seq(w, kvw), seq(w, kvw)],
        out_shape=[jax.ShapeDtypeStruct((s, t, aw), BF16), jax.ShapeDtypeStruct((s, w, kvw), F32),
                   jax.ShapeDtypeStruct((s, w, kvw), F32)],
        scratch_shapes=[pltpu.VMEM((sb, jpad, kvw), F32), pltpu.VMEM((sb, jpad, kvw), F32)],
        compiler_params=pltpu.CompilerParams(dimension_semantics=("arbitrary",),
                                             vmem_limit_bytes=_vmem_limit(40 << 20)),
        name="attn_sample",
    )(q3, kn3, vn3, ck, cv, sk, on)


def _rope_tables(pos):
    half = HEAD_DIM // 2
    inv_freq = ROPE_THETA ** (-jnp.arange(half, dtype=F32) / half)
    ang = pos.astype(F32)[:, None] * inv_freq[None, :]
    cos = jnp.cos(ang)
    sin = jnp.sin(ang)
    reps = V7X_LANES // HEAD_DIM
    return jnp.tile(jnp.concatenate([cos, cos], axis=1), (1, reps)), jnp.tile(jnp.concatenate([-sin, sin], axis=1), (1, reps))


def _block_diag(w):
    n, c, d = w.shape
    eye = jnp.eye(n, dtype=w.dtype)
    return (w[:, :, None, :] * eye[:, None, :, None]).reshape(n * c, n * d)


def _row(v):
    return v.reshape(1, -1).astype(F32)


def kernel(x_prompt, x_sample, state_lru_h, state_conv, cache_k, cache_v, ffn1_norm, ffn1_w_gate, ffn1_w_up, ffn1_w_down, mix_norm, w_in, conv_w, conv_b, lru_wa, lru_ba, lru_wx, lru_bx, lru_lambda, q_norm, k_norm, attn_sinks, lru_out_norm, attn_out_norm, w_out, ffn2_norm, ffn2_w_gate, ffn2_w_up, ffn2_w_down):
    depth = w_in.shape[0]
    b, t, d = x_prompt.shape
    s, ts, _ = x_sample.shape
    c = lru_lambda.shape[1]
    aw = attn_out_norm.shape[1]
    kvw = N_KV_HEADS * HEAD_DIM
    win = cache_k.shape[2]
    assert t >= WINDOW and win == WINDOW

    cos_p, sin_p = _rope_tables(jnp.arange(t, dtype=jnp.int32))
    cos_s, sin_s = _rope_tables(PAST_LEN + jnp.arange(ts, dtype=jnp.int32))
    cos_s, sin_s = jnp.tile(cos_s, (s, 1)), jnp.tile(sin_s, (s, 1))
    ones = _block_diag(jnp.ones((V7X_LANES // HEAD_DIM, HEAD_DIM, HEAD_DIM), BF16))

    yp = x_prompt.reshape(b * t, d)
    ys = x_sample.reshape(s * ts, d)
    outs = [[] for _ in range(8)]
    for l in range(depth):
        ffn1 = (_row(ffn1_norm[l]), ffn1_w_gate[l].astype(BF16), ffn1_w_up[l].astype(BF16), ffn1_w_down[l].astype(BF16))
        ffn2 = (_row(ffn2_norm[l]), ffn2_w_gate[l].astype(BF16), ffn2_w_up[l].astype(BF16), ffn2_w_down[l].astype(BF16))
        win_b = w_in[l].astype(BF16)
        ffn_in_p = ffn1 + (_row(mix_norm[l]), win_b[:, :2 * c], win_b[:, 2 * c:])
        wout_b = w_out[l].astype(BF16)
        wol, woa = wout_b[:c], wout_b[c:]
        lru_p = (conv_w[l], _row(conv_b[l]), (0.5 * _block_diag(lru_wa[l])).astype(BF16), _row(0.5 * lru_ba[l]),
                 (0.5 * _block_diag(lru_wx[l])).astype(BF16), _row(0.5 * lru_bx[l]), _row(lru_lambda[l]),
                 _row(lru_out_norm[l]))
        reps = V7X_LANES // HEAD_DIM
        qn = _row(jnp.tile(q_norm[l], reps))
        kn = _row(jnp.tile(k_norm[l], reps))
        sk = jnp.broadcast_to(attn_sinks[l].astype(F32)[:, None], (attn_sinks.shape[1], V7X_LANES))
        sk_rows = jnp.repeat(attn_sinks[l].astype(F32).reshape(N_KV_HEADS, -1), WINDOW, axis=1)
        a_on = _row(attn_out_norm[l])

        x1, lo, ao, hl, cs, pk, pv = _mixer_in(yp, t, cos_p, sin_p, ffn_in_p, lru_p, (qn, kn, sk_rows, ones, a_on))
        yp = _out_ffn(lo, ao, x1, wol, woa, *ffn2)
        outs[0].append(hl.reshape(b, c))
        outs[1].append(cs)
        outs[2].append(pk.reshape(b, WINDOW, N_KV_HEADS, HEAD_DIM))
        outs[3].append(pv.reshape(b, WINDOW, N_KV_HEADS, HEAD_DIM))

        x1s, pl_s, pa_s = _ffn_in(ys, *ffn_in_p)
        x_tm = pl_s.reshape(s, ts, 2 * c).transpose(1, 0, 2)
        cst_tm = state_conv[l].transpose(1, 0, 2)
        lo_tm, hls, cs_tm = _lru_sample(x_tm, cst_tm, state_lru_h[l], *lru_p)
        q_s, k_s = _qk_prep(pa_s, cos_s, sin_s, qn, kn, ones, aw)
        v_s = pa_s[:, aw + kvw:]
        aos, nk, nv = _attn_cached(q_s.reshape(s, ts, aw), k_s.reshape(s, ts, kvw), v_s.reshape(s, ts, kvw),
                                   cache_k[l].reshape(s, win, kvw), cache_v[l].reshape(s, win, kvw), sk, a_on)
        ys = _out_ffn(lo_tm.transpose(1, 0, 2).reshape(s * ts, c), aos.reshape(s * ts, aw), x1s, wol, woa, *ffn2)
        outs[4].append(hls)
        outs[5].append(cs_tm.transpose(1, 0, 2))
        outs[6].append(nk.reshape(s, win, N_KV_HEADS, HEAD_DIM))
        outs[7].append(nv.reshape(s, win, N_KV_HEADS, HEAD_DIM))

    st = [jnp.stack(o) for o in outs]
    return (yp.reshape(b, t, d), ys.reshape(s, ts, d), st[0], st[1], st[2], st[3], st[4], st[5], st[6], st[7])
```

```python
import functools

import jax
import jax.numpy as jnp
from jax import lax
from jax.experimental import pallas as pl
from jax.experimental.pallas import tpu as pltpu

F32 = jnp.float32
BF16 = jnp.bfloat16

NORM_EPS = 1e-6
LRU_C = 8.0
CONV_W = 4
HEAD_DIM = 64
N_KV_HEADS = 2
WINDOW = 128
ROPE_THETA = 10000.0
PAST_LEN = 16384

V7X_LANES = 128
V7X_SUBLANES = 8
V7X_MXU_DIM = 256
V7X_VMEM_BYTES = 64 * 1024 * 1024

TOKEN_TILE = 512
SAMPLE_SEQ_TILE = 16
FF_CHUNK = V7X_MXU_DIM


def _vmem_limit(nbytes):
    return int(min(nbytes, V7X_VMEM_BYTES - 4 * 1024 * 1024))


def _resident(shape):
    nd = len(shape)
    return pl.BlockSpec(shape, lambda *_: (0,) * nd, pipeline_mode=pl.Buffered(1))


def _rms(x, g):
    ms = jnp.mean(x * x, axis=-1, keepdims=True)
    return (x * lax.rsqrt(ms + NORM_EPS)) * g


def _ffn(xn_ref, wg_ref, wu_ref, wd_ref, act_ref, after_chunk=None):
    for ci, c0 in enumerate(range(0, wg_ref.shape[1], FF_CHUNK)):
        xn = xn_ref[...]
        h = jnp.dot(xn, wg_ref[:, c0:c0 + FF_CHUNK], preferred_element_type=F32)
        u = jnp.dot(xn, wu_ref[:, c0:c0 + FF_CHUNK], preferred_element_type=F32)
        a = (h + h * jnp.tanh(h)) * u
        act_ref[:, c0:c0 + FF_CHUNK] = a.astype(BF16)
        if after_chunk is not None:
            after_chunk(ci, a[0:1, 0:1])
    return jnp.dot(act_ref[...], wd_ref[...], preferred_element_type=F32)


def _ffn_in_tile(x_ref, ffn_refs, gm_ref, winl_ref, wina_ref, x1_ref, pl_ref, pa_ref, xn_ref, act_ref,
                 after_chunk=None):
    g1_ref, wg_ref, wu_ref, wd_ref = ffn_refs
    xn_ref[...] = _rms(x_ref[...], g1_ref[...]).astype(BF16)
    x1 = x_ref[...] + 0.5 * _ffn(xn_ref, wg_ref, wu_ref, wd_ref, act_ref, after_chunk)
    x1_ref[...] = x1
    xn_ref[...] = _rms(x1, gm_ref[...]).astype(BF16)
    pl_ref[...] = jnp.dot(xn_ref[...], winl_ref[...], preferred_element_type=F32)
    pa_ref[...] = jnp.dot(xn_ref[...], wina_ref[...], preferred_element_type=F32)


def _lru_gates(xc, gate_w, lam_ref):
    wa_ref, ba_ref, wx_ref, bx_ref = gate_w
    xcb = xc.astype(BF16)
    r = 0.5 * jnp.tanh(jnp.dot(xcb, wa_ref[...], preferred_element_type=F32) + ba_ref[...]) + 0.5
    i = 0.5 * jnp.tanh(jnp.dot(xcb, wx_ref[...], preferred_element_type=F32) + bx_ref[...]) + 0.5
    log_a = (-LRU_C * jax.nn.softplus(-lam_ref[...])) * r
    a = jnp.exp(log_a)
    w = -jnp.tanh(log_a) * (a * a + 1.0)
    root = jnp.where(w > 0.0, w * lax.rsqrt(w), 0.0)
    u = root * (i * xc)
    return a, u


def _lru_block(r, nblocks, gate, p_ref, first, lru_refs, out_ref, hl_ref, cs_ref, xh_ref, h_ref):
    cw_ref, cb_ref, wa_ref, ba_ref, wx_ref, bx_ref, lam_ref, on_ref = lru_refs
    tt = p_ref.shape[0]
    c = out_ref.shape[1]
    rb = tt // nblocks
    r0 = r * rb
    hist = V7X_SUBLANES

    if r == 0:
        xh_ref[0:hist, :] = jnp.where(first, jnp.zeros((hist, c), F32), xh_ref[0:hist, :])
    x = gate(p_ref[r0:r0 + rb, 0:c])
    xh_ref[hist + r0:hist + r0 + rb, :] = x
    w = cw_ref[...]
    xc = cb_ref[...]
    for j in range(CONV_W):
        off = hist - (CONV_W - 1) + j + r0
        xc = xc + xh_ref[off:off + rb, :] * w[j:j + 1, :]

    a, u = _lru_gates(xc, (wa_ref, ba_ref, wx_ref, bx_ref), lam_ref)

    groups = rb // V7X_SUBLANES
    a3 = a.reshape(groups, V7X_SUBLANES, c)
    u3 = u.reshape(groups, V7X_SUBLANES, c)
    sub = lax.broadcasted_iota(jnp.int32, a3.shape, 1)
    k = 1
    while k < V7X_SUBLANES:
        m = sub >= k
        a_prev = pltpu.roll(a3, k, 1)
        u_prev = pltpu.roll(u3, k, 1)
        u3 = jnp.where(m, a3 * u_prev + u3, u3)
        a3 = jnp.where(m, a3 * a_prev, a3)
        k *= 2

    h = h_ref[...]
    if r == 0:
        h = jnp.where(first, jnp.zeros((1, c), F32), h)
    hs = []
    for g in range(groups):
        hr = a3[g] * h + u3[g]
        hs.append(hr)
        h = hr[V7X_SUBLANES - 1:V7X_SUBLANES, :]
    h_ref[...] = h

    y = jnp.concatenate(hs, axis=0) * jax.nn.gelu(gate(p_ref[r0:r0 + rb, c:2 * c]))
    out_ref[r0:r0 + rb, :] = _rms(y, on_ref[...]).astype(out_ref.dtype)
    if r == nblocks - 1:
        xh_ref[0:hist, :] = x[rb - hist:rb, :]
        hl_ref[0] = h
        cs_ref[0] = x[rb - (CONV_W - 1):rb, :]


def _lru_sample_kernel(x_ref, cst_ref, h0_ref, cw_ref, cb_ref, wa_ref, ba_ref, wx_ref, bx_ref, lam_ref, on_ref,
                       out_ref, hl_ref, cs_ref):
    t = x_ref.shape[0]
    c = out_ref.shape[2]
    w = cw_ref[...]
    rows = [cst_ref[j] for j in range(CONV_W - 1)] + [x_ref[j, :, 0:c] for j in range(t)]
    h = h0_ref[...]
    for s in range(t):
        xc = cb_ref[...]
        for j in range(CONV_W):
            xc = xc + rows[s + j] * w[j:j + 1, :]
        a, u = _lru_gates(xc, (wa_ref, ba_ref, wx_ref, bx_ref), lam_ref)
        h = a * h + u
        y = h * jax.nn.gelu(x_ref[s, :, c:2 * c])
        out_ref[s] = _rms(y, on_ref[...]).astype(out_ref.dtype)
    hl_ref[...] = h
    for j in range(CONV_W - 1):
        cs_ref[j] = rows[t + j]


def _lru_sample(x_tm, cst_tm, h0, cw, cb, wa, ba, wx, bx, lam, on):
    t, s, w2 = x_tm.shape
    c = w2 // 2
    return pl.pallas_call(
        _lru_sample_kernel,
        out_shape=[jax.ShapeDtypeStruct((t, s, c), BF16), jax.ShapeDtypeStruct((s, c), F32),
                   jax.ShapeDtypeStruct((CONV_W - 1, s, c), F32)],
        compiler_params=pltpu.CompilerParams(vmem_limit_bytes=_vmem_limit(32 << 20)),
        name="lru_sample",
    )(x_tm, cst_tm, h0, cw, cb, wa, ba, wx, bx, lam, on)


def _head_norm_rope(xc, gain, cos, sin, ones_ref):
    x2 = xc * xc
    hi = x2.astype(BF16)
    lo = (x2 - hi.astype(F32)).astype(BF16)
    ss = jnp.dot(hi, ones_ref[...], preferred_element_type=F32) + jnp.dot(lo, ones_ref[...], preferred_element_type=F32)
    xn = (xc * lax.rsqrt(ss * (1.0 / HEAD_DIM) + NORM_EPS)) * gain
    lane = lax.broadcasted_iota(jnp.int32, xn.shape, xn.ndim - 1)
    first_half = (lane & (HEAD_DIM // 2)) == 0
    ax = xn.ndim - 1
    swapped = jnp.where(first_half, pltpu.roll(xn, V7X_LANES - HEAD_DIM // 2, ax), pltpu.roll(xn, HEAD_DIM // 2, ax))
    return xn * cos + swapped * sin


def _group_lhs(qcols, g, lane_lo, axis):
    zero = jnp.zeros_like(qcols[0])
    parts = [jnp.where(lane_lo, qcols[2 * g], zero), jnp.where(lane_lo, qcols[2 * g + 1], zero),
             jnp.where(lane_lo, zero, qcols[2 * g]), jnp.where(lane_lo, zero, qcols[2 * g + 1])]
    return jnp.concatenate(parts, axis=axis).astype(BF16)


def _attn_prep(pa_ref, cos_ref, sin_ref, attn_refs, pk_ref, pv_ref, qh_ref, k2_ref, vt_ref):
    qn_ref, kn_ref, _, ones_ref, _ = attn_refs
    tq = pa_ref.shape[0]
    blk = WINDOW
    lanes = V7X_LANES
    ncol = qh_ref.shape[0] // 2
    aw = ncol * lanes

    cos = cos_ref[...]
    sin = sin_ref[...]
    cols = [pa_ref[:, cc * lanes:(cc + 1) * lanes] for cc in range(ncol + 1)]
    sq = [x * x for x in cols]
    hi = [x.astype(BF16) for x in sq]
    lo = [(x - h.astype(F32)).astype(BF16) for x, h in zip(sq, hi)]
    ss = (jnp.dot(jnp.concatenate(hi, axis=0), ones_ref[...], preferred_element_type=F32)
          + jnp.dot(jnp.concatenate(lo, axis=0), ones_ref[...], preferred_element_type=F32))
    lane = lax.broadcasted_iota(jnp.int32, (tq, lanes), 1)
    first_half = (lane & (HEAD_DIM // 2)) == 0
    lane_lo = lane < HEAD_DIM
    rot = []
    for cc, x in enumerate(cols):
        gain = qn_ref[...] if cc < ncol else kn_ref[...]
        xn = (x * lax.rsqrt(ss[cc * tq:(cc + 1) * tq] * (1.0 / HEAD_DIM) + NORM_EPS)) * gain
        swapped = jnp.where(first_half, pltpu.roll(xn, lanes - HEAD_DIM // 2, 1), pltpu.roll(xn, HEAD_DIM // 2, 1))
        rot.append(xn * cos + swapped * sin)
    k = rot[ncol]
    v = pa_ref[:, aw + lanes:aw + 2 * lanes]

    zero = jnp.zeros((tq, lanes), F32)
    for cc in range(ncol):
        qs = rot[cc] * (HEAD_DIM ** -0.5)
        qh_ref[2 * cc] = jnp.where(lane_lo, qs, zero).astype(BF16)
        qh_ref[2 * cc + 1] = jnp.where(lane_lo, zero, qs).astype(BF16)

    k_sw = pltpu.roll(k, HEAD_DIM, 1)
    k2_ref[0, blk:blk + tq, :] = jnp.where(lane_lo, k, k_sw).astype(BF16)
    k2_ref[1, blk:blk + tq, :] = jnp.where(lane_lo, k_sw, k).astype(BF16)
    for jb in range(tq // blk):
        vt_ref[:, (jb + 1) * blk:(jb + 2) * blk] = v[jb * blk:(jb + 1) * blk].T.astype(BF16)
    pk_ref[0] = k[tq - blk:tq]
    pv_ref[0] = v[tq - blk:tq]


def _attn_block(jb, gate, first, attn_refs, out_ref, qh_ref, k2_ref, vt_ref):
    _, _, sk_ref, _, on_ref = attn_refs
    tq = qh_ref.shape[1]
    blk = WINDOW
    nblk = tq // blk
    heads_per_group = qh_ref.shape[0] // N_KV_HEADS
    r0 = jb * blk

    ncolq = heads_per_group * blk
    kj = lax.broadcasted_iota(jnp.int32, (blk, ncolq), 0)
    qi = lax.broadcasted_iota(jnp.int32, (blk, ncolq), 1) % blk
    tri = kj <= qi

    ocols = []
    for g in range(N_KV_HEADS):
        lhs = jnp.concatenate([gate(qh_ref[h, r0:r0 + blk, :])
                               for h in range(g * heads_per_group, (g + 1) * heads_per_group)], axis=0)
        st = lax.dot_general(k2_ref[g, r0:r0 + 2 * blk, :], lhs, (((1,), (1,)), ((), ())),
                             preferred_element_type=F32)
        s_prev = st[0:blk]
        if jb == 0:
            s_prev = s_prev + jnp.where(first, jnp.float32(-jnp.inf), jnp.float32(0.0))
        m = jnp.where(tri, st[blk:2 * blk], s_prev)
        sk = sk_ref[g:g + 1, :]
        mx = jnp.maximum(jnp.max(m, axis=0, keepdims=True), sk)
        p = jnp.exp(m - mx)
        den = jnp.sum(p, axis=0, keepdims=True) + jnp.exp(sk - mx)
        pz = jnp.zeros_like(p)
        pt = jnp.concatenate([jnp.where(tri, pz, p), jnp.where(tri, p, pz)], axis=0).astype(BF16)
        ot = jnp.dot(vt_ref[:, r0:r0 + 2 * blk], pt, preferred_element_type=F32)
        og = ot[g * HEAD_DIM:(g + 1) * HEAD_DIM, :] * (1.0 / den)
        for half in range(heads_per_group // 2):
            pair = jnp.concatenate([og[:, (2 * half) * blk:(2 * half + 1) * blk],
                                    og[:, (2 * half + 1) * blk:(2 * half + 2) * blk]], axis=0)
            ocols.append(pair.T)
    o = jnp.concatenate(ocols, axis=1)
    out_ref[r0:r0 + blk, :] = _rms(o, on_ref[...]).astype(out_ref.dtype)

    if jb == nblk - 1:
        k2_ref[:, 0:blk, :] = k2_ref[:, tq:tq + blk, :]
        vt_ref[:, 0:blk] = vt_ref[:, tq:tq + blk]


N_FFN_IN = 7
N_LRU = 8
N_ATTN = 5
MIXER_LRU_BLOCKS = 4


def _mixer_in_kernel(*refs, tiles_per_seq):
    x_ref, cos_ref, sin_ref = refs[0:3]
    p0 = 3
    g1_ref, wg_ref, wu_ref, wd_ref, gm_ref, winl_ref, wina_ref = refs[p0:p0 + N_FFN_IN]
    lru_refs = refs[p0 + N_FFN_IN:p0 + N_FFN_IN + N_LRU]
    attn_refs = refs[p0 + N_FFN_IN + N_LRU:p0 + N_FFN_IN + N_LRU + N_ATTN]
    o0 = p0 + N_FFN_IN + N_LRU + N_ATTN
    x1_ref, lo_ref, ao_ref, hl_ref, cs_ref, pk_ref, pv_ref = refs[o0:o0 + 7]
    xn_ref, act_ref, pl_ref, pa_ref, xh_ref, h_ref, qh_ref, k2_ref, vt_ref = refs[o0 + 7:]

    i = pl.program_id(0)

    @pl.when(i == 0)
    def _():
        pl_ref[...] = jnp.zeros_like(pl_ref)
        pa_ref[...] = jnp.zeros_like(pa_ref)
        xh_ref[...] = jnp.zeros_like(xh_ref)
        h_ref[...] = jnp.zeros_like(h_ref)
        k2_ref[...] = jnp.zeros_like(k2_ref)
        vt_ref[...] = jnp.zeros_like(vt_ref)

    first = lax.rem(jnp.maximum(i - 1, 0), tiles_per_seq) == 0
    ident = lambda x: x
    _attn_prep(pa_ref, cos_ref, sin_ref, attn_refs, pk_ref, pv_ref, qh_ref, k2_ref, vt_ref)
    for r in range(MIXER_LRU_BLOCKS):
        _lru_block(r, MIXER_LRU_BLOCKS, ident, pl_ref, first, lru_refs, lo_ref, hl_ref, cs_ref, xh_ref, h_ref)
    for jb in range(pa_ref.shape[0] // WINDOW):
        _attn_block(jb, ident, first, attn_refs, ao_ref, qh_ref, k2_ref, vt_ref)

    _ffn_in_tile(x_ref, (g1_ref, wg_ref, wu_ref, wd_ref), gm_ref, winl_ref, wina_ref,
                 x1_ref, pl_ref, pa_ref, xn_ref, act_ref)


def _mixer_in(x, seq_len, cos, sin, ffn_in_p, lru_p, attn_p):
    n, d = x.shape
    tm = TOKEN_TILE
    assert seq_len % tm == 0 and n % seq_len == 0 and tm % WINDOW == 0
    nseq = n // seq_len
    tps = seq_len // tm
    ntile = n // tm
    wg, winl, wina = ffn_in_p[1], ffn_in_p[5], ffn_in_p[6]
    w2, wa = winl.shape[1], wina.shape[1]
    c = w2 // 2
    aw = attn_p[4].shape[1]
    kvw = N_KV_HEADS * HEAD_DIM
    cur = lambda i: jnp.minimum(i, ntile - 1)
    prev = lambda i: jnp.maximum(i - 1, 0)
    params = list(ffn_in_p) + list(lru_p) + list(attn_p)
    weights = sum(p.size * p.dtype.itemsize for p in params)
    tiles = 2 * 4 * tm * 2 * d + 2 * 2 * tm * (c + aw) + 4 * tm * (w2 + wa) + tm * d * (2 + 4) + 3 * 4 * tm * c
    return pl.pallas_call(
        functools.partial(_mixer_in_kernel, tiles_per_seq=tps),
        grid=(ntile + 1,),
        in_specs=[pl.BlockSpec((tm, d), lambda i: (cur(i), 0)),
                  pl.BlockSpec((tm, V7X_LANES), lambda i: (lax.rem(prev(i), tps), 0)),
                  pl.BlockSpec((tm, V7X_LANES), lambda i: (lax.rem(prev(i), tps), 0))]
                 + [_resident(p.shape) for p in params],
        out_specs=[pl.BlockSpec((tm, d), lambda i: (cur(i), 0)),
                   pl.BlockSpec((tm, c), lambda i: (prev(i), 0)),
                   pl.BlockSpec((tm, aw), lambda i: (prev(i), 0)),
                   pl.BlockSpec((1, 1, c), lambda i: (prev(i) // tps, 0, 0)),
                   pl.BlockSpec((1, CONV_W - 1, c), lambda i: (prev(i) // tps, 0, 0)),
                   pl.BlockSpec((1, WINDOW, kvw), lambda i: (prev(i) // tps, 0, 0)),
                   pl.BlockSpec((1, WINDOW, kvw), lambda i: (prev(i) // tps, 0, 0))],
        out_shape=[jax.ShapeDtypeStruct((n, d), F32), jax.ShapeDtypeStruct((n, c), BF16),
                   jax.ShapeDtypeStruct((n, aw), BF16), jax.ShapeDtypeStruct((nseq, 1, c), F32),
                   jax.ShapeDtypeStruct((nseq, CONV_W - 1, c), F32), jax.ShapeDtypeStruct((nseq, WINDOW, kvw), F32),
                   jax.ShapeDtypeStruct((nseq, WINDOW, kvw), F32)],
        scratch_shapes=[pltpu.VMEM((tm, d), BF16), pltpu.VMEM((tm, wg.shape[1]), BF16),
                        pltpu.VMEM((tm, w2), F32), pltpu.VMEM((tm, wa), F32),
                        pltpu.VMEM((tm + V7X_SUBLANES, c), F32), pltpu.VMEM((1, c), F32),
                        pltpu.VMEM((2 * aw // V7X_LANES, tm, V7X_LANES), BF16),
                        pltpu.VMEM((N_KV_HEADS, WINDOW + tm, V7X_LANES), BF16),
                        pltpu.VMEM((V7X_LANES, WINDOW + tm), BF16)],
        compiler_params=pltpu.CompilerParams(dimension_semantics=("arbitrary",),
                                             vmem_limit_bytes=_vmem_limit(weights + tiles + (20 << 20))),
        name="mixer_in",
    )(x, cos, sin, *params)


def _ffn_in_kernel(x_ref, g1_ref, wg_ref, wu_ref, wd_ref, gm_ref, winl_ref, wina_ref,
                   x1_ref, pl_ref, pa_ref, xn_ref, act_ref):
    _ffn_in_tile(x_ref, (g1_ref, wg_ref, wu_ref, wd_ref), gm_ref, winl_ref, wina_ref,
                 x1_ref, pl_ref, pa_ref, xn_ref, act_ref)


def _ffn_in(x, g1, wg, wu, wd, gm, winl, wina):
    n, d = x.shape
    tm = min(TOKEN_TILE, n)
    wl, wa = winl.shape[1], wina.shape[1]
    row = lambda w: pl.BlockSpec((tm, w), lambda i: (i, 0))
    weights = 2 * (wg.size + wu.size + wd.size + winl.size + wina.size)
    tiles = 2 * 4 * tm * (2 * d + wl + wa) + tm * d * (2 + 4) + 4 * 4 * tm * max(FF_CHUNK, wl)
    return pl.pallas_call(
        _ffn_in_kernel,
        grid=(n // tm,),
        in_specs=[row(d), _resident(g1.shape), _resident(wg.shape), _resident(wu.shape), _resident(wd.shape),
                  _resident(gm.shape), _resident(winl.shape), _resident(wina.shape)],
        out_specs=[row(d), row(wl), row(wa)],
        out_shape=[jax.ShapeDtypeStruct((n, d), F32), jax.ShapeDtypeStruct((n, wl), F32),
                   jax.ShapeDtypeStruct((n, wa), F32)],
        scratch_shapes=[pltpu.VMEM((tm, d), BF16), pltpu.VMEM((tm, wg.shape[1]), BF16)],
        compiler_params=pltpu.CompilerParams(dimension_semantics=("arbitrary",),
                                             vmem_limit_bytes=_vmem_limit(weights + tiles + (8 << 20))),
        name="ffn_in",
    )(x, g1, wg, wu, wd, gm, winl, wina)


def _out_ffn_kernel(lo_ref, ao_ref, x1_ref, wol_ref, woa_ref, g2_ref, wg_ref, wu_ref, wd_ref,
                    out_ref, xn_ref, act_ref):
    y = jnp.dot(lo_ref[...], wol_ref[...], preferred_element_type=F32)
    y = y + jnp.dot(ao_ref[...], woa_ref[...], preferred_element_type=F32)
    x2 = x1_ref[...] + y
    out_ref[...] = x2
    xn_ref[...] = _rms(x2, g2_ref[...]).astype(BF16)
    out_ref[...] = out_ref[...] + 0.5 * _ffn(xn_ref, wg_ref, wu_ref, wd_ref, act_ref)


def _out_ffn(lo, ao, x1, wol, woa, g2, wg, wu, wd):
    n, d = x1.shape
    tm = min(TOKEN_TILE, n)
    row = lambda w: pl.BlockSpec((tm, w), lambda i: (i, 0))
    weights = 2 * (wg.size + wu.size + wd.size + wol.size + woa.size)
    tiles = 2 * tm * (2 * lo.shape[1] + 2 * ao.shape[1] + 8 * d) + tm * d * (2 + 4) + 4 * 4 * tm * d
    return pl.pallas_call(
        _out_ffn_kernel,
        grid=(n // tm,),
        in_specs=[row(lo.shape[1]), row(ao.shape[1]), row(d), _resident(wol.shape), _resident(woa.shape),
                  _resident(g2.shape), _resident(wg.shape), _resident(wu.shape), _resident(wd.shape)],
        out_specs=row(d),
        out_shape=jax.ShapeDtypeStruct((n, d), F32),
        scratch_shapes=[pltpu.VMEM((tm, d), BF16), pltpu.VMEM((tm, wg.shape[1]), BF16)],
        compiler_params=pltpu.CompilerParams(dimension_semantics=("arbitrary",),
                                             vmem_limit_bytes=_vmem_limit(weights + tiles + (8 << 20))),
        name="out_ffn",
    )(lo, ao, x1, wol, woa, g2, wg, wu, wd)


def _qk_prep_kernel(pa_ref, cos_ref, sin_ref, qn_ref, kn_ref, ones_ref, q_ref, k_ref):
    lanes = V7X_LANES
    aw = q_ref.shape[1]
    cos = cos_ref[...]
    sin = sin_ref[...]
    for cc in range(aw // lanes):
        q_ref[:, cc * lanes:(cc + 1) * lanes] = _head_norm_rope(
            pa_ref[:, cc * lanes:(cc + 1) * lanes], qn_ref[...], cos, sin, ones_ref) * (HEAD_DIM ** -0.5)
    k_ref[...] = _head_norm_rope(pa_ref[:, aw:aw + lanes], kn_ref[...], cos, sin, ones_ref)


def _qk_prep(pa, cos, sin, qn, kn, ones, aw):
    n = pa.shape[0]
    return pl.pallas_call(
        _qk_prep_kernel,
        out_shape=[jax.ShapeDtypeStruct((n, aw), F32), jax.ShapeDtypeStruct((n, N_KV_HEADS * HEAD_DIM), F32)],
        compiler_params=pltpu.CompilerParams(vmem_limit_bytes=_vmem_limit(32 << 20)),
        name="qk_prep_sample",
    )(pa, cos, sin, qn, kn, ones)


def _attn_cached_kernel(q_ref, kn_ref, vn_ref, ck_ref, cv_ref, sk_ref, on_ref,
                        out_ref, nk_ref, nv_ref, kk_ref, vv_ref):
    sb, t, aw = q_ref.shape
    w = ck_ref.shape[1]
    lanes = V7X_LANES
    jpad = kk_ref.shape[1]

    kk_ref[:, 0:w, :] = ck_ref[...]
    vv_ref[:, 0:w, :] = cv_ref[...]
    kk_ref[:, w:w + t, :] = kn_ref[...]
    vv_ref[:, w:w + t, :] = vn_ref[...]
    kk_ref[:, w + t:jpad, :] = jnp.zeros((sb, jpad - w - t, lanes), F32)
    vv_ref[:, w + t:jpad, :] = jnp.zeros((sb, jpad - w - t, lanes), F32)
    nk_ref[...] = kk_ref[:, t:t + w, :]
    nv_ref[...] = vv_ref[:, t:t + w, :]

    kk = kk_ref[...]
    vv = vv_ref[...]
    k_sw = pltpu.roll(kk, HEAD_DIM, 2)
    v_sw = pltpu.roll(vv, HEAD_DIM, 2)
    lane_k = lax.broadcasted_iota(jnp.int32, kk.shape, 2) < HEAD_DIM
    lane_q = lax.broadcasted_iota(jnp.int32, (sb, t, lanes), 2) < HEAD_DIM
    k2 = (jnp.where(lane_k, kk, k_sw).astype(BF16), jnp.where(lane_k, k_sw, kk).astype(BF16))
    v2 = (vv.astype(BF16), v_sw.astype(BF16))

    qcols = [q_ref[:, :, cc * lanes:(cc + 1) * lanes] for cc in range(aw // lanes)]
    tq = lax.broadcasted_iota(jnp.int32, (sb, 4 * t, jpad), 1) % t
    j = lax.broadcasted_iota(jnp.int32, (sb, 4 * t, jpad), 2)
    valid = jnp.logical_and(j > tq, j <= tq + w)
    neg_inf = jnp.float32(-jnp.inf)

    ocols = []
    for g in range(N_KV_HEADS):
        lhs = _group_lhs(qcols, g, lane_q, 1)
        s = jnp.einsum('bqd,bjd->bqj', lhs, k2[g], preferred_element_type=F32)
        s = jnp.where(valid, s, neg_inf)
        heads = (4 * g, 4 * g + 2, 4 * g + 1, 4 * g + 3)
        sk = jnp.concatenate([jnp.broadcast_to(sk_ref[h:h + 1, 0:1].reshape(1, 1, 1), (sb, t, 1)) for h in heads],
                             axis=1)
        mx = jnp.maximum(jnp.max(s, axis=-1, keepdims=True), sk)
        p = jnp.exp(s - mx)
        den = jnp.sum(p, axis=-1, keepdims=True) + jnp.exp(sk - mx)
        pb = p.astype(BF16)
        o_even = jnp.einsum('bqj,bjd->bqd', pb[:, 0:2 * t], v2[g], preferred_element_type=F32) / den[:, 0:2 * t]
        o_odd = jnp.einsum('bqj,bjd->bqd', pb[:, 2 * t:4 * t], v2[1 - g], preferred_element_type=F32) / den[:, 2 * t:4 * t]
        for half in range(2):
            ocols.append(jnp.where(lane_q, o_even[:, half * t:(half + 1) * t], o_odd[:, half * t:(half + 1) * t]))
    o = jnp.concatenate(ocols, axis=2)
    out_ref[...] = _rms(o, on_ref[...].reshape(1, 1, aw)).astype(out_ref.dtype)


def _attn_cached(q3, kn3, vn3, ck, cv, sk, on):
    s, t, aw = q3.shape
    w, kvw = ck.shape[1], ck.shape[2]
    sb = min(SAMPLE_SEQ_TILE, s)
    jpad = -(-(w + t) // V7X_SUBLANES) * V7X_SUBLANES
    seq = lambda a, c: pl.BlockSpec((sb, a, c), lambda i: (i, 0, 0))
    return pl.pallas_call(
        _attn_cached_kernel,
        grid=(s // sb,),
        in_specs=[seq(t, aw), seq(t, kvw), seq(t, kvw), seq(w, kvw), seq(w, kvw), _resident(sk.shape), _resident(on.shape)],
        out_specs=[seq(t, aw), seq(w, kvw), seq(w, kvw)],
        out_shape=[jax.ShapeDtypeStruct((s, t, aw), BF16), jax.ShapeDtypeStruct((s, w, kvw), F32),
                   jax.ShapeDtypeStruct((s, w, kvw), F32)],
        scratch_shapes=[pltpu.VMEM((sb, jpad, kvw), F32), pltpu.VMEM((sb, jpad, kvw), F32)],
        compiler_params=pltpu.CompilerParams(dimension_semantics=("arbitrary",),
                                             vmem_limit_bytes=_vmem_limit(40 << 20)),
        name="attn_sample",
    )(q3, kn3, vn3, ck, cv, sk, on)


def _rope_tables(pos):
    half = HEAD_DIM // 2
    lane = jnp.arange(V7X_LANES, dtype=jnp.int32)
    inv_freq = ROPE_THETA ** (-(lane % half).astype(F32) / half)
    ang = pos.astype(F32)[:, None] * inv_freq[None, :]
    sign = jnp.where((lane % HEAD_DIM) < half, -1.0, 1.0).astype(F32)
    return jnp.cos(ang), jnp.sin(ang) * sign[None, :]


def _block_diag(w):
    n, c, d = w.shape
    eye = jnp.eye(n, dtype=w.dtype)
    return (w[:, :, None, :] * eye[:, None, :, None]).reshape(n * c, n * d)


def _row(v):
    return v.reshape(1, -1).astype(F32)


def kernel(x_prompt, x_sample, state_lru_h, state_conv, cache_k, cache_v, ffn1_norm, ffn1_w_gate, ffn1_w_up, ffn1_w_down, mix_norm, w_in, conv_w, conv_b, lru_wa, lru_ba, lru_wx, lru_bx, lru_lambda, q_norm, k_norm, attn_sinks, lru_out_norm, attn_out_norm, w_out, ffn2_norm, ffn2_w_gate, ffn2_w_up, ffn2_w_down):
    depth = w_in.shape[0]
    b, t, d = x_prompt.shape
    s, ts, _ = x_sample.shape
    c = lru_lambda.shape[1]
    aw = attn_out_norm.shape[1]
    kvw = N_KV_HEADS * HEAD_DIM
    win = cache_k.shape[2]
    assert t >= WINDOW and win == WINDOW

    cos_p, sin_p = _rope_tables(jnp.arange(t, dtype=jnp.int32))
    cos_s, sin_s = _rope_tables(PAST_LEN + jnp.arange(ts, dtype=jnp.int32))
    cos_s, sin_s = jnp.tile(cos_s, (s, 1)), jnp.tile(sin_s, (s, 1))
    ones = _block_diag(jnp.ones((V7X_LANES // HEAD_DIM, HEAD_DIM, HEAD_DIM), BF16))

    yp = x_prompt.reshape(b * t, d)
    ys = x_sample.reshape(s * ts, d)
    outs = [[] for _ in range(8)]
    for l in range(depth):
        ffn1 = (_row(ffn1_norm[l]), (0.5 * ffn1_w_gate[l]).astype(BF16), ffn1_w_up[l].astype(BF16),
                ffn1_w_down[l].astype(BF16))
        ffn2 = (_row(ffn2_norm[l]), (0.5 * ffn2_w_gate[l]).astype(BF16), ffn2_w_up[l].astype(BF16),
                ffn2_w_down[l].astype(BF16))
        win_b = w_in[l].astype(BF16)
        ffn_in_p = ffn1 + (_row(mix_norm[l]), win_b[:, :2 * c], win_b[:, 2 * c:])
        wout_b = w_out[l].astype(BF16)
        wol, woa = wout_b[:c], wout_b[c:]
        lru_p = (conv_w[l], _row(conv_b[l]), (0.5 * _block_diag(lru_wa[l])).astype(BF16), _row(0.5 * lru_ba[l]),
                 (0.5 * _block_diag(lru_wx[l])).astype(BF16), _row(0.5 * lru_bx[l]), _row(lru_lambda[l]),
                 _row(lru_out_norm[l]))
        reps = V7X_LANES // HEAD_DIM
        qn = _row(jnp.tile(q_norm[l], reps))
        kn = _row(jnp.tile(k_norm[l], reps))
        sk = jnp.broadcast_to(attn_sinks[l].astype(F32)[:, None], (attn_sinks.shape[1], V7X_LANES))
        sk_rows = jnp.repeat(attn_sinks[l].astype(F32).reshape(N_KV_HEADS, -1), WINDOW, axis=1)
        a_on = _row(attn_out_norm[l])

        x1, lo, ao, hl, cs, pk, pv = _mixer_in(yp, t, cos_p, sin_p, ffn_in_p, lru_p, (qn, kn, sk_rows, ones, a_on))
        yp = _out_ffn(lo, ao, x1, wol, woa, *ffn2)
        outs[0].append(hl.reshape(b, c))
        outs[1].append(cs)
        outs[2].append(pk.reshape(b, WINDOW, N_KV_HEADS, HEAD_DIM))
        outs[3].append(pv.reshape(b, WINDOW, N_KV_HEADS, HEAD_DIM))

        x1s, pl_s, pa_s = _ffn_in(ys, *ffn_in_p)
        x_tm = pl_s.reshape(s, ts, 2 * c).transpose(1, 0, 2)
        cst_tm = state_conv[l].transpose(1, 0, 2)
        lo_tm, hls, cs_tm = _lru_sample(x_tm, cst_tm, state_lru_h[l], *lru_p)
        q_s, k_s = _qk_prep(pa_s, cos_s, sin_s, qn, kn, ones, aw)
        v_s = pa_s[:, aw + kvw:]
        aos, nk, nv = _attn_cached(q_s.reshape(s, ts, aw), k_s.reshape(s, ts, kvw), v_s.reshape(s, ts, kvw),
                                   cache_k[l].reshape(s, win, kvw), cache_v[l].reshape(s, win, kvw), sk, a_on)
        ys = _out_ffn(lo_tm.transpose(1, 0, 2).reshape(s * ts, c), aos.reshape(s * ts, aw), x1s, wol, woa, *ffn2)
        outs[4].append(hls)
        outs[5].append(cs_tm.transpose(1, 0, 2))
        outs[6].append(nk.reshape(s, win, N_KV_HEADS, HEAD_DIM))
        outs[7].append(nv.reshape(s, win, N_KV_HEADS, HEAD_DIM))

    st = [jnp.stack(o) for o in outs]
    return (yp.reshape(b, t, d), ys.reshape(s, ts, d), st[0], st[1], st[2], st[3], st[4], st[5], st[6], st[7])
```

```python
import functools

import jax
import jax.numpy as jnp
from jax import lax
from jax.experimental import pallas as pl
from jax.experimental.pallas import tpu as pltpu

F32 = jnp.float32
BF16 = jnp.bfloat16

NORM_EPS = 1e-6
LRU_C = 8.0
CONV_W = 4
HEAD_DIM = 64
N_KV_HEADS = 2
WINDOW = 128
ROPE_THETA = 10000.0
PAST_LEN = 16384

V7X_LANES = 128
V7X_SUBLANES = 8
V7X_MXU_DIM = 256
V7X_VMEM_BYTES = 64 * 1024 * 1024

TOKEN_TILE = 512
SAMPLE_SEQ_TILE = 16
FF_CHUNK = V7X_MXU_DIM


def _vmem_limit(nbytes):
    return int(min(nbytes, V7X_VMEM_BYTES - 4 * 1024 * 1024))


def _resident(shape):
    nd = len(shape)
    return pl.BlockSpec(shape, lambda *_: (0,) * nd, pipeline_mode=pl.Buffered(1))


def _rms(x, g):
    ms = jnp.mean(x * x, axis=-1, keepdims=True)
    return (x * lax.rsqrt(ms + NORM_EPS)) * g


def _ffn(xn_ref, wg_ref, wu_ref, wd_ref, acc_ref):
    acc_ref[...] = jnp.zeros_like(acc_ref)
    for c0 in range(0, wg_ref.shape[1], FF_CHUNK):
        xn = xn_ref[...]
        h = jnp.dot(xn, wg_ref[:, c0:c0 + FF_CHUNK], preferred_element_type=F32)
        u = jnp.dot(xn, wu_ref[:, c0:c0 + FF_CHUNK], preferred_element_type=F32)
        a = ((h + h * jnp.tanh(h)) * u).astype(BF16)
        acc_ref[...] += jnp.dot(a, wd_ref[c0:c0 + FF_CHUNK, :], preferred_element_type=F32)
    return acc_ref[...]


def _ffn_in_tile(x_ref, ffn_refs, gm_ref, winl_ref, wina_ref, x1_ref, pl_ref, pa_ref, xn_ref, acc_ref):
    g1_ref, wg_ref, wu_ref, wd_ref = ffn_refs
    xn_ref[...] = _rms(x_ref[...], g1_ref[...]).astype(BF16)
    x1 = x_ref[...] + 0.5 * _ffn(xn_ref, wg_ref, wu_ref, wd_ref, acc_ref)
    x1_ref[...] = x1
    xn_ref[...] = _rms(x1, gm_ref[...]).astype(BF16)
    pl_ref[...] = jnp.dot(xn_ref[...], winl_ref[...], preferred_element_type=F32)
    pa_ref[...] = jnp.dot(xn_ref[...], wina_ref[...], preferred_element_type=F32)


def _lru_preact(xc, gate_w):
    wa_ref, ba_ref, wx_ref, bx_ref = gate_w
    xcb = xc.astype(BF16)
    zr = jnp.dot(xcb, wa_ref[...], preferred_element_type=F32) + ba_ref[...]
    zi = jnp.dot(xcb, wx_ref[...], preferred_element_type=F32) + bx_ref[...]
    return zr, zi


def _lru_terms(xc, zr, zi, lam_ref):
    r = 0.5 * jnp.tanh(zr) + 0.5
    i = 0.5 * jnp.tanh(zi) + 0.5
    log_a = (-LRU_C * jax.nn.softplus(-lam_ref[...])) * r
    a = jnp.exp(log_a)
    w = -jnp.tanh(log_a) * (a * a + 1.0)
    root = jnp.where(w > 0.0, w * lax.rsqrt(w), 0.0)
    u = root * (i * xc)
    return a, u


def _lru_prep(p_ref, first, lru_refs, xh_ref, xc_ref, zr_ref, zi_ref, gate_ref, ctail_ref):
    cw_ref, cb_ref, wa_ref, ba_ref, wx_ref, bx_ref, _, _ = lru_refs
    tt = p_ref.shape[0]
    c = xc_ref.shape[1]
    hist = V7X_SUBLANES

    xh_ref[0:hist, :] = jnp.where(first, jnp.zeros((hist, c), F32), xh_ref[0:hist, :])
    x = p_ref[:, 0:c]
    xh_ref[hist:hist + tt, :] = x
    xg = xh_ref[...].reshape(tt // hist + 1, hist, c)
    sub = lax.broadcasted_iota(jnp.int32, (tt // hist, hist, c), 1)
    w = cw_ref[...]
    xc = cb_ref[...].reshape(1, 1, c)
    for j in range(CONV_W - 1):
        k = CONV_W - 1 - j
        rolled = pltpu.roll(xg, k, 1)
        xc = xc + jnp.where(sub >= k, rolled[1:], rolled[:-1]) * w[j:j + 1, :].reshape(1, 1, c)
    xc = (xc + xg[1:] * w[CONV_W - 1:CONV_W, :].reshape(1, 1, c)).reshape(tt, c)
    xh_ref[0:hist, :] = x[tt - hist:tt, :]
    ctail_ref[...] = x[tt - (CONV_W - 1):tt, :]

    xc_ref[...] = xc
    zr, zi = _lru_preact(xc, (wa_ref, ba_ref, wx_ref, bx_ref))
    zr_ref[...] = zr
    zi_ref[...] = zi
    gate_ref[...] = p_ref[:, c:2 * c]


def _lru_core(r, nblocks, first, lru_refs, xc_ref, zr_ref, zi_ref, gate_ref, ctail_ref,
              out_ref, hl_ref, cs_ref, h_ref):
    lam_ref, on_ref = lru_refs[6], lru_refs[7]
    tt, c = xc_ref.shape
    rb = tt // nblocks
    r0 = r * rb

    a, u = _lru_terms(xc_ref[r0:r0 + rb, :], zr_ref[r0:r0 + rb, :], zi_ref[r0:r0 + rb, :], lam_ref)

    groups = rb // V7X_SUBLANES
    a3 = a.reshape(groups, V7X_SUBLANES, c)
    u3 = u.reshape(groups, V7X_SUBLANES, c)
    sub = lax.broadcasted_iota(jnp.int32, a3.shape, 1)
    k = 1
    while k < V7X_SUBLANES:
        m = sub >= k
        a_prev = pltpu.roll(a3, k, 1)
        u_prev = pltpu.roll(u3, k, 1)
        u3 = jnp.where(m, a3 * u_prev + u3, u3)
        a3 = jnp.where(m, a3 * a_prev, a3)
        k *= 2

    h = h_ref[...]
    if r == 0:
        h = jnp.where(first, jnp.zeros((1, c), F32), h)
    hs = []
    for g in range(groups):
        hr = a3[g] * h + u3[g]
        hs.append(hr)
        h = hr[V7X_SUBLANES - 1:V7X_SUBLANES, :]
    h_ref[...] = h

    y = jnp.concatenate(hs, axis=0) * jax.nn.gelu(gate_ref[r0:r0 + rb, :])
    out_ref[r0:r0 + rb, :] = _rms(y, on_ref[...]).astype(out_ref.dtype)
    if r == nblocks - 1:
        hl_ref[0] = h
        cs_ref[0] = ctail_ref[...]


def _lru_sample_kernel(x_ref, cst_ref, h0_ref, cw_ref, cb_ref, wa_ref, ba_ref, wx_ref, bx_ref, lam_ref, on_ref,
                       out_ref, hl_ref, cs_ref):
    t = x_ref.shape[0]
    c = out_ref.shape[2]
    w = cw_ref[...]
    rows = [cst_ref[j] for j in range(CONV_W - 1)] + [x_ref[j, :, 0:c] for j in range(t)]
    h = h0_ref[...]
    for s in range(t):
        xc = cb_ref[...]
        for j in range(CONV_W):
            xc = xc + rows[s + j] * w[j:j + 1, :]
        zr, zi = _lru_preact(xc, (wa_ref, ba_ref, wx_ref, bx_ref))
        a, u = _lru_terms(xc, zr, zi, lam_ref)
        h = a * h + u
        y = h * jax.nn.gelu(x_ref[s, :, c:2 * c])
        out_ref[s] = _rms(y, on_ref[...]).astype(out_ref.dtype)
    hl_ref[...] = h
    for j in range(CONV_W - 1):
        cs_ref[j] = rows[t + j]


def _lru_sample(x_tm, cst_tm, h0, cw, cb, wa, ba, wx, bx, lam, on):
    t, s, w2 = x_tm.shape
    c = w2 // 2
    return pl.pallas_call(
        _lru_sample_kernel,
        out_shape=[jax.ShapeDtypeStruct((t, s, c), BF16), jax.ShapeDtypeStruct((s, c), F32),
                   jax.ShapeDtypeStruct((CONV_W - 1, s, c), F32)],
        compiler_params=pltpu.CompilerParams(vmem_limit_bytes=_vmem_limit(32 << 20)),
        name="lru_sample",
    )(x_tm, cst_tm, h0, cw, cb, wa, ba, wx, bx, lam, on)


def _head_norm_rope(xc, gain, cos, sin, ones_ref):
    x2 = xc * xc
    hi = x2.astype(BF16)
    lo = (x2 - hi.astype(F32)).astype(BF16)
    ss = jnp.dot(hi, ones_ref[...], preferred_element_type=F32) + jnp.dot(lo, ones_ref[...], preferred_element_type=F32)
    xn = (xc * lax.rsqrt(ss + NORM_EPS)) * gain
    lane = lax.broadcasted_iota(jnp.int32, xn.shape, xn.ndim - 1)
    first_half = (lane & (HEAD_DIM // 2)) == 0
    ax = xn.ndim - 1
    swapped = jnp.where(first_half, pltpu.roll(xn, V7X_LANES - HEAD_DIM // 2, ax), pltpu.roll(xn, HEAD_DIM // 2, ax))
    return xn * cos + swapped * sin


def _group_lhs(qcols, g, lane_lo, axis):
    zero = jnp.zeros_like(qcols[0])
    parts = [jnp.where(lane_lo, qcols[2 * g], zero), jnp.where(lane_lo, qcols[2 * g + 1], zero),
             jnp.where(lane_lo, zero, qcols[2 * g]), jnp.where(lane_lo, zero, qcols[2 * g + 1])]
    return jnp.concatenate(parts, axis=axis).astype(BF16)


def _attn_prep(pa_ref, cos_ref, sin_ref, attn_refs, qh_ref, k2_ref, vt_ref, kvl_ref):
    qn_ref, kn_ref, _, ones_ref, _ = attn_refs
    tq = pa_ref.shape[0]
    blk = WINDOW
    lanes = V7X_LANES
    ncol = qh_ref.shape[0] // 2
    aw = ncol * lanes

    cos = cos_ref[...]
    sin = sin_ref[...]
    cols = [pa_ref[:, cc * lanes:(cc + 1) * lanes] for cc in range(ncol + 1)]
    sq = [x * x for x in cols]
    hi = [x.astype(BF16) for x in sq]
    lo = [(x - h.astype(F32)).astype(BF16) for x, h in zip(sq, hi)]
    ss = (jnp.dot(jnp.concatenate(hi, axis=0), ones_ref[...], preferred_element_type=F32)
          + jnp.dot(jnp.concatenate(lo, axis=0), ones_ref[...], preferred_element_type=F32))
    lane = lax.broadcasted_iota(jnp.int32, (tq, lanes), 1)
    first_half = (lane & (HEAD_DIM // 2)) == 0
    lane_lo = lane < HEAD_DIM
    rot = []
    for cc, x in enumerate(cols):
        gain = qn_ref[...] if cc < ncol else kn_ref[...]
        xn = (x * lax.rsqrt(ss[cc * tq:(cc + 1) * tq] + NORM_EPS)) * gain
        swapped = jnp.where(first_half, pltpu.roll(xn, lanes - HEAD_DIM // 2, 1), pltpu.roll(xn, HEAD_DIM // 2, 1))
        rot.append(xn * cos + swapped * sin)
    k = rot[ncol]
    v = pa_ref[:, aw + lanes:aw + 2 * lanes]

    zero = jnp.zeros((tq, lanes), F32)
    for cc in range(ncol):
        qh_ref[2 * cc] = jnp.where(lane_lo, rot[cc], zero).astype(BF16)
        qh_ref[2 * cc + 1] = jnp.where(lane_lo, zero, rot[cc]).astype(BF16)

    k_sw = pltpu.roll(k, HEAD_DIM, 1)
    k2_ref[0, blk:blk + tq, :] = jnp.where(lane_lo, k, k_sw).astype(BF16)
    k2_ref[1, blk:blk + tq, :] = jnp.where(lane_lo, k_sw, k).astype(BF16)
    for jb in range(tq // blk):
        vt_ref[:, (jb + 1) * blk:(jb + 2) * blk] = v[jb * blk:(jb + 1) * blk].T.astype(BF16)
    kvl_ref[0] = k[tq - blk:tq]
    kvl_ref[1] = v[tq - blk:tq]


def _attn_block(jb, first, attn_refs, out_ref, pk_ref, pv_ref, qh_ref, k2_ref, vt_ref, kvl_ref):
    _, _, sk_ref, _, on_ref = attn_refs
    tq = qh_ref.shape[1]
    blk = WINDOW
    nblk = tq // blk
    heads_per_group = qh_ref.shape[0] // N_KV_HEADS
    r0 = jb * blk

    ncolq = heads_per_group * blk
    kj = lax.broadcasted_iota(jnp.int32, (blk, ncolq), 0)
    qi = lax.broadcasted_iota(jnp.int32, (blk, ncolq), 1) % blk
    tri = kj <= qi

    ocols = []
    for g in range(N_KV_HEADS):
        lhs = jnp.concatenate([qh_ref[h, r0:r0 + blk, :]
                               for h in range(g * heads_per_group, (g + 1) * heads_per_group)], axis=0)
        st = lax.dot_general(k2_ref[g, r0:r0 + 2 * blk, :], lhs, (((1,), (1,)), ((), ())),
                             preferred_element_type=F32)
        s_prev = st[0:blk]
        if jb == 0:
            s_prev = s_prev + jnp.where(first, jnp.float32(-jnp.inf), jnp.float32(0.0))
        m = jnp.where(tri, st[blk:2 * blk], s_prev)
        sk = sk_ref[g:g + 1, :]
        mx = jnp.maximum(jnp.max(m, axis=0, keepdims=True), sk)
        p = jnp.exp2(m - mx)
        den = jnp.sum(p, axis=0, keepdims=True) + jnp.exp2(sk - mx)
        pz = jnp.zeros_like(p)
        pt = jnp.concatenate([jnp.where(tri, pz, p), jnp.where(tri, p, pz)], axis=0).astype(BF16)
        ot = jnp.dot(vt_ref[:, r0:r0 + 2 * blk], pt, preferred_element_type=F32)
        og = ot[g * HEAD_DIM:(g + 1) * HEAD_DIM, :] * (1.0 / den)
        for half in range(heads_per_group // 2):
            pair = jnp.concatenate([og[:, (2 * half) * blk:(2 * half + 1) * blk],
                                    og[:, (2 * half + 1) * blk:(2 * half + 2) * blk]], axis=0)
            ocols.append(pair.T)
    o = jnp.concatenate(ocols, axis=1)
    out_ref[r0:r0 + blk, :] = _rms(o, on_ref[...]).astype(out_ref.dtype)

    if jb == nblk - 1:
        k2_ref[:, 0:blk, :] = k2_ref[:, tq:tq + blk, :]
        vt_ref[:, 0:blk] = vt_ref[:, tq:tq + blk]
        pk_ref[0] = kvl_ref[0]
        pv_ref[0] = kvl_ref[1]


N_FFN_IN = 7
N_LRU = 8
N_ATTN = 5
MIXER_LRU_BLOCKS = 4


def _mixer_in_kernel(*refs, tiles_per_seq, ntile):
    x_ref, cos_ref, sin_ref = refs[0:3]
    p0 = 3
    g1_ref, wg_ref, wu_ref, wd_ref, gm_ref, winl_ref, wina_ref = refs[p0:p0 + N_FFN_IN]
    lru_refs = refs[p0 + N_FFN_IN:p0 + N_FFN_IN + N_LRU]
    attn_refs = refs[p0 + N_FFN_IN + N_LRU:p0 + N_FFN_IN + N_LRU + N_ATTN]
    o0 = p0 + N_FFN_IN + N_LRU + N_ATTN
    x1_ref, lo_ref, ao_ref, hl_ref, cs_ref, pk_ref, pv_ref = refs[o0:o0 + 7]
    (xn_ref, acc_ref, pl_ref, pa_ref, xh_ref, h_ref, xc_ref, zr_ref, zi_ref, gate_ref, ctail_ref,
     qh_ref, k2_ref, vt_ref, kvl_ref) = refs[o0 + 7:]

    i = pl.program_id(0)

    @pl.when(i == 0)
    def _():
        for ref in (pl_ref, pa_ref, xh_ref, h_ref, xc_ref, zr_ref, zi_ref, gate_ref, ctail_ref,
                    qh_ref, k2_ref, vt_ref, kvl_ref):
            ref[...] = jnp.zeros_like(ref)

    def step(with_ffn):
        first = lax.rem(jnp.maximum(i - 1, 0), tiles_per_seq) == 0
        _lru_prep(pl_ref, first, lru_refs, xh_ref, xc_ref, zr_ref, zi_ref, gate_ref, ctail_ref)
        _attn_prep(pa_ref, cos_ref, sin_ref, attn_refs, qh_ref, k2_ref, vt_ref, kvl_ref)
        for r in range(MIXER_LRU_BLOCKS):
            _lru_core(r, MIXER_LRU_BLOCKS, first, lru_refs, xc_ref, zr_ref, zi_ref, gate_ref, ctail_ref,
                      lo_ref, hl_ref, cs_ref, h_ref)
        for jb in range(pa_ref.shape[0] // WINDOW):
            _attn_block(jb, first, attn_refs, ao_ref, pk_ref, pv_ref, qh_ref, k2_ref, vt_ref, kvl_ref)
        if with_ffn:
            _ffn_in_tile(x_ref, (g1_ref, wg_ref, wu_ref, wd_ref), gm_ref, winl_ref, wina_ref,
                         x1_ref, pl_ref, pa_ref, xn_ref, acc_ref)

    pl.when(i < ntile)(functools.partial(step, True))
    pl.when(i >= ntile)(functools.partial(step, False))


def _mixer_in(x, seq_len, cos, sin, ffn_in_p, lru_p, attn_p):
    n, d = x.shape
    tm = TOKEN_TILE
    assert seq_len % tm == 0 and n % seq_len == 0 and tm % WINDOW == 0
    nseq = n // seq_len
    tps = seq_len // tm
    ntile = n // tm
    wg, winl, wina = ffn_in_p[1], ffn_in_p[5], ffn_in_p[6]
    w2, wa = winl.shape[1], wina.shape[1]
    c = w2 // 2
    aw = attn_p[4].shape[1]
    kvw = N_KV_HEADS * HEAD_DIM
    cur = lambda i: jnp.minimum(i, ntile - 1)
    prep = lambda i: jnp.maximum(i - 1, 0)
    core = prep
    params = list(ffn_in_p) + list(lru_p) + list(attn_p)
    weights = sum(p.size * p.dtype.itemsize for p in params)
    tiles = 2 * 4 * tm * 2 * d + 2 * 2 * tm * (c + aw) + 4 * tm * (w2 + wa) + tm * d * (2 + 4) + 6 * 4 * tm * c
    return pl.pallas_call(
        functools.partial(_mixer_in_kernel, tiles_per_seq=tps, ntile=ntile),
        grid=(ntile + 1,),
        in_specs=[pl.BlockSpec((tm, d), lambda i: (cur(i), 0)),
                  pl.BlockSpec((tm, V7X_LANES), lambda i: (lax.rem(prep(i), tps), 0)),
                  pl.BlockSpec((tm, V7X_LANES), lambda i: (lax.rem(prep(i), tps), 0))]
                 + [_resident(p.shape) for p in params],
        out_specs=[pl.BlockSpec((tm, d), lambda i: (cur(i), 0)),
                   pl.BlockSpec((tm, c), lambda i: (core(i), 0)),
                   pl.BlockSpec((tm, aw), lambda i: (core(i), 0)),
                   pl.BlockSpec((1, 1, c), lambda i: (core(i) // tps, 0, 0)),
                   pl.BlockSpec((1, CONV_W - 1, c), lambda i: (core(i) // tps, 0, 0)),
                   pl.BlockSpec((1, WINDOW, kvw), lambda i: (core(i) // tps, 0, 0)),
                   pl.BlockSpec((1, WINDOW, kvw), lambda i: (core(i) // tps, 0, 0))],
        out_shape=[jax.ShapeDtypeStruct((n, d), F32), jax.ShapeDtypeStruct((n, c), BF16),
                   jax.ShapeDtypeStruct((n, aw), BF16), jax.ShapeDtypeStruct((nseq, 1, c), F32),
                   jax.ShapeDtypeStruct((nseq, CONV_W - 1, c), F32), jax.ShapeDtypeStruct((nseq, WINDOW, kvw), F32),
                   jax.ShapeDtypeStruct((nseq, WINDOW, kvw), F32)],
        scratch_shapes=[pltpu.VMEM((tm, d), BF16), pltpu.VMEM((tm, d), F32),
                        pltpu.VMEM((tm, w2), F32), pltpu.VMEM((tm, wa), F32),
                        pltpu.VMEM((tm + V7X_SUBLANES, c), F32), pltpu.VMEM((1, c), F32),
                        pltpu.VMEM((tm, c), F32), pltpu.VMEM((tm, c), F32), pltpu.VMEM((tm, c), F32),
                        pltpu.VMEM((tm, c), F32), pltpu.VMEM((CONV_W - 1, c), F32),
                        pltpu.VMEM((2 * aw // V7X_LANES, tm, V7X_LANES), BF16),
                        pltpu.VMEM((N_KV_HEADS, WINDOW + tm, V7X_LANES), BF16),
                        pltpu.VMEM((V7X_LANES, WINDOW + tm), BF16),
                        pltpu.VMEM((2, WINDOW, kvw), F32)],
        compiler_params=pltpu.CompilerParams(dimension_semantics=("arbitrary",),
                                             vmem_limit_bytes=_vmem_limit(weights + tiles + (20 << 20))),
        name="mixer_in",
    )(x, cos, sin, *params)


def _ffn_in_kernel(x_ref, g1_ref, wg_ref, wu_ref, wd_ref, gm_ref, winl_ref, wina_ref,
                   x1_ref, pl_ref, pa_ref, xn_ref, acc_ref):
    _ffn_in_tile(x_ref, (g1_ref, wg_ref, wu_ref, wd_ref), gm_ref, winl_ref, wina_ref,
                 x1_ref, pl_ref, pa_ref, xn_ref, acc_ref)


def _ffn_in(x, g1, wg, wu, wd, gm, winl, wina):
    n, d = x.shape
    tm = min(TOKEN_TILE, n)
    wl, wa = winl.shape[1], wina.shape[1]
    row = lambda w: pl.BlockSpec((tm, w), lambda i: (i, 0))
    weights = 2 * (wg.size + wu.size + wd.size + winl.size + wina.size)
    tiles = 2 * 4 * tm * (2 * d + wl + wa) + tm * d * (2 + 4) + 4 * 4 * tm * max(FF_CHUNK, wl)
    return pl.pallas_call(
        _ffn_in_kernel,
        grid=(n // tm,),
        in_specs=[row(d), _resident(g1.shape), _resident(wg.shape), _resident(wu.shape), _resident(wd.shape),
                  _resident(gm.shape), _resident(winl.shape), _resident(wina.shape)],
        out_specs=[row(d), row(wl), row(wa)],
        out_shape=[jax.ShapeDtypeStruct((n, d), F32), jax.ShapeDtypeStruct((n, wl), F32),
                   jax.ShapeDtypeStruct((n, wa), F32)],
        scratch_shapes=[pltpu.VMEM((tm, d), BF16), pltpu.VMEM((tm, d), F32)],
        compiler_params=pltpu.CompilerParams(dimension_semantics=("arbitrary",),
                                             vmem_limit_bytes=_vmem_limit(weights + tiles + (8 << 20))),
        name="ffn_in",
    )(x, g1, wg, wu, wd, gm, winl, wina)


def _out_ffn_kernel(lo_ref, ao_ref, x1_ref, wol_ref, woa_ref, g2_ref, wg_ref, wu_ref, wd_ref,
                    out_ref, xn_ref, acc_ref):
    y = jnp.dot(lo_ref[...], wol_ref[...], preferred_element_type=F32)
    y = y + jnp.dot(ao_ref[...], woa_ref[...], preferred_element_type=F32)
    x2 = x1_ref[...] + y
    out_ref[...] = x2
    xn_ref[...] = _rms(x2, g2_ref[...]).astype(BF16)
    out_ref[...] = out_ref[...] + 0.5 * _ffn(xn_ref, wg_ref, wu_ref, wd_ref, acc_ref)


def _out_ffn(lo, ao, x1, wol, woa, g2, wg, wu, wd):
    n, d = x1.shape
    tm = min(TOKEN_TILE, n)
    row = lambda w: pl.BlockSpec((tm, w), lambda i: (i, 0))
    weights = 2 * (wg.size + wu.size + wd.size + wol.size + woa.size)
    tiles = 2 * tm * (2 * lo.shape[1] + 2 * ao.shape[1] + 8 * d) + tm * d * (2 + 4) + 4 * 4 * tm * d
    return pl.pallas_call(
        _out_ffn_kernel,
        grid=(n // tm,),
        in_specs=[row(lo.shape[1]), row(ao.shape[1]), row(d), _resident(wol.shape), _resident(woa.shape),
                  _resident(g2.shape), _resident(wg.shape), _resident(wu.shape), _resident(wd.shape)],
        out_specs=row(d),
        out_shape=jax.ShapeDtypeStruct((n, d), F32),
        scratch_shapes=[pltpu.VMEM((tm, d), BF16), pltpu.VMEM((tm, d), F32)],
        compiler_params=pltpu.CompilerParams(dimension_semantics=("arbitrary",),
                                             vmem_limit_bytes=_vmem_limit(weights + tiles + (8 << 20))),
        name="out_ffn",
    )(lo, ao, x1, wol, woa, g2, wg, wu, wd)


def _qk_prep_kernel(pa_ref, cos_ref, sin_ref, qn_ref, kn_ref, ones_ref, q_ref, k_ref):
    lanes = V7X_LANES
    aw = q_ref.shape[1]
    cos = cos_ref[...]
    sin = sin_ref[...]
    for cc in range(aw // lanes):
        q_ref[:, cc * lanes:(cc + 1) * lanes] = _head_norm_rope(
            pa_ref[:, cc * lanes:(cc + 1) * lanes], qn_ref[...], cos, sin, ones_ref) * (HEAD_DIM ** -0.5)
    k_ref[...] = _head_norm_rope(pa_ref[:, aw:aw + lanes], kn_ref[...], cos, sin, ones_ref)


def _qk_prep(pa, cos, sin, qn, kn, ones, aw):
    n = pa.shape[0]
    return pl.pallas_call(
        _qk_prep_kernel,
        out_shape=[jax.ShapeDtypeStruct((n, aw), F32), jax.ShapeDtypeStruct((n, N_KV_HEADS * HEAD_DIM), F32)],
        compiler_params=pltpu.CompilerParams(vmem_limit_bytes=_vmem_limit(32 << 20)),
        name="qk_prep_sample",
    )(pa, cos, sin, qn, kn, ones)


def _attn_cached_kernel(q_ref, kn_ref, vn_ref, ck_ref, cv_ref, sk_ref, on_ref,
                        out_ref, nk_ref, nv_ref, kk_ref, vv_ref):
    sb, t, aw = q_ref.shape
    w = ck_ref.shape[1]
    lanes = V7X_LANES
    jpad = kk_ref.shape[1]

    kk_ref[:, 0:w, :] = ck_ref[...]
    vv_ref[:, 0:w, :] = cv_ref[...]
    kk_ref[:, w:w + t, :] = kn_ref[...]
    vv_ref[:, w:w + t, :] = vn_ref[...]
    kk_ref[:, w + t:jpad, :] = jnp.zeros((sb, jpad - w - t, lanes), F32)
    vv_ref[:, w + t:jpad, :] = jnp.zeros((sb, jpad - w - t, lanes), F32)
    nk_ref[...] = kk_ref[:, t:t + w, :]
    nv_ref[...] = vv_ref[:, t:t + w, :]

    kk = kk_ref[...]
    vv = vv_ref[...]
    k_sw = pltpu.roll(kk, HEAD_DIM, 2)
    v_sw = pltpu.roll(vv, HEAD_DIM, 2)
    lane_k = lax.broadcasted_iota(jnp.int32, kk.shape, 2) < HEAD_DIM
    lane_q = lax.broadcasted_iota(jnp.int32, (sb, t, lanes), 2) < HEAD_DIM
    k2 = (jnp.where(lane_k, kk, k_sw).astype(BF16), jnp.where(lane_k, k_sw, kk).astype(BF16))
    v2 = (vv.astype(BF16), v_sw.astype(BF16))

    qcols = [q_ref[:, :, cc * lanes:(cc + 1) * lanes] for cc in range(aw // lanes)]
    tq = lax.broadcasted_iota(jnp.int32, (sb, 4 * t, jpad), 1) % t
    j = lax.broadcasted_iota(jnp.int32, (sb, 4 * t, jpad), 2)
    valid = jnp.logical_and(j > tq, j <= tq + w)
    neg_inf = jnp.float32(-jnp.inf)

    ocols = []
    for g in range(N_KV_HEADS):
        lhs = _group_lhs(qcols, g, lane_q, 1)
        s = jnp.einsum('bqd,bjd->bqj', lhs, k2[g], preferred_element_type=F32)
        s = jnp.where(valid, s, neg_inf)
        heads = (4 * g, 4 * g + 2, 4 * g + 1, 4 * g + 3)
        sk = jnp.concatenate([jnp.broadcast_to(sk_ref[h:h + 1, 0:1].reshape(1, 1, 1), (sb, t, 1)) for h in heads],
                             axis=1)
        mx = jnp.maximum(jnp.max(s, axis=-1, keepdims=True), sk)
        p = jnp.exp(s - mx)
        den = jnp.sum(p, axis=-1, keepdims=True) + jnp.exp(sk - mx)
        pb = p.astype(BF16)
        o_even = jnp.einsum('bqj,bjd->bqd', pb[:, 0:2 * t], v2[g], preferred_element_type=F32) / den[:, 0:2 * t]
        o_odd = jnp.einsum('bqj,bjd->bqd', pb[:, 2 * t:4 * t], v2[1 - g], preferred_element_type=F32) / den[:, 2 * t:4 * t]
        for half in range(2):
            ocols.append(jnp.where(lane_q, o_even[:, half * t:(half + 1) * t], o_odd[:, half * t:(half + 1) * t]))
    o = jnp.concatenate(ocols, axis=2)
    out_ref[...] = _rms(o, on_ref[...].reshape(1, 1, aw)).astype(out_ref.dtype)


def _attn_cached(q3, kn3, vn3, ck, cv, sk, on):
    s, t, aw = q3.shape
    w, kvw = ck.shape[1], ck.shape[2]
    sb = min(SAMPLE_SEQ_TILE, s)
    jpad = -(-(w + t) // V7X_SUBLANES) * V7X_SUBLANES
    seq = lambda a, c: pl.BlockSpec((sb, a, c), lambda i: (i, 0, 0))
    return pl.pallas_call(
        _attn_cached_kernel,
        grid=(s // sb,),
        in_specs=[seq(t, aw), seq(t, kvw), seq(t, kvw), seq(w, kvw), seq(w, kvw), _resident(sk.shape), _resident(on.shape)],
        out_specs=[seq(t, aw), seq(w, kvw), seq(w, kvw)],
        out_shape=[jax.ShapeDtypeStruct((s, t, aw), BF16), jax.ShapeDtypeStruct((s, w, kvw), F32),
                   jax.ShapeDtypeStruct((s, w, kvw), F32)],
        scratch_shapes=[pltpu.VMEM((sb, jpad, kvw), F32), pltpu.VMEM((sb, jpad, kvw), F32)],
        compiler_params=pltpu.CompilerParams(dimension_semantics=("arbitrary",),
                                             vmem_limit_bytes=_vmem_limit(40 << 20)),
        name="attn_sample",
    )(q3, kn3, vn3, ck, cv, sk, on)


def _rope_tables(pos):
    half = HEAD_DIM // 2
    lane = jnp.arange(V7X_LANES, dtype=jnp.int32)
    inv_freq = ROPE_THETA ** (-(lane % half).astype(F32) / half)
    ang = pos.astype(F32)[:, None] * inv_freq[None, :]
    sign = jnp.where((lane % HEAD_DIM) < half, -1.0, 1.0).astype(F32)
    return jnp.cos(ang), jnp.sin(ang) * sign[None, :]


def _block_diag(w):
    n, c, d = w.shape
    eye = jnp.eye(n, dtype=w.dtype)
    return (w[:, :, None, :] * eye[:, None, :, None]).reshape(n * c, n * d)


def _row(v):
    return v.reshape(1, -1).astype(F32)


def kernel(x_prompt, x_sample, state_lru_h, state_conv, cache_k, cache_v, ffn1_norm, ffn1_w_gate, ffn1_w_up, ffn1_w_down, mix_norm, w_in, conv_w, conv_b, lru_wa, lru_ba, lru_wx, lru_bx, lru_lambda, q_norm, k_norm, attn_sinks, lru_out_norm, attn_out_norm, w_out, ffn2_norm, ffn2_w_gate, ffn2_w_up, ffn2_w_down):
    depth = w_in.shape[0]
    b, t, d = x_prompt.shape
    s, ts, _ = x_sample.shape
    c = lru_lambda.shape[1]
    aw = attn_out_norm.shape[1]
    kvw = N_KV_HEADS * HEAD_DIM
    win = cache_k.shape[2]
    assert t >= WINDOW and win == WINDOW

    cos_p, sin_p = _rope_tables(jnp.arange(t, dtype=jnp.int32))
    cos_s, sin_s = _rope_tables(PAST_LEN + jnp.arange(ts, dtype=jnp.int32))
    cos_s, sin_s = jnp.tile(cos_s, (s, 1)), jnp.tile(sin_s, (s, 1))
    ones = _block_diag(jnp.full((V7X_LANES // HEAD_DIM, HEAD_DIM, HEAD_DIM), 1.0 / HEAD_DIM, BF16))
    log2e = 1.0 / jnp.log(jnp.float32(2.0))

    yp = x_prompt.reshape(b * t, d)
    ys = x_sample.reshape(s * ts, d)
    outs = [[] for _ in range(8)]
    for l in range(depth):
        ffn1 = (_row(ffn1_norm[l]), (0.5 * ffn1_w_gate[l]).astype(BF16), ffn1_w_up[l].astype(BF16),
                ffn1_w_down[l].astype(BF16))
        ffn2 = (_row(ffn2_norm[l]), (0.5 * ffn2_w_gate[l]).astype(BF16), ffn2_w_up[l].astype(BF16),
                ffn2_w_down[l].astype(BF16))
        win_b = w_in[l].astype(BF16)
        ffn_in_p = ffn1 + (_row(mix_norm[l]), win_b[:, :2 * c], win_b[:, 2 * c:])
        wout_b = w_out[l].astype(BF16)
        wol, woa = wout_b[:c], wout_b[c:]
        lru_p = (conv_w[l], _row(conv_b[l]), (0.5 * _block_diag(lru_wa[l])).astype(BF16), _row(0.5 * lru_ba[l]),
                 (0.5 * _block_diag(lru_wx[l])).astype(BF16), _row(0.5 * lru_bx[l]), _row(lru_lambda[l]),
                 _row(lru_out_norm[l]))
        reps = V7X_LANES // HEAD_DIM
        qn = _row(jnp.tile(q_norm[l], reps))
        kn = _row(jnp.tile(k_norm[l], reps))
        sk = jnp.broadcast_to(attn_sinks[l].astype(F32)[:, None], (attn_sinks.shape[1], V7X_LANES))
        qn_p = qn * (HEAD_DIM ** -0.5 * log2e)
        sk_rows = jnp.repeat(attn_sinks[l].astype(F32).reshape(N_KV_HEADS, -1), WINDOW, axis=1) * log2e
        a_on = _row(attn_out_norm[l])

        x1, lo, ao, hl, cs, pk, pv = _mixer_in(yp, t, cos_p, sin_p, ffn_in_p, lru_p, (qn_p, kn, sk_rows, ones, a_on))
        yp = _out_ffn(lo, ao, x1, wol, woa, *ffn2)
        outs[0].append(hl.reshape(b, c))
        outs[1].append(cs)
        outs[2].append(pk.reshape(b, WINDOW, N_KV_HEADS, HEAD_DIM))
        outs[3].append(pv.reshape(b, WINDOW, N_KV_HEADS, HEAD_DIM))

        x1s, pl_s, pa_s = _ffn_in(ys, *ffn_in_p)
        x_tm = pl_s.reshape(s, ts, 2 * c).transpose(1, 0, 2)
        cst_tm = state_conv[l].transpose(1, 0, 2)
        lo_tm, hls, cs_tm = _lru_sample(x_tm, cst_tm, state_lru_h[l], *lru_p)
        q_s, k_s = _qk_prep(pa_s, cos_s, sin_s, qn, kn, ones, aw)
        v_s = pa_s[:, aw + kvw:]
        aos, nk, nv = _attn_cached(q_s.reshape(s, ts, aw), k_s.reshape(s, ts, kvw), v_s.reshape(s, ts, kvw),
                                   cache_k[l].reshape(s, win, kvw), cache_v[l].reshape(s, win, kvw), sk, a_on)
        ys = _out_ffn(lo_tm.transpose(1, 0, 2).reshape(s * ts, c), aos.reshape(s * ts, aw), x1s, wol, woa, *ffn2)
        outs[4].append(hls)
        outs[5].append(cs_tm.transpose(1, 0, 2))
        outs[6].append(nk.reshape(s, win, N_KV_HEADS, HEAD_DIM))
        outs[7].append(nv.reshape(s, win, N_KV_HEADS, HEAD_DIM))

    st = [jnp.stack(o) for o in outs]
    return (yp.reshape(b, t, d), ys.reshape(s, ts, d), st[0], st[1], st[2], st[3], st[4], st[5], st[6], st[7])
```

```python
import functools

import jax
import jax.numpy as jnp
from jax import lax
from jax.experimental import pallas as pl
from jax.experimental.pallas import tpu as pltpu

F32 = jnp.float32
BF16 = jnp.bfloat16

NORM_EPS = 1e-6
LRU_C = 8.0
CONV_W = 4
HEAD_DIM = 64
N_KV_HEADS = 2
WINDOW = 128
ROPE_THETA = 10000.0
PAST_LEN = 16384

V7X_LANES = 128
V7X_SUBLANES = 8
V7X_MXU_DIM = 256
V7X_VMEM_BYTES = 64 * 1024 * 1024

TOKEN_TILE = 512
SAMPLE_SEQ_TILE = 16
FF_CHUNK = V7X_MXU_DIM


MIB = 1024 * 1024
VMEM_KEEP_FREE_BYTES = 4 * MIB
FFN_TEMP_BYTES = 8 * MIB
MIXER_TEMP_BYTES = 20 * MIB
SMALL_KERNEL_VMEM_BYTES = 40 * MIB


def _vmem_limit(nbytes):
    return int(min(nbytes, V7X_VMEM_BYTES - VMEM_KEEP_FREE_BYTES))


def _resident(shape):
    nd = len(shape)
    return pl.BlockSpec(shape, lambda *_: (0,) * nd, pipeline_mode=pl.Buffered(1))


def _rms(x, g):
    ms = jnp.mean(x * x, axis=-1, keepdims=True)
    return (x * lax.rsqrt(ms + NORM_EPS)) * g


def _ffn(xn_ref, wg_ref, wu_ref, wd_ref, acc_ref):
    acc_ref[...] = jnp.zeros_like(acc_ref)
    for c0 in range(0, wg_ref.shape[1], FF_CHUNK):
        xn = xn_ref[...]
        h = jnp.dot(xn, wg_ref[:, c0:c0 + FF_CHUNK], preferred_element_type=F32)
        u = jnp.dot(xn, wu_ref[:, c0:c0 + FF_CHUNK], preferred_element_type=F32)
        a = ((h + h * jnp.tanh(h)) * u).astype(BF16)
        acc_ref[...] += jnp.dot(a, wd_ref[c0:c0 + FF_CHUNK, :], preferred_element_type=F32)
    return acc_ref[...]


def _ffn_in_tile(x_ref, ffn_refs, gm_ref, winl_ref, wina_ref, x1_ref, pl_ref, pa_ref, xn_ref, acc_ref):
    g1_ref, wg_ref, wu_ref, wd_ref = ffn_refs
    xn_ref[...] = _rms(x_ref[...], g1_ref[...]).astype(BF16)
    x1 = x_ref[...] + 0.5 * _ffn(xn_ref, wg_ref, wu_ref, wd_ref, acc_ref)
    x1_ref[...] = x1
    xn_ref[...] = _rms(x1, gm_ref[...]).astype(BF16)
    pl_ref[...] = jnp.dot(xn_ref[...], winl_ref[...], preferred_element_type=F32)
    pa_ref[...] = jnp.dot(xn_ref[...], wina_ref[...], preferred_element_type=F32)


def _lru_preact(xc, gate_w):
    wa_ref, ba_ref, wx_ref, bx_ref = gate_w
    xcb = xc.astype(BF16)
    zr = jnp.dot(xcb, wa_ref[...], preferred_element_type=F32) + ba_ref[...]
    zi = jnp.dot(xcb, wx_ref[...], preferred_element_type=F32) + bx_ref[...]
    return zr, zi


def _lru_terms(xc, zr, zi, lam_ref):
    i = 0.5 * jnp.tanh(zi) + 0.5
    half_c = (-0.5 * LRU_C) * jax.nn.softplus(-lam_ref[...])
    log_a = half_c * jnp.tanh(zr) + half_c
    a = jnp.exp(log_a)
    w = -jnp.tanh(log_a) * (a * a + 1.0)
    root = jnp.where(w > 0.0, w * lax.rsqrt(w), 0.0)
    u = root * (i * xc)
    return a, u


def _lru_prep(p_ref, first, lru_refs, xh_ref, xc_ref, zr_ref, zi_ref, gate_ref, ctail_ref):
    cw_ref, cb_ref, wa_ref, ba_ref, wx_ref, bx_ref, _, _ = lru_refs
    tt = p_ref.shape[0]
    c = xc_ref.shape[1]
    hist = V7X_SUBLANES

    xh_ref[0:hist, :] = jnp.where(first, jnp.zeros((hist, c), F32), xh_ref[0:hist, :])
    x = p_ref[:, 0:c]
    xh_ref[hist:hist + tt, :] = x
    xg = xh_ref[...].reshape(tt // hist + 1, hist, c)
    sub = lax.broadcasted_iota(jnp.int32, (tt // hist, hist, c), 1)
    w = cw_ref[...]
    xc = cb_ref[...].reshape(1, 1, c)
    for j in range(CONV_W - 1):
        k = CONV_W - 1 - j
        rolled = pltpu.roll(xg, k, 1)
        xc = xc + jnp.where(sub >= k, rolled[1:], rolled[:-1]) * w[j:j + 1, :].reshape(1, 1, c)
    xc = (xc + xg[1:] * w[CONV_W - 1:CONV_W, :].reshape(1, 1, c)).reshape(tt, c)
    xh_ref[0:hist, :] = x[tt - hist:tt, :]
    ctail_ref[...] = x[tt - (CONV_W - 1):tt, :]

    xc_ref[...] = xc
    zr, zi = _lru_preact(xc, (wa_ref, ba_ref, wx_ref, bx_ref))
    zr_ref[...] = zr
    zi_ref[...] = zi
    gate_ref[...] = p_ref[:, c:2 * c]


def _lru_core(first, lru_refs, xc_ref, zr_ref, zi_ref, gate_ref, ctail_ref, out_ref, hl_ref, cs_ref, h_ref):
    lam_ref, on_ref = lru_refs[6], lru_refs[7]
    tt, c = xc_ref.shape

    a, u = _lru_terms(xc_ref[...], zr_ref[...], zi_ref[...], lam_ref)

    groups = tt // V7X_SUBLANES
    a3 = a.reshape(groups, V7X_SUBLANES, c)
    u3 = u.reshape(groups, V7X_SUBLANES, c)
    sub = lax.broadcasted_iota(jnp.int32, a3.shape, 1)
    k = 1
    while k < V7X_SUBLANES:
        m = sub >= k
        a_prev = pltpu.roll(a3, k, 1)
        u_prev = pltpu.roll(u3, k, 1)
        u3 = jnp.where(m, a3 * u_prev + u3, u3)
        a3 = jnp.where(m, a3 * a_prev, a3)
        k *= 2

    h = jnp.where(first, jnp.zeros((1, c), F32), h_ref[...])
    hs = []
    for g in range(groups):
        hr = a3[g] * h + u3[g]
        hs.append(hr)
        h = hr[V7X_SUBLANES - 1:V7X_SUBLANES, :]
    h_ref[...] = h

    y = jnp.concatenate(hs, axis=0) * jax.nn.gelu(gate_ref[...])
    out_ref[...] = _rms(y, on_ref[...]).astype(out_ref.dtype)
    hl_ref[0] = h
    cs_ref[0] = ctail_ref[...]


def _lru_sample_kernel(x_ref, cst_ref, h0_ref, cw_ref, cb_ref, wa_ref, ba_ref, wx_ref, bx_ref, lam_ref, on_ref,
                       out_ref, hl_ref, cs_ref):
    t = x_ref.shape[0]
    c = out_ref.shape[2]
    w = cw_ref[...]
    rows = [cst_ref[j] for j in range(CONV_W - 1)] + [x_ref[j, :, 0:c] for j in range(t)]
    h = h0_ref[...]
    for s in range(t):
        xc = cb_ref[...]
        for j in range(CONV_W):
            xc = xc + rows[s + j] * w[j:j + 1, :]
        zr, zi = _lru_preact(xc, (wa_ref, ba_ref, wx_ref, bx_ref))
        a, u = _lru_terms(xc, zr, zi, lam_ref)
        h = a * h + u
        y = h * jax.nn.gelu(x_ref[s, :, c:2 * c])
        out_ref[s] = _rms(y, on_ref[...]).astype(out_ref.dtype)
    hl_ref[...] = h
    for j in range(CONV_W - 1):
        cs_ref[j] = rows[t + j]


def _lru_sample(x_tm, cst_tm, h0, cw, cb, wa, ba, wx, bx, lam, on):
    t, s, w2 = x_tm.shape
    c = w2 // 2
    return pl.pallas_call(
        _lru_sample_kernel,
        out_shape=[jax.ShapeDtypeStruct((t, s, c), BF16), jax.ShapeDtypeStruct((s, c), F32),
                   jax.ShapeDtypeStruct((CONV_W - 1, s, c), F32)],
        compiler_params=pltpu.CompilerParams(vmem_limit_bytes=_vmem_limit(SMALL_KERNEL_VMEM_BYTES)),
        name="lru_sample",
    )(x_tm, cst_tm, h0, cw, cb, wa, ba, wx, bx, lam, on)


def _head_norm_rope(xc, gain, cos, sin, ones_ref):
    x2 = xc * xc
    hi = x2.astype(BF16)
    lo = (x2 - hi.astype(F32)).astype(BF16)
    ss = jnp.dot(hi, ones_ref[...], preferred_element_type=F32) + jnp.dot(lo, ones_ref[...], preferred_element_type=F32)
    xn = (xc * lax.rsqrt(ss + NORM_EPS)) * gain
    lane = lax.broadcasted_iota(jnp.int32, xn.shape, xn.ndim - 1)
    first_half = (lane & (HEAD_DIM // 2)) == 0
    ax = xn.ndim - 1
    swapped = jnp.where(first_half, pltpu.roll(xn, V7X_LANES - HEAD_DIM // 2, ax), pltpu.roll(xn, HEAD_DIM // 2, ax))
    return xn * cos + swapped * sin


def _group_lhs(qcols, g, lane_lo, axis):
    zero = jnp.zeros_like(qcols[0])
    parts = [jnp.where(lane_lo, qcols[2 * g], zero), jnp.where(lane_lo, qcols[2 * g + 1], zero),
             jnp.where(lane_lo, zero, qcols[2 * g]), jnp.where(lane_lo, zero, qcols[2 * g + 1])]
    return jnp.concatenate(parts, axis=axis).astype(BF16)


def _attn_prep(pa_ref, cos_ref, sin_ref, attn_refs, qh_ref, k2_ref, vt_ref, kvl_ref):
    qn_ref, kn_ref, _, ones_ref, _ = attn_refs
    tq = pa_ref.shape[0]
    blk = WINDOW
    lanes = V7X_LANES
    ncol = qh_ref.shape[0] // 2
    aw = ncol * lanes

    cos = cos_ref[...]
    sin = sin_ref[...]
    cols = [pa_ref[:, cc * lanes:(cc + 1) * lanes] for cc in range(ncol + 1)]
    sq = [x * x for x in cols]
    hi = [x.astype(BF16) for x in sq]
    lo = [(x - h.astype(F32)).astype(BF16) for x, h in zip(sq, hi)]
    ss = (jnp.dot(jnp.concatenate(hi, axis=0), ones_ref[...], preferred_element_type=F32)
          + jnp.dot(jnp.concatenate(lo, axis=0), ones_ref[...], preferred_element_type=F32))
    lane = lax.broadcasted_iota(jnp.int32, (tq, lanes), 1)
    first_half = (lane & (HEAD_DIM // 2)) == 0
    lane_lo = lane < HEAD_DIM
    rot = []
    for cc, x in enumerate(cols):
        gain = qn_ref[...] if cc < ncol else kn_ref[...]
        xn = (x * lax.rsqrt(ss[cc * tq:(cc + 1) * tq] + NORM_EPS)) * gain
        swapped = jnp.where(first_half, pltpu.roll(xn, lanes - HEAD_DIM // 2, 1), pltpu.roll(xn, HEAD_DIM // 2, 1))
        rot.append(xn * cos + swapped * sin)
    k = rot[ncol]
    v = pa_ref[:, aw + lanes:aw + 2 * lanes]

    zero = jnp.zeros((tq, lanes), F32)
    for cc in range(ncol):
        qh_ref[2 * cc] = jnp.where(lane_lo, rot[cc], zero).astype(BF16)
        qh_ref[2 * cc + 1] = jnp.where(lane_lo, zero, rot[cc]).astype(BF16)

    k_sw = pltpu.roll(k, HEAD_DIM, 1)
    k2_ref[0, blk:blk + tq, :] = jnp.where(lane_lo, k, k_sw).astype(BF16)
    k2_ref[1, blk:blk + tq, :] = jnp.where(lane_lo, k_sw, k).astype(BF16)
    for jb in range(tq // blk):
        vt_ref[:, (jb + 1) * blk:(jb + 2) * blk] = v[jb * blk:(jb + 1) * blk].T.astype(BF16)
    kvl_ref[0] = k[tq - blk:tq]
    kvl_ref[1] = v[tq - blk:tq]


def _attn_block(jb, first, attn_refs, out_ref, pk_ref, pv_ref, qh_ref, k2_ref, vt_ref, kvl_ref):
    _, _, sk_ref, _, on_ref = attn_refs
    tq = qh_ref.shape[1]
    blk = WINDOW
    nblk = tq // blk
    heads_per_group = qh_ref.shape[0] // N_KV_HEADS
    r0 = jb * blk

    ncolq = heads_per_group * blk
    kj = lax.broadcasted_iota(jnp.int32, (blk, ncolq), 0)
    qi = lax.broadcasted_iota(jnp.int32, (blk, ncolq), 1) % blk
    tri = kj <= qi

    ocols = []
    for g in range(N_KV_HEADS):
        lhs = jnp.concatenate([qh_ref[h, r0:r0 + blk, :]
                               for h in range(g * heads_per_group, (g + 1) * heads_per_group)], axis=0)
        st = lax.dot_general(k2_ref[g, r0:r0 + 2 * blk, :], lhs, (((1,), (1,)), ((), ())),
                             preferred_element_type=F32)
        s_prev = st[0:blk]
        if jb == 0:
            s_prev = s_prev + jnp.where(first, jnp.float32(-jnp.inf), jnp.float32(0.0))
        m = jnp.where(tri, st[blk:2 * blk], s_prev)
        sk = sk_ref[g:g + 1, :]
        mx = jnp.maximum(jnp.max(m, axis=0, keepdims=True), sk)
        p = jnp.exp2(m - mx)
        den = jnp.sum(p, axis=0, keepdims=True) + jnp.exp2(sk - mx)
        pz = jnp.zeros_like(p)
        pt = jnp.concatenate([jnp.where(tri, pz, p), jnp.where(tri, p, pz)], axis=0).astype(BF16)
        ot = jnp.dot(vt_ref[:, r0:r0 + 2 * blk], pt, preferred_element_type=F32)
        og = ot[g * HEAD_DIM:(g + 1) * HEAD_DIM, :] * (1.0 / den)
        for half in range(heads_per_group // 2):
            pair = jnp.concatenate([og[:, (2 * half) * blk:(2 * half + 1) * blk],
                                    og[:, (2 * half + 1) * blk:(2 * half + 2) * blk]], axis=0)
            ocols.append(pair.T)
    o = jnp.concatenate(ocols, axis=1)
    out_ref[r0:r0 + blk, :] = _rms(o, on_ref[...]).astype(out_ref.dtype)

    if jb == nblk - 1:
        k2_ref[:, 0:blk, :] = k2_ref[:, tq:tq + blk, :]
        vt_ref[:, 0:blk] = vt_ref[:, tq:tq + blk]
        pk_ref[0] = kvl_ref[0]
        pv_ref[0] = kvl_ref[1]


N_FFN_IN = 7
N_LRU = 8
N_ATTN = 5


def _mixer_in_kernel(*refs, tiles_per_seq, ntile):
    x_ref, cos_ref, sin_ref = refs[0:3]
    p0 = 3
    g1_ref, wg_ref, wu_ref, wd_ref, gm_ref, winl_ref, wina_ref = refs[p0:p0 + N_FFN_IN]
    lru_refs = refs[p0 + N_FFN_IN:p0 + N_FFN_IN + N_LRU]
    attn_refs = refs[p0 + N_FFN_IN + N_LRU:p0 + N_FFN_IN + N_LRU + N_ATTN]
    o0 = p0 + N_FFN_IN + N_LRU + N_ATTN
    x1_ref, lo_ref, ao_ref, hl_ref, cs_ref, pk_ref, pv_ref = refs[o0:o0 + 7]
    (xn_ref, acc_ref, pl_ref, pa_ref, xh_ref, h_ref, xc_ref, zr_ref, zi_ref, gate_ref, ctail_ref,
     qh_ref, k2_ref, vt_ref, kvl_ref) = refs[o0 + 7:]

    i = pl.program_id(0)

    @pl.when(i == 0)
    def _():
        for ref in (pl_ref, pa_ref, xh_ref, h_ref, xc_ref, zr_ref, zi_ref, gate_ref, ctail_ref,
                    qh_ref, k2_ref, vt_ref, kvl_ref):
            ref[...] = jnp.zeros_like(ref)

    def step(with_ffn):
        first = lax.rem(jnp.maximum(i - 1, 0), tiles_per_seq) == 0
        _lru_prep(pl_ref, first, lru_refs, xh_ref, xc_ref, zr_ref, zi_ref, gate_ref, ctail_ref)
        _attn_prep(pa_ref, cos_ref, sin_ref, attn_refs, qh_ref, k2_ref, vt_ref, kvl_ref)
        _lru_core(first, lru_refs, xc_ref, zr_ref, zi_ref, gate_ref, ctail_ref, lo_ref, hl_ref, cs_ref, h_ref)
        for jb in range(pa_ref.shape[0] // WINDOW):
            _attn_block(jb, first, attn_refs, ao_ref, pk_ref, pv_ref, qh_ref, k2_ref, vt_ref, kvl_ref)
        if with_ffn:
            _ffn_in_tile(x_ref, (g1_ref, wg_ref, wu_ref, wd_ref), gm_ref, winl_ref, wina_ref,
                         x1_ref, pl_ref, pa_ref, xn_ref, acc_ref)

    pl.when(i < ntile)(functools.partial(step, True))
    pl.when(i >= ntile)(functools.partial(step, False))


def _mixer_in(x, seq_len, cos, sin, ffn_in_p, lru_p, attn_p):
    n, d = x.shape
    tm = TOKEN_TILE
    assert seq_len % tm == 0 and n % seq_len == 0 and tm % WINDOW == 0
    nseq = n // seq_len
    tps = seq_len // tm
    ntile = n // tm
    wg, winl, wina = ffn_in_p[1], ffn_in_p[5], ffn_in_p[6]
    w2, wa = winl.shape[1], wina.shape[1]
    c = w2 // 2
    aw = attn_p[4].shape[1]
    kvw = N_KV_HEADS * HEAD_DIM
    cur = lambda i: jnp.minimum(i, ntile - 1)
    mix = lambda i: jnp.maximum(i - 1, 0)
    params = list(ffn_in_p) + list(lru_p) + list(attn_p)
    weights = sum(p.size * p.dtype.itemsize for p in params)
    tiles = 2 * 4 * tm * 2 * d + 2 * 2 * tm * (c + aw) + 4 * tm * (w2 + wa) + tm * d * (2 + 4) + 6 * 4 * tm * c
    return pl.pallas_call(
        functools.partial(_mixer_in_kernel, tiles_per_seq=tps, ntile=ntile),
        grid=(ntile + 1,),
        in_specs=[pl.BlockSpec((tm, d), lambda i: (cur(i), 0)),
                  pl.BlockSpec((tm, V7X_LANES), lambda i: (lax.rem(mix(i), tps), 0)),
                  pl.BlockSpec((tm, V7X_LANES), lambda i: (lax.rem(mix(i), tps), 0))]
                 + [_resident(p.shape) for p in params],
        out_specs=[pl.BlockSpec((tm, d), lambda i: (cur(i), 0)),
                   pl.BlockSpec((tm, c), lambda i: (mix(i), 0)),
                   pl.BlockSpec((tm, aw), lambda i: (mix(i), 0)),
                   pl.BlockSpec((1, 1, c), lambda i: (mix(i) // tps, 0, 0)),
                   pl.BlockSpec((1, CONV_W - 1, c), lambda i: (mix(i) // tps, 0, 0)),
                   pl.BlockSpec((1, WINDOW, kvw), lambda i: (mix(i) // tps, 0, 0)),
                   pl.BlockSpec((1, WINDOW, kvw), lambda i: (mix(i) // tps, 0, 0))],
        out_shape=[jax.ShapeDtypeStruct((n, d), F32), jax.ShapeDtypeStruct((n, c), BF16),
                   jax.ShapeDtypeStruct((n, aw), BF16), jax.ShapeDtypeStruct((nseq, 1, c), F32),
                   jax.ShapeDtypeStruct((nseq, CONV_W - 1, c), F32), jax.ShapeDtypeStruct((nseq, WINDOW, kvw), F32),
                   jax.ShapeDtypeStruct((nseq, WINDOW, kvw), F32)],
        scratch_shapes=[pltpu.VMEM((tm, d), BF16), pltpu.VMEM((tm, d), F32),
                        pltpu.VMEM((tm, w2), F32), pltpu.VMEM((tm, wa), F32),
                        pltpu.VMEM((tm + V7X_SUBLANES, c), F32), pltpu.VMEM((1, c), F32),
                        pltpu.VMEM((tm, c), F32), pltpu.VMEM((tm, c), F32), pltpu.VMEM((tm, c), F32),
                        pltpu.VMEM((tm, c), F32), pltpu.VMEM((CONV_W - 1, c), F32),
                        pltpu.VMEM((2 * aw // V7X_LANES, tm, V7X_LANES), BF16),
                        pltpu.VMEM((N_KV_HEADS, WINDOW + tm, V7X_LANES), BF16),
                        pltpu.VMEM((V7X_LANES, WINDOW + tm), BF16),
                        pltpu.VMEM((2, WINDOW, kvw), F32)],
        compiler_params=pltpu.CompilerParams(dimension_semantics=("arbitrary",),
                                             vmem_limit_bytes=_vmem_limit(weights + tiles + MIXER_TEMP_BYTES)),
        name="mixer_in",
    )(x, cos, sin, *params)


def _ffn_in_kernel(x_ref, g1_ref, wg_ref, wu_ref, wd_ref, gm_ref, winl_ref, wina_ref,
                   x1_ref, pl_ref, pa_ref, xn_ref, acc_ref):
    _ffn_in_tile(x_ref, (g1_ref, wg_ref, wu_ref, wd_ref), gm_ref, winl_ref, wina_ref,
                 x1_ref, pl_ref, pa_ref, xn_ref, acc_ref)


def _ffn_in(x, g1, wg, wu, wd, gm, winl, wina):
    n, d = x.shape
    tm = min(TOKEN_TILE, n)
    wl, wa = winl.shape[1], wina.shape[1]
    row = lambda w: pl.BlockSpec((tm, w), lambda i: (i, 0))
    weights = 2 * (wg.size + wu.size + wd.size + winl.size + wina.size)
    tiles = 2 * 4 * tm * (2 * d + wl + wa) + tm * d * (2 + 4) + 4 * 4 * tm * max(FF_CHUNK, wl)
    return pl.pallas_call(
        _ffn_in_kernel,
        grid=(n // tm,),
        in_specs=[row(d), _resident(g1.shape), _resident(wg.shape), _resident(wu.shape), _resident(wd.shape),
                  _resident(gm.shape), _resident(winl.shape), _resident(wina.shape)],
        out_specs=[row(d), row(wl), row(wa)],
        out_shape=[jax.ShapeDtypeStruct((n, d), F32), jax.ShapeDtypeStruct((n, wl), F32),
                   jax.ShapeDtypeStruct((n, wa), F32)],
        scratch_shapes=[pltpu.VMEM((tm, d), BF16), pltpu.VMEM((tm, d), F32)],
        compiler_params=pltpu.CompilerParams(dimension_semantics=("arbitrary",),
                                             vmem_limit_bytes=_vmem_limit(weights + tiles + FFN_TEMP_BYTES)),
        name="ffn_in",
    )(x, g1, wg, wu, wd, gm, winl, wina)


def _out_ffn_kernel(lo_ref, ao_ref, x1_ref, wol_ref, woa_ref, g2_ref, wg_ref, wu_ref, wd_ref,
                    out_ref, xn_ref, acc_ref):
    y = jnp.dot(lo_ref[...], wol_ref[...], preferred_element_type=F32)
    y = y + jnp.dot(ao_ref[...], woa_ref[...], preferred_element_type=F32)
    x2 = x1_ref[...] + y
    out_ref[...] = x2
    xn_ref[...] = _rms(x2, g2_ref[...]).astype(BF16)
    out_ref[...] = out_ref[...] + 0.5 * _ffn(xn_ref, wg_ref, wu_ref, wd_ref, acc_ref)


def _out_ffn(lo, ao, x1, wol, woa, g2, wg, wu, wd):
    n, d = x1.shape
    tm = min(TOKEN_TILE, n)
    row = lambda w: pl.BlockSpec((tm, w), lambda i: (i, 0))
    weights = 2 * (wg.size + wu.size + wd.size + wol.size + woa.size)
    tiles = 2 * tm * (2 * lo.shape[1] + 2 * ao.shape[1] + 8 * d) + tm * d * (2 + 4) + 4 * 4 * tm * d
    return pl.pallas_call(
        _out_ffn_kernel,
        grid=(n // tm,),
        in_specs=[row(lo.shape[1]), row(ao.shape[1]), row(d), _resident(wol.shape), _resident(woa.shape),
                  _resident(g2.shape), _resident(wg.shape), _resident(wu.shape), _resident(wd.shape)],
        out_specs=row(d),
        out_shape=jax.ShapeDtypeStruct((n, d), F32),
        scratch_shapes=[pltpu.VMEM((tm, d), BF16), pltpu.VMEM((tm, d), F32)],
        compiler_params=pltpu.CompilerParams(dimension_semantics=("arbitrary",),
                                             vmem_limit_bytes=_vmem_limit(weights + tiles + FFN_TEMP_BYTES)),
        name="out_ffn",
    )(lo, ao, x1, wol, woa, g2, wg, wu, wd)


def _qk_prep_kernel(pa_ref, cos_ref, sin_ref, qn_ref, kn_ref, ones_ref, q_ref, k_ref):
    lanes = V7X_LANES
    aw = q_ref.shape[1]
    cos = cos_ref[...]
    sin = sin_ref[...]
    for cc in range(aw // lanes):
        q_ref[:, cc * lanes:(cc + 1) * lanes] = _head_norm_rope(
            pa_ref[:, cc * lanes:(cc + 1) * lanes], qn_ref[...], cos, sin, ones_ref) * (HEAD_DIM ** -0.5)
    k_ref[...] = _head_norm_rope(pa_ref[:, aw:aw + lanes], kn_ref[...], cos, sin, ones_ref)


def _qk_prep(pa, cos, sin, qn, kn, ones, aw):
    n = pa.shape[0]
    return pl.pallas_call(
        _qk_prep_kernel,
        out_shape=[jax.ShapeDtypeStruct((n, aw), F32), jax.ShapeDtypeStruct((n, N_KV_HEADS * HEAD_DIM), F32)],
        compiler_params=pltpu.CompilerParams(vmem_limit_bytes=_vmem_limit(SMALL_KERNEL_VMEM_BYTES)),
        name="qk_prep_sample",
    )(pa, cos, sin, qn, kn, ones)


def _attn_cached_kernel(q_ref, kn_ref, vn_ref, ck_ref, cv_ref, sk_ref, on_ref,
                        out_ref, nk_ref, nv_ref, kk_ref, vv_ref):
    sb, t, aw = q_ref.shape
    w = ck_ref.shape[1]
    lanes = V7X_LANES
    jpad = kk_ref.shape[1]

    kk_ref[:, 0:w, :] = ck_ref[...]
    vv_ref[:, 0:w, :] = cv_ref[...]
    kk_ref[:, w:w + t, :] = kn_ref[...]
    vv_ref[:, w:w + t, :] = vn_ref[...]
    kk_ref[:, w + t:jpad, :] = jnp.zeros((sb, jpad - w - t, lanes), F32)
    vv_ref[:, w + t:jpad, :] = jnp.zeros((sb, jpad - w - t, lanes), F32)
    nk_ref[...] = kk_ref[:, t:t + w, :]
    nv_ref[...] = vv_ref[:, t:t + w, :]

    kk = kk_ref[...]
    vv = vv_ref[...]
    k_sw = pltpu.roll(kk, HEAD_DIM, 2)
    v_sw = pltpu.roll(vv, HEAD_DIM, 2)
    lane_k = lax.broadcasted_iota(jnp.int32, kk.shape, 2) < HEAD_DIM
    lane_q = lax.broadcasted_iota(jnp.int32, (sb, t, lanes), 2) < HEAD_DIM
    k2 = (jnp.where(lane_k, kk, k_sw).astype(BF16), jnp.where(lane_k, k_sw, kk).astype(BF16))
    v2 = (vv.astype(BF16), v_sw.astype(BF16))

    qcols = [q_ref[:, :, cc * lanes:(cc + 1) * lanes] for cc in range(aw // lanes)]
    tq = lax.broadcasted_iota(jnp.int32, (sb, 4 * t, jpad), 1) % t
    j = lax.broadcasted_iota(jnp.int32, (sb, 4 * t, jpad), 2)
    valid = jnp.logical_and(j > tq, j <= tq + w)
    neg_inf = jnp.float32(-jnp.inf)

    ocols = []
    for g in range(N_KV_HEADS):
        lhs = _group_lhs(qcols, g, lane_q, 1)
        s = jnp.einsum('bqd,bjd->bqj', lhs, k2[g], preferred_element_type=F32)
        s = jnp.where(valid, s, neg_inf)
        heads = (4 * g, 4 * g + 2, 4 * g + 1, 4 * g + 3)
        sk = jnp.concatenate([jnp.broadcast_to(sk_ref[h:h + 1, 0:1].reshape(1, 1, 1), (sb, t, 1)) for h in heads],
                             axis=1)
        mx = jnp.maximum(jnp.max(s, axis=-1, keepdims=True), sk)
        p = jnp.exp(s - mx)
        den = jnp.sum(p, axis=-1, keepdims=True) + jnp.exp(sk - mx)
        pb = p.astype(BF16)
        o_even = jnp.einsum('bqj,bjd->bqd', pb[:, 0:2 * t], v2[g], preferred_element_type=F32) / den[:, 0:2 * t]
        o_odd = jnp.einsum('bqj,bjd->bqd', pb[:, 2 * t:4 * t], v2[1 - g], preferred_element_type=F32) / den[:, 2 * t:4 * t]
        for half in range(2):
            ocols.append(jnp.where(lane_q, o_even[:, half * t:(half + 1) * t], o_odd[:, half * t:(half + 1) * t]))
    o = jnp.concatenate(ocols, axis=2)
    out_ref[...] = _rms(o, on_ref[...].reshape(1, 1, aw)).astype(out_ref.dtype)


def _attn_cached(q3, kn3, vn3, ck, cv, sk, on):
    s, t, aw = q3.shape
    w, kvw = ck.shape[1], ck.shape[2]
    sb = min(SAMPLE_SEQ_TILE, s)
    jpad = -(-(w + t) // V7X_SUBLANES) * V7X_SUBLANES
    seq = lambda a, c: pl.BlockSpec((sb, a, c), lambda i: (i, 0, 0))
    return pl.pallas_call(
        _attn_cached_kernel,
        grid=(s // sb,),
        in_specs=[seq(t, aw), seq(t, kvw), seq(t, kvw), seq(w, kvw), seq(w, kvw), _resident(sk.shape), _resident(on.shape)],
        out_specs=[seq(t, aw), seq(w, kvw), seq(w, kvw)],
        out_shape=[jax.ShapeDtypeStruct((s, t, aw), BF16), jax.ShapeDtypeStruct((s, w, kvw), F32),
                   jax.ShapeDtypeStruct((s, w, kvw), F32)],
        scratch_shapes=[pltpu.VMEM((sb, jpad, kvw), F32), pltpu.VMEM((sb, jpad, kvw), F32)],
        compiler_params=pltpu.CompilerParams(dimension_semantics=("arbitrary",),
                                             vmem_limit_bytes=_vmem_limit(SMALL_KERNEL_VMEM_BYTES)),
        name="attn_sample",
    )(q3, kn3, vn3, ck, cv, sk, on)


ROPE_SPLIT = 128


def _rope_tables(start, n):
    half = HEAD_DIM // 2
    lane = jnp.arange(V7X_LANES, dtype=jnp.int32)
    inv_freq = ROPE_THETA ** (-(lane % half).astype(F32) / half)
    sign = jnp.where((lane % HEAD_DIM) < half, -1.0, 1.0).astype(F32)
    if n % ROPE_SPLIT:
        pos = start + jnp.arange(n, dtype=jnp.int32)
        ang = pos.astype(F32)[:, None] * inv_freq[None, :]
        return jnp.cos(ang), jnp.sin(ang) * sign[None, :]
    lo = jnp.arange(ROPE_SPLIT, dtype=jnp.int32)
    hi = start + ROPE_SPLIT * jnp.arange(n // ROPE_SPLIT, dtype=jnp.int32)
    a_hi = hi.astype(F32)[:, None] * inv_freq[None, :]
    a_lo = lo.astype(F32)[:, None] * inv_freq[None, :]
    ch, sh = jnp.cos(a_hi)[:, None, :], jnp.sin(a_hi)[:, None, :]
    cl, sl = jnp.cos(a_lo)[None, :, :], jnp.sin(a_lo)[None, :, :]
    cos = (ch * cl - sh * sl).reshape(n, V7X_LANES)
    sin = (sh * cl + ch * sl).reshape(n, V7X_LANES)
    return cos, sin * sign[None, :]


def _block_diag(w):
    n, c, d = w.shape
    eye = jnp.eye(n, dtype=w.dtype)
    return (w[:, :, None, :] * eye[:, None, :, None]).reshape(n * c, n * d)


def _row(v):
    return v.reshape(1, -1).astype(F32)


def kernel(x_prompt, x_sample, state_lru_h, state_conv, cache_k, cache_v, ffn1_norm, ffn1_w_gate, ffn1_w_up, ffn1_w_down, mix_norm, w_in, conv_w, conv_b, lru_wa, lru_ba, lru_wx, lru_bx, lru_lambda, q_norm, k_norm, attn_sinks, lru_out_norm, attn_out_norm, w_out, ffn2_norm, ffn2_w_gate, ffn2_w_up, ffn2_w_down):
    depth = w_in.shape[0]
    b, t, d = x_prompt.shape
    s, ts, _ = x_sample.shape
    c = lru_lambda.shape[1]
    aw = attn_out_norm.shape[1]
    kvw = N_KV_HEADS * HEAD_DIM
    win = cache_k.shape[2]
    assert t >= WINDOW and win == WINDOW

    cos_p, sin_p = _rope_tables(0, t)
    cos_s, sin_s = _rope_tables(PAST_LEN, ts)
    cos_s, sin_s = jnp.tile(cos_s, (s, 1)), jnp.tile(sin_s, (s, 1))
    ones = _block_diag(jnp.full((V7X_LANES // HEAD_DIM, HEAD_DIM, HEAD_DIM), 1.0 / HEAD_DIM, BF16))
    log2e = 1.0 / jnp.log(jnp.float32(2.0))

    yp = x_prompt.reshape(b * t, d)
    ys = x_sample.reshape(s * ts, d)
    outs = [[] for _ in range(8)]
    for l in range(depth):
        ffn1 = (_row(ffn1_norm[l]), (0.5 * ffn1_w_gate[l]).astype(BF16), ffn1_w_up[l].astype(BF16),
                ffn1_w_down[l].astype(BF16))
        ffn2 = (_row(ffn2_norm[l]), (0.5 * ffn2_w_gate[l]).astype(BF16), ffn2_w_up[l].astype(BF16),
                ffn2_w_down[l].astype(BF16))
        win_b = w_in[l].astype(BF16)
        ffn_in_p = ffn1 + (_row(mix_norm[l]), win_b[:, :2 * c], win_b[:, 2 * c:])
        wout_b = w_out[l].astype(BF16)
        wol, woa = wout_b[:c], wout_b[c:]
        lru_p = (conv_w[l], _row(conv_b[l]), (0.5 * _block_diag(lru_wa[l])).astype(BF16), _row(0.5 * lru_ba[l]),
                 (0.5 * _block_diag(lru_wx[l])).astype(BF16), _row(0.5 * lru_bx[l]), _row(lru_lambda[l]),
                 _row(lru_out_norm[l]))
        reps = V7X_LANES // HEAD_DIM
        qn = _row(jnp.tile(q_norm[l], reps))
        kn = _row(jnp.tile(k_norm[l], reps))
        sk = jnp.broadcast_to(attn_sinks[l].astype(F32)[:, None], (attn_sinks.shape[1], V7X_LANES))
        qn_p = qn * (HEAD_DIM ** -0.5 * log2e)
        sk_rows = jnp.repeat(attn_sinks[l].astype(F32).reshape(N_KV_HEADS, -1), WINDOW, axis=1) * log2e
        a_on = _row(attn_out_norm[l])

        x1, lo, ao, hl, cs, pk, pv = _mixer_in(yp, t, cos_p, sin_p, ffn_in_p, lru_p, (qn_p, kn, sk_rows, ones, a_on))
        yp = _out_ffn(lo, ao, x1, wol, woa, *ffn2)
        outs[0].append(hl.reshape(b, c))
        outs[1].append(cs)
        outs[2].append(pk.reshape(b, WINDOW, N_KV_HEADS, HEAD_DIM))
        outs[3].append(pv.reshape(b, WINDOW, N_KV_HEADS, HEAD_DIM))

        x1s, pl_s, pa_s = _ffn_in(ys, *ffn_in_p)
        x_tm = pl_s.reshape(s, ts, 2 * c).transpose(1, 0, 2)
        cst_tm = state_conv[l].transpose(1, 0, 2)
        lo_tm, hls, cs_tm = _lru_sample(x_tm, cst_tm, state_lru_h[l], *lru_p)
        q_s, k_s = _qk_prep(pa_s, cos_s, sin_s, qn, kn, ones, aw)
        v_s = pa_s[:, aw + kvw:]
        aos, nk, nv = _attn_cached(q_s.reshape(s, ts, aw), k_s.reshape(s, ts, kvw), v_s.reshape(s, ts, kvw),
                                   cache_k[l].reshape(s, win, kvw), cache_v[l].reshape(s, win, kvw), sk, a_on)
        ys = _out_ffn(lo_tm.transpose(1, 0, 2).reshape(s * ts, c), aos.reshape(s * ts, aw), x1s, wol, woa, *ffn2)
        outs[4].append(hls)
        outs[5].append(cs_tm.transpose(1, 0, 2))
        outs[6].append(nk.reshape(s, win, N_KV_HEADS, HEAD_DIM))
        outs[7].append(nv.reshape(s, win, N_KV_HEADS, HEAD_DIM))

    st = [jnp.stack(o) for o in outs]
    return (yp.reshape(b, t, d), ys.reshape(s, ts, d), st[0], st[1], st[2], st[3], st[4], st[5], st[6], st[7])
```

```python
import functools

import jax
import jax.numpy as jnp
from jax import lax
from jax.experimental import pallas as pl
from jax.experimental.pallas import tpu as pltpu

F32 = jnp.float32
BF16 = jnp.bfloat16

NORM_EPS = 1e-6
LRU_C = 8.0
CONV_W = 4
HEAD_DIM = 64
N_KV_HEADS = 2
WINDOW = 128
ROPE_THETA = 10000.0
PAST_LEN = 16384

V7X_LANES = 128
V7X_SUBLANES = 8
V7X_MXU_DIM = 256
V7X_VMEM_BYTES = 64 * 1024 * 1024

TOKEN_TILE = 512
SAMPLE_SEQ_TILE = 16
FF_CHUNK = V7X_MXU_DIM
GATE_DIAG_TILE = V7X_MXU_DIM


MIB = 1024 * 1024
VMEM_KEEP_FREE_BYTES = 4 * MIB
FFN_TEMP_BYTES = 8 * MIB
MIXER_TEMP_BYTES = 20 * MIB
SMALL_KERNEL_VMEM_BYTES = 40 * MIB


def _vmem_limit(nbytes):
    return int(min(nbytes, V7X_VMEM_BYTES - VMEM_KEEP_FREE_BYTES))


def _resident(shape):
    nd = len(shape)
    return pl.BlockSpec(shape, lambda *_: (0,) * nd, pipeline_mode=pl.Buffered(1))


def _rms(x, g):
    ms = jnp.mean(x * x, axis=-1, keepdims=True)
    return (x * lax.rsqrt(ms + NORM_EPS)) * g


def _ffn(xn_ref, wg_ref, wu_ref, wd_ref, acc_ref):
    acc_ref[...] = jnp.zeros_like(acc_ref)
    for c0 in range(0, wg_ref.shape[1], FF_CHUNK):
        xn = xn_ref[...]
        h = jnp.dot(xn, wg_ref[:, c0:c0 + FF_CHUNK], preferred_element_type=F32)
        u = jnp.dot(xn, wu_ref[:, c0:c0 + FF_CHUNK], preferred_element_type=F32)
        a = ((h + h * jnp.tanh(h)) * u).astype(BF16)
        acc_ref[...] += jnp.dot(a, wd_ref[c0:c0 + FF_CHUNK, :], preferred_element_type=F32)
    return acc_ref[...]


def _ffn_in_tile(x_ref, ffn_refs, gm_ref, winl_ref, wina_ref, x1_ref, pl_ref, pa_ref, xn_ref, acc_ref):
    g1_ref, wg_ref, wu_ref, wd_ref = ffn_refs
    xn_ref[...] = _rms(x_ref[...], g1_ref[...]).astype(BF16)
    x1 = x_ref[...] + 0.5 * _ffn(xn_ref, wg_ref, wu_ref, wd_ref, acc_ref)
    x1_ref[...] = x1
    xn_ref[...] = _rms(x1, gm_ref[...]).astype(BF16)
    pl_ref[...] = jnp.dot(xn_ref[...], winl_ref[...], preferred_element_type=F32)
    pa_ref[...] = jnp.dot(xn_ref[...], wina_ref[...], preferred_element_type=F32)


def _lru_preact(xc, gate_w):
    wa_ref, ba_ref, wx_ref, bx_ref = gate_w
    xcb = xc.astype(BF16)
    c = xc.shape[-1]
    dt = min(GATE_DIAG_TILE, c)

    def blockdiag_dot(w_ref):
        return jnp.concatenate([jnp.dot(xcb[:, s:s + dt], w_ref[s:s + dt, s:s + dt], preferred_element_type=F32)
                                for s in range(0, c, dt)], axis=1)

    zr = blockdiag_dot(wa_ref) + ba_ref[...]
    zi = blockdiag_dot(wx_ref) + bx_ref[...]
    return zr, zi


def _lru_terms(xc, zr, zi, lam_ref):
    i = 0.5 * jnp.tanh(zi) + 0.5
    half_c = (-0.5 * LRU_C) * jax.nn.softplus(-lam_ref[...])
    log_a = half_c * jnp.tanh(zr) + half_c
    a = jnp.exp(log_a)
    w = -jnp.tanh(log_a) * (a * a + 1.0)
    root = jnp.where(w > 0.0, w * lax.rsqrt(w), 0.0)
    u = root * (i * xc)
    return a, u


def _lru_prep(p_ref, first, lru_refs, xh_ref, xc_ref, zr_ref, zi_ref, gate_ref, ctail_ref):
    cw_ref, cb_ref, wa_ref, ba_ref, wx_ref, bx_ref, _, _ = lru_refs
    tt = p_ref.shape[0]
    c = xc_ref.shape[1]
    hist = V7X_SUBLANES

    xh_ref[0:hist, :] = jnp.where(first, jnp.zeros((hist, c), F32), xh_ref[0:hist, :])
    x = p_ref[:, 0:c]
    xh_ref[hist:hist + tt, :] = x
    xg = xh_ref[...].reshape(tt // hist + 1, hist, c)
    sub = lax.broadcasted_iota(jnp.int32, (tt // hist, hist, c), 1)
    w = cw_ref[...]
    xc = cb_ref[...].reshape(1, 1, c)
    for j in range(CONV_W - 1):
        k = CONV_W - 1 - j
        rolled = pltpu.roll(xg, k, 1)
        xc = xc + jnp.where(sub >= k, rolled[1:], rolled[:-1]) * w[j:j + 1, :].reshape(1, 1, c)
    xc = (xc + xg[1:] * w[CONV_W - 1:CONV_W, :].reshape(1, 1, c)).reshape(tt, c)
    xh_ref[0:hist, :] = x[tt - hist:tt, :]
    ctail_ref[...] = x[tt - (CONV_W - 1):tt, :]

    xc_ref[...] = xc
    zr, zi = _lru_preact(xc, (wa_ref, ba_ref, wx_ref, bx_ref))
    zr_ref[...] = zr
    zi_ref[...] = zi
    gate_ref[...] = p_ref[:, c:2 * c]


def _lru_core(first, lru_refs, xc_ref, zr_ref, zi_ref, gate_ref, ctail_ref, out_ref, hl_ref, cs_ref, h_ref):
    lam_ref, on_ref = lru_refs[6], lru_refs[7]
    tt, c = xc_ref.shape

    a, u = _lru_terms(xc_ref[...], zr_ref[...], zi_ref[...], lam_ref)

    groups = tt // V7X_SUBLANES
    a3 = a.reshape(groups, V7X_SUBLANES, c)
    u3 = u.reshape(groups, V7X_SUBLANES, c)
    sub = lax.broadcasted_iota(jnp.int32, a3.shape, 1)
    k = 1
    while k < V7X_SUBLANES:
        m = sub >= k
        a_prev = pltpu.roll(a3, k, 1)
        u_prev = pltpu.roll(u3, k, 1)
        u3 = jnp.where(m, a3 * u_prev + u3, u3)
        a3 = jnp.where(m, a3 * a_prev, a3)
        k *= 2

    h = jnp.where(first, jnp.zeros((1, c), F32), h_ref[...])
    hs = []
    for g in range(groups):
        hr = a3[g] * h + u3[g]
        hs.append(hr)
        h = hr[V7X_SUBLANES - 1:V7X_SUBLANES, :]
    h_ref[...] = h

    y = jnp.concatenate(hs, axis=0) * jax.nn.gelu(gate_ref[...])
    out_ref[...] = _rms(y, on_ref[...]).astype(out_ref.dtype)
    hl_ref[0] = h
    cs_ref[0] = ctail_ref[...]


def _lru_sample_kernel(x_ref, cst_ref, h0_ref, cw_ref, cb_ref, wa_ref, ba_ref, wx_ref, bx_ref, lam_ref, on_ref,
                       out_ref, hl_ref, cs_ref):
    t = x_ref.shape[0]
    c = out_ref.shape[2]
    w = cw_ref[...]
    rows = [cst_ref[j] for j in range(CONV_W - 1)] + [x_ref[j, :, 0:c] for j in range(t)]
    h = h0_ref[...]
    for s in range(t):
        xc = cb_ref[...]
        for j in range(CONV_W):
            xc = xc + rows[s + j] * w[j:j + 1, :]
        zr, zi = _lru_preact(xc, (wa_ref, ba_ref, wx_ref, bx_ref))
        a, u = _lru_terms(xc, zr, zi, lam_ref)
        h = a * h + u
        y = h * jax.nn.gelu(x_ref[s, :, c:2 * c])
        out_ref[s] = _rms(y, on_ref[...]).astype(out_ref.dtype)
    hl_ref[...] = h
    for j in range(CONV_W - 1):
        cs_ref[j] = rows[t + j]


def _lru_sample(x_tm, cst_tm, h0, cw, cb, wa, ba, wx, bx, lam, on):
    t, s, w2 = x_tm.shape
    c = w2 // 2
    return pl.pallas_call(
        _lru_sample_kernel,
        out_shape=[jax.ShapeDtypeStruct((t, s, c), BF16), jax.ShapeDtypeStruct((s, c), F32),
                   jax.ShapeDtypeStruct((CONV_W - 1, s, c), F32)],
        compiler_params=pltpu.CompilerParams(vmem_limit_bytes=_vmem_limit(SMALL_KERNEL_VMEM_BYTES)),
        name="lru_sample",
    )(x_tm, cst_tm, h0, cw, cb, wa, ba, wx, bx, lam, on)


def _head_norm_rope(xc, gain, cos, sin, ones_ref):
    x2 = xc * xc
    hi = x2.astype(BF16)
    lo = (x2 - hi.astype(F32)).astype(BF16)
    ss = jnp.dot(hi, ones_ref[...], preferred_element_type=F32) + jnp.dot(lo, ones_ref[...], preferred_element_type=F32)
    xn = (xc * lax.rsqrt(ss + NORM_EPS)) * gain
    lane = lax.broadcasted_iota(jnp.int32, xn.shape, xn.ndim - 1)
    first_half = (lane & (HEAD_DIM // 2)) == 0
    ax = xn.ndim - 1
    swapped = jnp.where(first_half, pltpu.roll(xn, V7X_LANES - HEAD_DIM // 2, ax), pltpu.roll(xn, HEAD_DIM // 2, ax))
    return xn * cos + swapped * sin


def _group_lhs(qcols, g, lane_lo, axis):
    zero = jnp.zeros_like(qcols[0])
    parts = [jnp.where(lane_lo, qcols[2 * g], zero), jnp.where(lane_lo, qcols[2 * g + 1], zero),
             jnp.where(lane_lo, zero, qcols[2 * g]), jnp.where(lane_lo, zero, qcols[2 * g + 1])]
    return jnp.concatenate(parts, axis=axis).astype(BF16)


def _attn_prep(pa_ref, cos_ref, sin_ref, attn_refs, qh_ref, k2_ref, vt_ref, kvl_ref):
    qn_ref, kn_ref, _, ones_ref, _ = attn_refs
    tq = pa_ref.shape[0]
    blk = WINDOW
    lanes = V7X_LANES
    ncol = qh_ref.shape[0] // 2
    aw = ncol * lanes

    cos = cos_ref[...]
    sin = sin_ref[...]
    cols = [pa_ref[:, cc * lanes:(cc + 1) * lanes] for cc in range(ncol + 1)]
    sq = [x * x for x in cols]
    hi = [x.astype(BF16) for x in sq]
    lo = [(x - h.astype(F32)).astype(BF16) for x, h in zip(sq, hi)]
    ss = (jnp.dot(jnp.concatenate(hi, axis=0), ones_ref[...], preferred_element_type=F32)
          + jnp.dot(jnp.concatenate(lo, axis=0), ones_ref[...], preferred_element_type=F32))
    lane = lax.broadcasted_iota(jnp.int32, (tq, lanes), 1)
    first_half = (lane & (HEAD_DIM // 2)) == 0
    lane_lo = lane < HEAD_DIM
    rot = []
    for cc, x in enumerate(cols):
        gain = qn_ref[...] if cc < ncol else kn_ref[...]
        xn = (x * lax.rsqrt(ss[cc * tq:(cc + 1) * tq] + NORM_EPS)) * gain
        swapped = jnp.where(first_half, pltpu.roll(xn, lanes - HEAD_DIM // 2, 1), pltpu.roll(xn, HEAD_DIM // 2, 1))
        rot.append(xn * cos + swapped * sin)
    k = rot[ncol]
    v = pa_ref[:, aw + lanes:aw + 2 * lanes]

    zero = jnp.zeros((tq, lanes), F32)
    for cc in range(ncol):
        qh_ref[2 * cc] = jnp.where(lane_lo, rot[cc], zero).astype(BF16)
        qh_ref[2 * cc + 1] = jnp.where(lane_lo, zero, rot[cc]).astype(BF16)

    k_sw = pltpu.roll(k, HEAD_DIM, 1)
    k2_ref[0, blk:blk + tq, :] = jnp.where(lane_lo, k, k_sw).astype(BF16)
    k2_ref[1, blk:blk + tq, :] = jnp.where(lane_lo, k_sw, k).astype(BF16)
    for jb in range(tq // blk):
        vt_ref[:, (jb + 1) * blk:(jb + 2) * blk] = v[jb * blk:(jb + 1) * blk].T.astype(BF16)
    kvl_ref[0] = k[tq - blk:tq]
    kvl_ref[1] = v[tq - blk:tq]


def _attn_block(jb, first, attn_refs, out_ref, pk_ref, pv_ref, qh_ref, k2_ref, vt_ref, kvl_ref):
    _, _, sk_ref, _, on_ref = attn_refs
    tq = qh_ref.shape[1]
    blk = WINDOW
    nblk = tq // blk
    heads_per_group = qh_ref.shape[0] // N_KV_HEADS
    r0 = jb * blk

    ncolq = heads_per_group * blk
    kj = lax.broadcasted_iota(jnp.int32, (blk, ncolq), 0)
    qi = lax.broadcasted_iota(jnp.int32, (blk, ncolq), 1) % blk
    tri = kj <= qi

    ocols = []
    for g in range(N_KV_HEADS):
        lhs = jnp.concatenate([qh_ref[h, r0:r0 + blk, :]
                               for h in range(g * heads_per_group, (g + 1) * heads_per_group)], axis=0)
        st = lax.dot_general(k2_ref[g, r0:r0 + 2 * blk, :], lhs, (((1,), (1,)), ((), ())),
                             preferred_element_type=F32)
        s_prev = st[0:blk]
        if jb == 0:
            s_prev = s_prev + jnp.where(first, jnp.float32(-jnp.inf), jnp.float32(0.0))
        m = jnp.where(tri, st[blk:2 * blk], s_prev)
        sk = sk_ref[g:g + 1, :]
        mx = jnp.maximum(jnp.max(m, axis=0, keepdims=True), sk)
        p = jnp.exp2(m - mx)
        den = jnp.sum(p, axis=0, keepdims=True) + jnp.exp2(sk - mx)
        pz = jnp.zeros_like(p)
        pt = jnp.concatenate([jnp.where(tri, pz, p), jnp.where(tri, p, pz)], axis=0).astype(BF16)
        vt_g = vt_ref[g * HEAD_DIM:(g + 1) * HEAD_DIM, r0:r0 + 2 * blk]
        og = jnp.dot(vt_g, pt, preferred_element_type=F32) * (1.0 / den)
        for half in range(heads_per_group // 2):
            pair = jnp.concatenate([og[:, (2 * half) * blk:(2 * half + 1) * blk],
                                    og[:, (2 * half + 1) * blk:(2 * half + 2) * blk]], axis=0)
            ocols.append(pair.T)
    o = jnp.concatenate(ocols, axis=1)
    out_ref[r0:r0 + blk, :] = _rms(o, on_ref[...]).astype(out_ref.dtype)

    if jb == nblk - 1:
        k2_ref[:, 0:blk, :] = k2_ref[:, tq:tq + blk, :]
        vt_ref[:, 0:blk] = vt_ref[:, tq:tq + blk]
        pk_ref[0] = kvl_ref[0]
        pv_ref[0] = kvl_ref[1]


N_FFN_IN = 7
N_LRU = 8
N_ATTN = 5


def _mixer_in_kernel(*refs, tiles_per_seq, ntile):
    x_ref, cos_ref, sin_ref = refs[0:3]
    p0 = 3
    g1_ref, wg_ref, wu_ref, wd_ref, gm_ref, winl_ref, wina_ref = refs[p0:p0 + N_FFN_IN]
    lru_refs = refs[p0 + N_FFN_IN:p0 + N_FFN_IN + N_LRU]
    attn_refs = refs[p0 + N_FFN_IN + N_LRU:p0 + N_FFN_IN + N_LRU + N_ATTN]
    o0 = p0 + N_FFN_IN + N_LRU + N_ATTN
    x1_ref, lo_ref, ao_ref, hl_ref, cs_ref, pk_ref, pv_ref = refs[o0:o0 + 7]
    (xn_ref, acc_ref, pl_ref, pa_ref, xh_ref, h_ref, xc_ref, zr_ref, zi_ref, gate_ref, ctail_ref,
     qh_ref, k2_ref, vt_ref, kvl_ref) = refs[o0 + 7:]

    i = pl.program_id(0)

    @pl.when(i == 0)
    def _():
        for ref in (pl_ref, pa_ref, xh_ref, h_ref, xc_ref, zr_ref, zi_ref, gate_ref, ctail_ref,
                    qh_ref, k2_ref, vt_ref, kvl_ref):
            ref[...] = jnp.zeros_like(ref)

    def step(with_ffn):
        first = lax.rem(jnp.maximum(i - 1, 0), tiles_per_seq) == 0
        _lru_prep(pl_ref, first, lru_refs, xh_ref, xc_ref, zr_ref, zi_ref, gate_ref, ctail_ref)
        _attn_prep(pa_ref, cos_ref, sin_ref, attn_refs, qh_ref, k2_ref, vt_ref, kvl_ref)
        _lru_core(first, lru_refs, xc_ref, zr_ref, zi_ref, gate_ref, ctail_ref, lo_ref, hl_ref, cs_ref, h_ref)
        for jb in range(pa_ref.shape[0] // WINDOW):
            _attn_block(jb, first, attn_refs, ao_ref, pk_ref, pv_ref, qh_ref, k2_ref, vt_ref, kvl_ref)
        if with_ffn:
            _ffn_in_tile(x_ref, (g1_ref, wg_ref, wu_ref, wd_ref), gm_ref, winl_ref, wina_ref,
                         x1_ref, pl_ref, pa_ref, xn_ref, acc_ref)

    pl.when(i < ntile)(functools.partial(step, True))
    pl.when(i >= ntile)(functools.partial(step, False))


def _mixer_in(x, seq_len, cos, sin, ffn_in_p, lru_p, attn_p):
    n, d = x.shape
    tm = TOKEN_TILE
    assert seq_len % tm == 0 and n % seq_len == 0 and tm % WINDOW == 0
    nseq = n // seq_len
    tps = seq_len // tm
    ntile = n // tm
    wg, winl, wina = ffn_in_p[1], ffn_in_p[5], ffn_in_p[6]
    w2, wa = winl.shape[1], wina.shape[1]
    c = w2 // 2
    aw = attn_p[4].shape[1]
    kvw = N_KV_HEADS * HEAD_DIM
    cur = lambda i: jnp.minimum(i, ntile - 1)
    mix = lambda i: jnp.maximum(i - 1, 0)
    params = list(ffn_in_p) + list(lru_p) + list(attn_p)
    weights = sum(p.size * p.dtype.itemsize for p in params)
    tiles = 2 * 4 * tm * 2 * d + 2 * 2 * tm * (c + aw) + 4 * tm * (w2 + wa) + tm * d * (2 + 4) + 6 * 4 * tm * c
    return pl.pallas_call(
        functools.partial(_mixer_in_kernel, tiles_per_seq=tps, ntile=ntile),
        grid=(ntile + 1,),
        in_specs=[pl.BlockSpec((tm, d), lambda i: (cur(i), 0)),
                  pl.BlockSpec((tm, V7X_LANES), lambda i: (lax.rem(mix(i), tps), 0)),
                  pl.BlockSpec((tm, V7X_LANES), lambda i: (lax.rem(mix(i), tps), 0))]
                 + [_resident(p.shape) for p in params],
        out_specs=[pl.BlockSpec((tm, d), lambda i: (cur(i), 0)),
                   pl.BlockSpec((tm, c), lambda i: (mix(i), 0)),
                   pl.BlockSpec((tm, aw), lambda i: (mix(i), 0)),
                   pl.BlockSpec((1, 1, c), lambda i: (mix(i) // tps, 0, 0)),
                   pl.BlockSpec((1, CONV_W - 1, c), lambda i: (mix(i) // tps, 0, 0)),
                   pl.BlockSpec((1, WINDOW, kvw), lambda i: (mix(i) // tps, 0, 0)),
                   pl.BlockSpec((1, WINDOW, kvw), lambda i: (mix(i) // tps, 0, 0))],
        out_shape=[jax.ShapeDtypeStruct((n, d), F32), jax.ShapeDtypeStruct((n, c), BF16),
                   jax.ShapeDtypeStruct((n, aw), BF16), jax.ShapeDtypeStruct((nseq, 1, c), F32),
                   jax.ShapeDtypeStruct((nseq, CONV_W - 1, c), F32), jax.ShapeDtypeStruct((nseq, WINDOW, kvw), F32),
                   jax.ShapeDtypeStruct((nseq, WINDOW, kvw), F32)],
        scratch_shapes=[pltpu.VMEM((tm, d), BF16), pltpu.VMEM((tm, d), F32),
                        pltpu.VMEM((tm, w2), F32), pltpu.VMEM((tm, wa), F32),
                        pltpu.VMEM((tm + V7X_SUBLANES, c), F32), pltpu.VMEM((1, c), F32),
                        pltpu.VMEM((tm, c), F32), pltpu.VMEM((tm, c), F32), pltpu.VMEM((tm, c), F32),
                        pltpu.VMEM((tm, c), F32), pltpu.VMEM((CONV_W - 1, c), F32),
                        pltpu.VMEM((2 * aw // V7X_LANES, tm, V7X_LANES), BF16),
                        pltpu.VMEM((N_KV_HEADS, WINDOW + tm, V7X_LANES), BF16),
                        pltpu.VMEM((V7X_LANES, WINDOW + tm), BF16),
                        pltpu.VMEM((2, WINDOW, kvw), F32)],
        compiler_params=pltpu.CompilerParams(dimension_semantics=("arbitrary",),
                                             vmem_limit_bytes=_vmem_limit(weights + tiles + MIXER_TEMP_BYTES)),
        name="mixer_in",
    )(x, cos, sin, *params)


def _ffn_in_kernel(x_ref, g1_ref, wg_ref, wu_ref, wd_ref, gm_ref, winl_ref, wina_ref,
                   x1_ref, pl_ref, pa_ref, xn_ref, acc_ref):
    _ffn_in_tile(x_ref, (g1_ref, wg_ref, wu_ref, wd_ref), gm_ref, winl_ref, wina_ref,
                 x1_ref, pl_ref, pa_ref, xn_ref, acc_ref)


def _ffn_in(x, g1, wg, wu, wd, gm, winl, wina):
    n, d = x.shape
    tm = min(TOKEN_TILE, n)
    wl, wa = winl.shape[1], wina.shape[1]
    row = lambda w: pl.BlockSpec((tm, w), lambda i: (i, 0))
    weights = 2 * (wg.size + wu.size + wd.size + winl.size + wina.size)
    tiles = 2 * 4 * tm * (2 * d + wl + wa) + tm * d * (2 + 4) + 4 * 4 * tm * max(FF_CHUNK, wl)
    return pl.pallas_call(
        _ffn_in_kernel,
        grid=(n // tm,),
        in_specs=[row(d), _resident(g1.shape), _resident(wg.shape), _resident(wu.shape), _resident(wd.shape),
                  _resident(gm.shape), _resident(winl.shape), _resident(wina.shape)],
        out_specs=[row(d), row(wl), row(wa)],
        out_shape=[jax.ShapeDtypeStruct((n, d), F32), jax.ShapeDtypeStruct((n, wl), F32),
                   jax.ShapeDtypeStruct((n, wa), F32)],
        scratch_shapes=[pltpu.VMEM((tm, d), BF16), pltpu.VMEM((tm, d), F32)],
        compiler_params=pltpu.CompilerParams(dimension_semantics=("arbitrary",),
                                             vmem_limit_bytes=_vmem_limit(weights + tiles + FFN_TEMP_BYTES)),
        name="ffn_in",
    )(x, g1, wg, wu, wd, gm, winl, wina)


def _out_ffn_kernel(lo_ref, ao_ref, x1_ref, wol_ref, woa_ref, g2_ref, wg_ref, wu_ref, wd_ref,
                    out_ref, xn_ref, acc_ref):
    y = jnp.dot(lo_ref[...], wol_ref[...], preferred_element_type=F32)
    y = y + jnp.dot(ao_ref[...], woa_ref[...], preferred_element_type=F32)
    x2 = x1_ref[...] + y
    out_ref[...] = x2
    xn_ref[...] = _rms(x2, g2_ref[...]).astype(BF16)
    out_ref[...] = out_ref[...] + 0.5 * _ffn(xn_ref, wg_ref, wu_ref, wd_ref, acc_ref)


def _out_ffn(lo, ao, x1, wol, woa, g2, wg, wu, wd):
    n, d = x1.shape
    tm = min(TOKEN_TILE, n)
    row = lambda w: pl.BlockSpec((tm, w), lambda i: (i, 0))
    weights = 2 * (wg.size + wu.size + wd.size + wol.size + woa.size)
    tiles = 2 * tm * (2 * lo.shape[1] + 2 * ao.shape[1] + 8 * d) + tm * d * (2 + 4) + 4 * 4 * tm * d
    return pl.pallas_call(
        _out_ffn_kernel,
        grid=(n // tm,),
        in_specs=[row(lo.shape[1]), row(ao.shape[1]), row(d), _resident(wol.shape), _resident(woa.shape),
                  _resident(g2.shape), _resident(wg.shape), _resident(wu.shape), _resident(wd.shape)],
        out_specs=row(d),
        out_shape=jax.ShapeDtypeStruct((n, d), F32),
        scratch_shapes=[pltpu.VMEM((tm, d), BF16), pltpu.VMEM((tm, d), F32)],
        compiler_params=pltpu.CompilerParams(dimension_semantics=("arbitrary",),
                                             vmem_limit_bytes=_vmem_limit(weights + tiles + FFN_TEMP_BYTES)),
        name="out_ffn",
    )(lo, ao, x1, wol, woa, g2, wg, wu, wd)


def _qk_prep_kernel(pa_ref, cos_ref, sin_ref, qn_ref, kn_ref, ones_ref, q_ref, k_ref):
    lanes = V7X_LANES
    aw = q_ref.shape[1]
    cos = cos_ref[...]
    sin = sin_ref[...]
    for cc in range(aw // lanes):
        q_ref[:, cc * lanes:(cc + 1) * lanes] = _head_norm_rope(
            pa_ref[:, cc * lanes:(cc + 1) * lanes], qn_ref[...], cos, sin, ones_ref) * (HEAD_DIM ** -0.5)
    k_ref[...] = _head_norm_rope(pa_ref[:, aw:aw + lanes], kn_ref[...], cos, sin, ones_ref)


def _qk_prep(pa, cos, sin, qn, kn, ones, aw):
    n = pa.shape[0]
    return pl.pallas_call(
        _qk_prep_kernel,
        out_shape=[jax.ShapeDtypeStruct((n, aw), F32), jax.ShapeDtypeStruct((n, N_KV_HEADS * HEAD_DIM), F32)],
        compiler_params=pltpu.CompilerParams(vmem_limit_bytes=_vmem_limit(SMALL_KERNEL_VMEM_BYTES)),
        name="qk_prep_sample",
    )(pa, cos, sin, qn, kn, ones)


def _attn_cached_kernel(q_ref, kn_ref, vn_ref, ck_ref, cv_ref, sk_ref, on_ref,
                        out_ref, nk_ref, nv_ref, kk_ref, vv_ref):
    sb, t, aw = q_ref.shape
    w = ck_ref.shape[1]
    lanes = V7X_LANES
    jpad = kk_ref.shape[1]

    kk_ref[:, 0:w, :] = ck_ref[...]
    vv_ref[:, 0:w, :] = cv_ref[...]
    kk_ref[:, w:w + t, :] = kn_ref[...]
    vv_ref[:, w:w + t, :] = vn_ref[...]
    kk_ref[:, w + t:jpad, :] = jnp.zeros((sb, jpad - w - t, lanes), F32)
    vv_ref[:, w + t:jpad, :] = jnp.zeros((sb, jpad - w - t, lanes), F32)
    nk_ref[...] = kk_ref[:, t:t + w, :]
    nv_ref[...] = vv_ref[:, t:t + w, :]

    kk = kk_ref[...]
    vv = vv_ref[...]
    k_sw = pltpu.roll(kk, HEAD_DIM, 2)
    v_sw = pltpu.roll(vv, HEAD_DIM, 2)
    lane_k = lax.broadcasted_iota(jnp.int32, kk.shape, 2) < HEAD_DIM
    lane_q = lax.broadcasted_iota(jnp.int32, (sb, t, lanes), 2) < HEAD_DIM
    k2 = (jnp.where(lane_k, kk, k_sw).astype(BF16), jnp.where(lane_k, k_sw, kk).astype(BF16))
    v2 = (vv.astype(BF16), v_sw.astype(BF16))

    qcols = [q_ref[:, :, cc * lanes:(cc + 1) * lanes] for cc in range(aw // lanes)]
    tq = lax.broadcasted_iota(jnp.int32, (sb, 4 * t, jpad), 1) % t
    j = lax.broadcasted_iota(jnp.int32, (sb, 4 * t, jpad), 2)
    valid = jnp.logical_and(j > tq, j <= tq + w)
    neg_inf = jnp.float32(-jnp.inf)

    ocols = []
    for g in range(N_KV_HEADS):
        lhs = _group_lhs(qcols, g, lane_q, 1)
        s = jnp.einsum('bqd,bjd->bqj', lhs, k2[g], preferred_element_type=F32)
        s = jnp.where(valid, s, neg_inf)
        heads = (4 * g, 4 * g + 2, 4 * g + 1, 4 * g + 3)
        sk = jnp.concatenate([jnp.broadcast_to(sk_ref[h:h + 1, 0:1].reshape(1, 1, 1), (sb, t, 1)) for h in heads],
                             axis=1)
        mx = jnp.maximum(jnp.max(s, axis=-1, keepdims=True), sk)
        p = jnp.exp(s - mx)
        den = jnp.sum(p, axis=-1, keepdims=True) + jnp.exp(sk - mx)
        pb = p.astype(BF16)
        o_even = jnp.einsum('bqj,bjd->bqd', pb[:, 0:2 * t], v2[g], preferred_element_type=F32) / den[:, 0:2 * t]
        o_odd = jnp.einsum('bqj,bjd->bqd', pb[:, 2 * t:4 * t], v2[1 - g], preferred_element_type=F32) / den[:, 2 * t:4 * t]
        for half in range(2):
            ocols.append(jnp.where(lane_q, o_even[:, half * t:(half + 1) * t], o_odd[:, half * t:(half + 1) * t]))
    o = jnp.concatenate(ocols, axis=2)
    out_ref[...] = _rms(o, on_ref[...].reshape(1, 1, aw)).astype(out_ref.dtype)


def _attn_cached(q3, kn3, vn3, ck, cv, sk, on):
    s, t, aw = q3.shape
    w, kvw = ck.shape[1], ck.shape[2]
    sb = min(SAMPLE_SEQ_TILE, s)
    jpad = -(-(w + t) // V7X_SUBLANES) * V7X_SUBLANES
    seq = lambda a, c: pl.BlockSpec((sb, a, c), lambda i: (i, 0, 0))
    return pl.pallas_call(
        _attn_cached_kernel,
        grid=(s // sb,),
        in_specs=[seq(t, aw), seq(t, kvw), seq(t, kvw), seq(w, kvw), seq(w, kvw), _resident(sk.shape), _resident(on.shape)],
        out_specs=[seq(t, aw), seq(w, kvw), seq(w, kvw)],
        out_shape=[jax.ShapeDtypeStruct((s, t, aw), BF16), jax.ShapeDtypeStruct((s, w, kvw), F32),
                   jax.ShapeDtypeStruct((s, w, kvw), F32)],
        scratch_shapes=[pltpu.VMEM((sb, jpad, kvw), F32), pltpu.VMEM((sb, jpad, kvw), F32)],
        compiler_params=pltpu.CompilerParams(dimension_semantics=("arbitrary",),
                                             vmem_limit_bytes=_vmem_limit(SMALL_KERNEL_VMEM_BYTES)),
        name="attn_sample",
    )(q3, kn3, vn3, ck, cv, sk, on)


ROPE_SPLIT = 128


def _rope_tables(start, n):
    half = HEAD_DIM // 2
    lane = jnp.arange(V7X_LANES, dtype=jnp.int32)
    inv_freq = ROPE_THETA ** (-(lane % half).astype(F32) / half)
    sign = jnp.where((lane % HEAD_DIM) < half, -1.0, 1.0).astype(F32)
    if n % ROPE_SPLIT:
        pos = start + jnp.arange(n, dtype=jnp.int32)
        ang = pos.astype(F32)[:, None] * inv_freq[None, :]
        return jnp.cos(ang), jnp.sin(ang) * sign[None, :]
    lo = jnp.arange(ROPE_SPLIT, dtype=jnp.int32)
    hi = start + ROPE_SPLIT * jnp.arange(n // ROPE_SPLIT, dtype=jnp.int32)
    a_hi = hi.astype(F32)[:, None] * inv_freq[None, :]
    a_lo = lo.astype(F32)[:, None] * inv_freq[None, :]
    ch, sh = jnp.cos(a_hi)[:, None, :], jnp.sin(a_hi)[:, None, :]
    cl, sl = jnp.cos(a_lo)[None, :, :], jnp.sin(a_lo)[None, :, :]
    cos = (ch * cl - sh * sl).reshape(n, V7X_LANES)
    sin = (sh * cl + ch * sl).reshape(n, V7X_LANES)
    return cos, sin * sign[None, :]


def _block_diag(w):
    n, c, d = w.shape
    eye = jnp.eye(n, dtype=w.dtype)
    return (w[:, :, None, :] * eye[:, None, :, None]).reshape(n * c, n * d)


def _row(v):
    return v.reshape(1, -1).astype(F32)


def kernel(x_prompt, x_sample, state_lru_h, state_conv, cache_k, cache_v, ffn1_norm, ffn1_w_gate, ffn1_w_up, ffn1_w_down, mix_norm, w_in, conv_w, conv_b, lru_wa, lru_ba, lru_wx, lru_bx, lru_lambda, q_norm, k_norm, attn_sinks, lru_out_norm, attn_out_norm, w_out, ffn2_norm, ffn2_w_gate, ffn2_w_up, ffn2_w_down):
    depth = w_in.shape[0]
    b, t, d = x_prompt.shape
    s, ts, _ = x_sample.shape
    c = lru_lambda.shape[1]
    aw = attn_out_norm.shape[1]
    kvw = N_KV_HEADS * HEAD_DIM
    win = cache_k.shape[2]
    assert t >= WINDOW and win == WINDOW
    assert GATE_DIAG_TILE % lru_wa.shape[2] == 0 and c % min(GATE_DIAG_TILE, c) == 0

    cos_p, sin_p = _rope_tables(0, t)
    cos_s, sin_s = _rope_tables(PAST_LEN, ts)
    cos_s, sin_s = jnp.tile(cos_s, (s, 1)), jnp.tile(sin_s, (s, 1))
    ones = _block_diag(jnp.full((V7X_LANES // HEAD_DIM, HEAD_DIM, HEAD_DIM), 1.0 / HEAD_DIM, BF16))
    log2e = 1.0 / jnp.log(jnp.float32(2.0))

    yp = x_prompt.reshape(b * t, d)
    ys = x_sample.reshape(s * ts, d)
    outs = [[] for _ in range(8)]
    for l in range(depth):
        ffn1 = (_row(ffn1_norm[l]), (0.5 * ffn1_w_gate[l]).astype(BF16), ffn1_w_up[l].astype(BF16),
                ffn1_w_down[l].astype(BF16))
        ffn2 = (_row(ffn2_norm[l]), (0.5 * ffn2_w_gate[l]).astype(BF16), ffn2_w_up[l].astype(BF16),
                ffn2_w_down[l].astype(BF16))
        win_b = w_in[l].astype(BF16)
        ffn_in_p = ffn1 + (_row(mix_norm[l]), win_b[:, :2 * c], win_b[:, 2 * c:])
        wout_b = w_out[l].astype(BF16)
        wol, woa = wout_b[:c], wout_b[c:]
        lru_p = (conv_w[l], _row(conv_b[l]), (0.5 * _block_diag(lru_wa[l])).astype(BF16), _row(0.5 * lru_ba[l]),
                 (0.5 * _block_diag(lru_wx[l])).astype(BF16), _row(0.5 * lru_bx[l]), _row(lru_lambda[l]),
                 _row(lru_out_norm[l]))
        reps = V7X_LANES // HEAD_DIM
        qn = _row(jnp.tile(q_norm[l], reps))
        kn = _row(jnp.tile(k_norm[l], reps))
        sk = jnp.broadcast_to(attn_sinks[l].astype(F32)[:, None], (attn_sinks.shape[1], V7X_LANES))
        qn_p = qn * (HEAD_DIM ** -0.5 * log2e)
        sk_rows = jnp.repeat(attn_sinks[l].astype(F32).reshape(N_KV_HEADS, -1), WINDOW, axis=1) * log2e
        a_on = _row(attn_out_norm[l])

        x1, lo, ao, hl, cs, pk, pv = _mixer_in(yp, t, cos_p, sin_p, ffn_in_p, lru_p, (qn_p, kn, sk_rows, ones, a_on))
        yp = _out_ffn(lo, ao, x1, wol, woa, *ffn2)
        outs[0].append(hl.reshape(b, c))
        outs[1].append(cs)
        outs[2].append(pk.reshape(b, WINDOW, N_KV_HEADS, HEAD_DIM))
        outs[3].append(pv.reshape(b, WINDOW, N_KV_HEADS, HEAD_DIM))

        x1s, pl_s, pa_s = _ffn_in(ys, *ffn_in_p)
        x_tm = pl_s.reshape(s, ts, 2 * c).transpose(1, 0, 2)
        cst_tm = state_conv[l].transpose(1, 0, 2)
        lo_tm, hls, cs_tm = _lru_sample(x_tm, cst_tm, state_lru_h[l], *lru_p)
        q_s, k_s = _qk_prep(pa_s, cos_s, sin_s, qn, kn, ones, aw)
        v_s = pa_s[:, aw + kvw:]
        aos, nk, nv = _attn_cached(q_s.reshape(s, ts, aw), k_s.reshape(s, ts, kvw), v_s.reshape(s, ts, kvw),
                                   cache_k[l].reshape(s, win, kvw), cache_v[l].reshape(s, win, kvw), sk, a_on)
        ys = _out_ffn(lo_tm.transpose(1, 0, 2).reshape(s * ts, c), aos.reshape(s * ts, aw), x1s, wol, woa, *ffn2)
        outs[4].append(hls)
        outs[5].append(cs_tm.transpose(1, 0, 2))
        outs[6].append(nk.reshape(s, win, N_KV_HEADS, HEAD_DIM))
        outs[7].append(nv.reshape(s, win, N_KV_HEADS, HEAD_DIM))

    st = [jnp.stack(o) for o in outs]
    return (yp.reshape(b, t, d), ys.reshape(s, ts, d), st[0], st[1], st[2], st[3], st[4], st[5], st[6], st[7])
```

```python
import functools

import jax
import jax.numpy as jnp
from jax import lax
from jax.experimental import pallas as pl
from jax.experimental.pallas import tpu as pltpu

F32 = jnp.float32
BF16 = jnp.bfloat16

NORM_EPS = 1e-6
LRU_C = 8.0
CONV_W = 4
HEAD_DIM = 64
N_KV_HEADS = 2
WINDOW = 128
ROPE_THETA = 10000.0
PAST_LEN = 16384

V7X_LANES = 128
V7X_SUBLANES = 8
V7X_MXU_DIM = 256
V7X_VMEM_BYTES = 64 * 1024 * 1024

TOKEN_TILE = 512
OUT_FFN_TILE = 1024
SAMPLE_SEQ_TILE = 16
FF_CHUNK = V7X_MXU_DIM
GATE_DIAG_TILE = V7X_MXU_DIM


MIB = 1024 * 1024
VMEM_KEEP_FREE_BYTES = 4 * MIB
FFN_TEMP_BYTES = 8 * MIB
MIXER_TEMP_BYTES = 20 * MIB
SMALL_KERNEL_VMEM_BYTES = 40 * MIB


def _vmem_limit(nbytes):
    return int(min(nbytes, V7X_VMEM_BYTES - VMEM_KEEP_FREE_BYTES))


def _resident(shape):
    nd = len(shape)
    return pl.BlockSpec(shape, lambda *_: (0,) * nd, pipeline_mode=pl.Buffered(1))


def _rms(x, g):
    ms = jnp.mean(x * x, axis=-1, keepdims=True)
    return (x * lax.rsqrt(ms + NORM_EPS)) * g


def _ffn(xn_ref, wg_ref, wu_ref, wd_ref, acc_ref):
    acc_ref[...] = jnp.zeros_like(acc_ref)
    for c0 in range(0, wg_ref.shape[1], FF_CHUNK):
        xn = xn_ref[...]
        h = jnp.dot(xn, wg_ref[:, c0:c0 + FF_CHUNK], preferred_element_type=F32)
        u = jnp.dot(xn, wu_ref[:, c0:c0 + FF_CHUNK], preferred_element_type=F32)
        a = ((h + h * jnp.tanh(h)) * u).astype(BF16)
        acc_ref[...] += jnp.dot(a, wd_ref[c0:c0 + FF_CHUNK, :], preferred_element_type=F32)
    return acc_ref[...]


def _ffn_in_tile(x_ref, ffn_refs, gm_ref, winl_ref, wina_ref, x1_ref, pl_ref, pa_ref, xn_ref, acc_ref):
    g1_ref, wg_ref, wu_ref, wd_ref = ffn_refs
    xn_ref[...] = _rms(x_ref[...], g1_ref[...]).astype(BF16)
    x1 = x_ref[...] + 0.5 * _ffn(xn_ref, wg_ref, wu_ref, wd_ref, acc_ref)
    x1_ref[...] = x1
    xn_ref[...] = _rms(x1, gm_ref[...]).astype(BF16)
    pl_ref[...] = jnp.dot(xn_ref[...], winl_ref[...], preferred_element_type=F32)
    pa_ref[...] = jnp.dot(xn_ref[...], wina_ref[...], preferred_element_type=F32)


def _lru_preact(xc, gate_w):
    wa_ref, ba_ref, wx_ref, bx_ref = gate_w
    xcb = xc.astype(BF16)
    c = xc.shape[-1]
    dt = min(GATE_DIAG_TILE, c)

    def blockdiag_dot(w_ref):
        return jnp.concatenate([jnp.dot(xcb[:, s:s + dt], w_ref[s:s + dt, s:s + dt], preferred_element_type=F32)
                                for s in range(0, c, dt)], axis=1)

    zr = blockdiag_dot(wa_ref) + ba_ref[...]
    zi = blockdiag_dot(wx_ref) + bx_ref[...]
    return zr, zi


def _lru_terms(xc, zr, zi, lam_ref):
    i = 0.5 * jnp.tanh(zi) + 0.5
    half_c = (-0.5 * LRU_C) * jax.nn.softplus(-lam_ref[...])
    log_a = half_c * jnp.tanh(zr) + half_c
    a = jnp.exp(log_a)
    w = -jnp.tanh(log_a) * (a * a + 1.0)
    root = jnp.where(w > 0.0, w * lax.rsqrt(w), 0.0)
    u = root * (i * xc)
    return a, u


def _lru_prep(p_ref, first, lru_refs, xh_ref, xc_ref, zr_ref, zi_ref, gate_ref, ctail_ref):
    cw_ref, cb_ref, wa_ref, ba_ref, wx_ref, bx_ref, _, _ = lru_refs
    tt = p_ref.shape[0]
    c = xc_ref.shape[1]
    hist = V7X_SUBLANES

    xh_ref[0:hist, :] = jnp.where(first, jnp.zeros((hist, c), F32), xh_ref[0:hist, :])
    x = p_ref[:, 0:c]
    xh_ref[hist:hist + tt, :] = x
    xg = xh_ref[...].reshape(tt // hist + 1, hist, c)
    sub = lax.broadcasted_iota(jnp.int32, (tt // hist, hist, c), 1)
    w = cw_ref[...]
    xc = cb_ref[...].reshape(1, 1, c)
    for j in range(CONV_W - 1):
        k = CONV_W - 1 - j
        rolled = pltpu.roll(xg, k, 1)
        xc = xc + jnp.where(sub >= k, rolled[1:], rolled[:-1]) * w[j:j + 1, :].reshape(1, 1, c)
    xc = (xc + xg[1:] * w[CONV_W - 1:CONV_W, :].reshape(1, 1, c)).reshape(tt, c)
    xh_ref[0:hist, :] = x[tt - hist:tt, :]
    ctail_ref[...] = x[tt - (CONV_W - 1):tt, :]

    xc_ref[...] = xc
    zr, zi = _lru_preact(xc, (wa_ref, ba_ref, wx_ref, bx_ref))
    zr_ref[...] = zr
    zi_ref[...] = zi
    gate_ref[...] = p_ref[:, c:2 * c]


def _lru_core(first, lru_refs, xc_ref, zr_ref, zi_ref, gate_ref, ctail_ref, out_ref, hl_ref, cs_ref, h_ref):
    lam_ref, on_ref = lru_refs[6], lru_refs[7]
    tt, c = xc_ref.shape

    a, u = _lru_terms(xc_ref[...], zr_ref[...], zi_ref[...], lam_ref)

    groups = tt // V7X_SUBLANES
    a3 = a.reshape(groups, V7X_SUBLANES, c)
    u3 = u.reshape(groups, V7X_SUBLANES, c)
    sub = lax.broadcasted_iota(jnp.int32, a3.shape, 1)
    k = 1
    while k < V7X_SUBLANES:
        m = sub >= k
        a_prev = pltpu.roll(a3, k, 1)
        u_prev = pltpu.roll(u3, k, 1)
        u3 = jnp.where(m, a3 * u_prev + u3, u3)
        a3 = jnp.where(m, a3 * a_prev, a3)
        k *= 2

    h = jnp.where(first, jnp.zeros((1, c), F32), h_ref[...])
    hs = []
    for g in range(groups):
        hr = a3[g] * h + u3[g]
        hs.append(hr)
        h = hr[V7X_SUBLANES - 1:V7X_SUBLANES, :]
    h_ref[...] = h

    y = jnp.concatenate(hs, axis=0) * jax.nn.gelu(gate_ref[...])
    out_ref[...] = _rms(y, on_ref[...]).astype(out_ref.dtype)
    hl_ref[0] = h
    cs_ref[0] = ctail_ref[...]


def _lru_sample_kernel(x_ref, cst_ref, h0_ref, cw_ref, cb_ref, wa_ref, ba_ref, wx_ref, bx_ref, lam_ref, on_ref,
                       out_ref, hl_ref, cs_ref):
    t = x_ref.shape[0]
    c = out_ref.shape[2]
    w = cw_ref[...]
    rows = [cst_ref[j] for j in range(CONV_W - 1)] + [x_ref[j, :, 0:c] for j in range(t)]
    h = h0_ref[...]
    for s in range(t):
        xc = cb_ref[...]
        for j in range(CONV_W):
            xc = xc + rows[s + j] * w[j:j + 1, :]
        zr, zi = _lru_preact(xc, (wa_ref, ba_ref, wx_ref, bx_ref))
        a, u = _lru_terms(xc, zr, zi, lam_ref)
        h = a * h + u
        y = h * jax.nn.gelu(x_ref[s, :, c:2 * c])
        out_ref[s] = _rms(y, on_ref[...]).astype(out_ref.dtype)
    hl_ref[...] = h
    for j in range(CONV_W - 1):
        cs_ref[j] = rows[t + j]


def _lru_sample(x_tm, cst_tm, h0, cw, cb, wa, ba, wx, bx, lam, on):
    t, s, w2 = x_tm.shape
    c = w2 // 2
    return pl.pallas_call(
        _lru_sample_kernel,
        out_shape=[jax.ShapeDtypeStruct((t, s, c), BF16), jax.ShapeDtypeStruct((s, c), F32),
                   jax.ShapeDtypeStruct((CONV_W - 1, s, c), F32)],
        compiler_params=pltpu.CompilerParams(vmem_limit_bytes=_vmem_limit(SMALL_KERNEL_VMEM_BYTES)),
        name="lru_sample",
    )(x_tm, cst_tm, h0, cw, cb, wa, ba, wx, bx, lam, on)


def _head_norm_rope(xc, gain, cos, sin, ones_ref):
    x2 = xc * xc
    hi = x2.astype(BF16)
    lo = (x2 - hi.astype(F32)).astype(BF16)
    ss = jnp.dot(hi, ones_ref[...], preferred_element_type=F32) + jnp.dot(lo, ones_ref[...], preferred_element_type=F32)
    xn = (xc * lax.rsqrt(ss + NORM_EPS)) * gain
    lane = lax.broadcasted_iota(jnp.int32, xn.shape, xn.ndim - 1)
    first_half = (lane & (HEAD_DIM // 2)) == 0
    ax = xn.ndim - 1
    swapped = jnp.where(first_half, pltpu.roll(xn, V7X_LANES - HEAD_DIM // 2, ax), pltpu.roll(xn, HEAD_DIM // 2, ax))
    return xn * cos + swapped * sin


def _group_lhs(qcols, g, lane_lo, axis):
    zero = jnp.zeros_like(qcols[0])
    parts = [jnp.where(lane_lo, qcols[2 * g], zero), jnp.where(lane_lo, qcols[2 * g + 1], zero),
             jnp.where(lane_lo, zero, qcols[2 * g]), jnp.where(lane_lo, zero, qcols[2 * g + 1])]
    return jnp.concatenate(parts, axis=axis).astype(BF16)


def _attn_prep(pa_ref, cos_ref, sin_ref, attn_refs, qh_ref, k2_ref, vt_ref, kvl_ref):
    qn_ref, kn_ref, _, ones_ref, _ = attn_refs
    tq = pa_ref.shape[0]
    blk = WINDOW
    lanes = V7X_LANES
    ncol = qh_ref.shape[0] // 2
    aw = ncol * lanes

    cos = cos_ref[...]
    sin = sin_ref[...]
    cols = [pa_ref[:, cc * lanes:(cc + 1) * lanes] for cc in range(ncol + 1)]
    sq = [x * x for x in cols]
    hi = [x.astype(BF16) for x in sq]
    lo = [(x - h.astype(F32)).astype(BF16) for x, h in zip(sq, hi)]
    ss = (jnp.dot(jnp.concatenate(hi, axis=0), ones_ref[...], preferred_element_type=F32)
          + jnp.dot(jnp.concatenate(lo, axis=0), ones_ref[...], preferred_element_type=F32))
    lane = lax.broadcasted_iota(jnp.int32, (tq, lanes), 1)
    first_half = (lane & (HEAD_DIM // 2)) == 0
    lane_lo = lane < HEAD_DIM
    rot = []
    for cc, x in enumerate(cols):
        gain = qn_ref[...] if cc < ncol else kn_ref[...]
        xn = (x * lax.rsqrt(ss[cc * tq:(cc + 1) * tq] + NORM_EPS)) * gain
        swapped = jnp.where(first_half, pltpu.roll(xn, lanes - HEAD_DIM // 2, 1), pltpu.roll(xn, HEAD_DIM // 2, 1))
        rot.append(xn * cos + swapped * sin)
    k = rot[ncol]
    v = pa_ref[:, aw + lanes:aw + 2 * lanes]

    zero = jnp.zeros((tq, lanes), F32)
    for cc in range(ncol):
        qh_ref[2 * cc] = jnp.where(lane_lo, rot[cc], zero).astype(BF16)
        qh_ref[2 * cc + 1] = jnp.where(lane_lo, zero, rot[cc]).astype(BF16)

    k_sw = pltpu.roll(k, HEAD_DIM, 1)
    k2_ref[0, blk:blk + tq, :] = jnp.where(lane_lo, k, k_sw).astype(BF16)
    k2_ref[1, blk:blk + tq, :] = jnp.where(lane_lo, k_sw, k).astype(BF16)
    for jb in range(tq // blk):
        vt_ref[:, (jb + 1) * blk:(jb + 2) * blk] = v[jb * blk:(jb + 1) * blk].T.astype(BF16)
    kvl_ref[0] = k[tq - blk:tq]
    kvl_ref[1] = v[tq - blk:tq]


def _attn_block(jb, first, attn_refs, out_ref, pk_ref, pv_ref, qh_ref, k2_ref, vt_ref, kvl_ref):
    _, _, sk_ref, _, on_ref = attn_refs
    tq = qh_ref.shape[1]
    blk = WINDOW
    nblk = tq // blk
    heads_per_group = qh_ref.shape[0] // N_KV_HEADS
    r0 = jb * blk

    ncolq = heads_per_group * blk
    kj = lax.broadcasted_iota(jnp.int32, (blk, ncolq), 0)
    qi = lax.broadcasted_iota(jnp.int32, (blk, ncolq), 1) % blk
    tri = kj <= qi

    ocols = []
    for g in range(N_KV_HEADS):
        lhs = jnp.concatenate([qh_ref[h, r0:r0 + blk, :]
                               for h in range(g * heads_per_group, (g + 1) * heads_per_group)], axis=0)
        st = lax.dot_general(k2_ref[g, r0:r0 + 2 * blk, :], lhs, (((1,), (1,)), ((), ())),
                             preferred_element_type=F32)
        s_prev = st[0:blk]
        if jb == 0:
            s_prev = s_prev + jnp.where(first, jnp.float32(-jnp.inf), jnp.float32(0.0))
        m = jnp.where(tri, st[blk:2 * blk], s_prev)
        sk = sk_ref[g:g + 1, :]
        mx = jnp.maximum(jnp.max(m, axis=0, keepdims=True), sk)
        p = jnp.exp2(m - mx)
        den = jnp.sum(p, axis=0, keepdims=True) + jnp.exp2(sk - mx)
        pz = jnp.zeros_like(p)
        pt = jnp.concatenate([jnp.where(tri, pz, p), jnp.where(tri, p, pz)], axis=0).astype(BF16)
        vt_g = vt_ref[g * HEAD_DIM:(g + 1) * HEAD_DIM, r0:r0 + 2 * blk]
        og = jnp.dot(vt_g, pt, preferred_element_type=F32) * (1.0 / den)
        for half in range(heads_per_group // 2):
            pair = jnp.concatenate([og[:, (2 * half) * blk:(2 * half + 1) * blk],
                                    og[:, (2 * half + 1) * blk:(2 * half + 2) * blk]], axis=0)
            ocols.append(pair.T)
    o = jnp.concatenate(ocols, axis=1)
    out_ref[r0:r0 + blk, :] = _rms(o, on_ref[...]).astype(out_ref.dtype)

    if jb == nblk - 1:
        k2_ref[:, 0:blk, :] = k2_ref[:, tq:tq + blk, :]
        vt_ref[:, 0:blk] = vt_ref[:, tq:tq + blk]
        pk_ref[0] = kvl_ref[0]
        pv_ref[0] = kvl_ref[1]


N_FFN_IN = 7
N_LRU = 8
N_ATTN = 5


def _mixer_in_kernel(*refs, tiles_per_seq, ntile):
    x_ref, cos_ref, sin_ref = refs[0:3]
    p0 = 3
    g1_ref, wg_ref, wu_ref, wd_ref, gm_ref, winl_ref, wina_ref = refs[p0:p0 + N_FFN_IN]
    lru_refs = refs[p0 + N_FFN_IN:p0 + N_FFN_IN + N_LRU]
    attn_refs = refs[p0 + N_FFN_IN + N_LRU:p0 + N_FFN_IN + N_LRU + N_ATTN]
    o0 = p0 + N_FFN_IN + N_LRU + N_ATTN
    x1_ref, lo_ref, ao_ref, hl_ref, cs_ref, pk_ref, pv_ref = refs[o0:o0 + 7]
    (xn_ref, acc_ref, pl_ref, pa_ref, xh_ref, h_ref, xc_ref, zr_ref, zi_ref, gate_ref, ctail_ref,
     qh_ref, k2_ref, vt_ref, kvl_ref) = refs[o0 + 7:]

    i = pl.program_id(0)

    @pl.when(i == 0)
    def _():
        for ref in (pl_ref, pa_ref, xh_ref, h_ref, xc_ref, zr_ref, zi_ref, gate_ref, ctail_ref,
                    qh_ref, k2_ref, vt_ref, kvl_ref):
            ref[...] = jnp.zeros_like(ref)

    def step(with_ffn):
        first = lax.rem(jnp.maximum(i - 1, 0), tiles_per_seq) == 0
        _lru_prep(pl_ref, first, lru_refs, xh_ref, xc_ref, zr_ref, zi_ref, gate_ref, ctail_ref)
        _attn_prep(pa_ref, cos_ref, sin_ref, attn_refs, qh_ref, k2_ref, vt_ref, kvl_ref)
        _lru_core(first, lru_refs, xc_ref, zr_ref, zi_ref, gate_ref, ctail_ref, lo_ref, hl_ref, cs_ref, h_ref)
        for jb in range(pa_ref.shape[0] // WINDOW):
            _attn_block(jb, first, attn_refs, ao_ref, pk_ref, pv_ref, qh_ref, k2_ref, vt_ref, kvl_ref)
        if with_ffn:
            _ffn_in_tile(x_ref, (g1_ref, wg_ref, wu_ref, wd_ref), gm_ref, winl_ref, wina_ref,
                         x1_ref, pl_ref, pa_ref, xn_ref, acc_ref)

    pl.when(i < ntile)(functools.partial(step, True))
    pl.when(i >= ntile)(functools.partial(step, False))


def _mixer_in(x, seq_len, cos, sin, ffn_in_p, lru_p, attn_p):
    n, d = x.shape
    tm = TOKEN_TILE
    assert seq_len % tm == 0 and n % seq_len == 0 and tm % WINDOW == 0
    nseq = n // seq_len
    tps = seq_len // tm
    ntile = n // tm
    wg, winl, wina = ffn_in_p[1], ffn_in_p[5], ffn_in_p[6]
    w2, wa = winl.shape[1], wina.shape[1]
    c = w2 // 2
    aw = attn_p[4].shape[1]
    kvw = N_KV_HEADS * HEAD_DIM
    cur = lambda i: jnp.minimum(i, ntile - 1)
    mix = lambda i: jnp.maximum(i - 1, 0)
    params = list(ffn_in_p) + list(lru_p) + list(attn_p)
    weights = sum(p.size * p.dtype.itemsize for p in params)
    tiles = 2 * 4 * tm * 2 * d + 2 * 2 * tm * (c + aw) + 4 * tm * (w2 + wa) + tm * d * (2 + 4) + 6 * 4 * tm * c
    return pl.pallas_call(
        functools.partial(_mixer_in_kernel, tiles_per_seq=tps, ntile=ntile),
        grid=(ntile + 1,),
        in_specs=[pl.BlockSpec((tm, d), lambda i: (cur(i), 0)),
                  pl.BlockSpec((tm, V7X_LANES), lambda i: (lax.rem(mix(i), tps), 0)),
                  pl.BlockSpec((tm, V7X_LANES), lambda i: (lax.rem(mix(i), tps), 0))]
                 + [_resident(p.shape) for p in params],
        out_specs=[pl.BlockSpec((tm, d), lambda i: (cur(i), 0)),
                   pl.BlockSpec((tm, c), lambda i: (mix(i), 0)),
                   pl.BlockSpec((tm, aw), lambda i: (mix(i), 0)),
                   pl.BlockSpec((1, 1, c), lambda i: (mix(i) // tps, 0, 0)),
                   pl.BlockSpec((1, CONV_W - 1, c), lambda i: (mix(i) // tps, 0, 0)),
                   pl.BlockSpec((1, WINDOW, kvw), lambda i: (mix(i) // tps, 0, 0)),
                   pl.BlockSpec((1, WINDOW, kvw), lambda i: (mix(i) // tps, 0, 0))],
        out_shape=[jax.ShapeDtypeStruct((n, d), F32), jax.ShapeDtypeStruct((n, c), BF16),
                   jax.ShapeDtypeStruct((n, aw), BF16), jax.ShapeDtypeStruct((nseq, 1, c), F32),
                   jax.ShapeDtypeStruct((nseq, CONV_W - 1, c), F32), jax.ShapeDtypeStruct((nseq, WINDOW, kvw), F32),
                   jax.ShapeDtypeStruct((nseq, WINDOW, kvw), F32)],
        scratch_shapes=[pltpu.VMEM((tm, d), BF16), pltpu.VMEM((tm, d), F32),
                        pltpu.VMEM((tm, w2), F32), pltpu.VMEM((tm, wa), F32),
                        pltpu.VMEM((tm + V7X_SUBLANES, c), F32), pltpu.VMEM((1, c), F32),
                        pltpu.VMEM((tm, c), F32), pltpu.VMEM((tm, c), F32), pltpu.VMEM((tm, c), F32),
                        pltpu.VMEM((tm, c), F32), pltpu.VMEM((CONV_W - 1, c), F32),
                        pltpu.VMEM((2 * aw // V7X_LANES, tm, V7X_LANES), BF16),
                        pltpu.VMEM((N_KV_HEADS, WINDOW + tm, V7X_LANES), BF16),
                        pltpu.VMEM((V7X_LANES, WINDOW + tm), BF16),
                        pltpu.VMEM((2, WINDOW, kvw), F32)],
        compiler_params=pltpu.CompilerParams(dimension_semantics=("arbitrary",),
                                             vmem_limit_bytes=_vmem_limit(weights + tiles + MIXER_TEMP_BYTES)),
        name="mixer_in",
    )(x, cos, sin, *params)


def _ffn_in_kernel(x_ref, g1_ref, wg_ref, wu_ref, wd_ref, gm_ref, winl_ref, wina_ref,
                   x1_ref, pl_ref, pa_ref, xn_ref, acc_ref):
    _ffn_in_tile(x_ref, (g1_ref, wg_ref, wu_ref, wd_ref), gm_ref, winl_ref, wina_ref,
                 x1_ref, pl_ref, pa_ref, xn_ref, acc_ref)


def _ffn_in(x, g1, wg, wu, wd, gm, winl, wina):
    n, d = x.shape
    tm = min(TOKEN_TILE, n)
    wl, wa = winl.shape[1], wina.shape[1]
    row = lambda w: pl.BlockSpec((tm, w), lambda i: (i, 0))
    weights = 2 * (wg.size + wu.size + wd.size + winl.size + wina.size)
    tiles = 2 * 4 * tm * (2 * d + wl + wa) + tm * d * (2 + 4) + 4 * 4 * tm * max(FF_CHUNK, wl)
    return pl.pallas_call(
        _ffn_in_kernel,
        grid=(n // tm,),
        in_specs=[row(d), _resident(g1.shape), _resident(wg.shape), _resident(wu.shape), _resident(wd.shape),
                  _resident(gm.shape), _resident(winl.shape), _resident(wina.shape)],
        out_specs=[row(d), row(wl), row(wa)],
        out_shape=[jax.ShapeDtypeStruct((n, d), F32), jax.ShapeDtypeStruct((n, wl), F32),
                   jax.ShapeDtypeStruct((n, wa), F32)],
        scratch_shapes=[pltpu.VMEM((tm, d), BF16), pltpu.VMEM((tm, d), F32)],
        compiler_params=pltpu.CompilerParams(dimension_semantics=("arbitrary",),
                                             vmem_limit_bytes=_vmem_limit(weights + tiles + FFN_TEMP_BYTES)),
        name="ffn_in",
    )(x, g1, wg, wu, wd, gm, winl, wina)


def _out_ffn_kernel(lo_ref, ao_ref, x1_ref, wol_ref, woa_ref, g2_ref, wg_ref, wu_ref, wd_ref,
                    out_ref, xn_ref, acc_ref):
    y = jnp.dot(lo_ref[...], wol_ref[...], preferred_element_type=F32)
    y = y + jnp.dot(ao_ref[...], woa_ref[...], preferred_element_type=F32)
    x2 = x1_ref[...] + y
    out_ref[...] = x2
    xn_ref[...] = _rms(x2, g2_ref[...]).astype(BF16)
    out_ref[...] = out_ref[...] + 0.5 * _ffn(xn_ref, wg_ref, wu_ref, wd_ref, acc_ref)


def _out_ffn(lo, ao, x1, wol, woa, g2, wg, wu, wd):
    n, d = x1.shape
    tm = OUT_FFN_TILE if n % OUT_FFN_TILE == 0 else min(TOKEN_TILE, n)
    row = lambda w: pl.BlockSpec((tm, w), lambda i: (i, 0))
    weights = 2 * (wg.size + wu.size + wd.size + wol.size + woa.size)
    tiles = 2 * tm * (2 * lo.shape[1] + 2 * ao.shape[1] + 8 * d) + tm * d * (2 + 4) + 4 * 4 * tm * d
    return pl.pallas_call(
        _out_ffn_kernel,
        grid=(n // tm,),
        in_specs=[row(lo.shape[1]), row(ao.shape[1]), row(d), _resident(wol.shape), _resident(woa.shape),
                  _resident(g2.shape), _resident(wg.shape), _resident(wu.shape), _resident(wd.shape)],
        out_specs=row(d),
        out_shape=jax.ShapeDtypeStruct((n, d), F32),
        scratch_shapes=[pltpu.VMEM((tm, d), BF16), pltpu.VMEM((tm, d), F32)],
        compiler_params=pltpu.CompilerParams(dimension_semantics=("arbitrary",),
                                             vmem_limit_bytes=_vmem_limit(weights + tiles + FFN_TEMP_BYTES)),
        name="out_ffn",
    )(lo, ao, x1, wol, woa, g2, wg, wu, wd)


def _qk_prep_kernel(pa_ref, cos_ref, sin_ref, qn_ref, kn_ref, ones_ref, q_ref, k_ref):
    lanes = V7X_LANES
    aw = q_ref.shape[1]
    cos = cos_ref[...]
    sin = sin_ref[...]
    for cc in range(aw // lanes):
        q_ref[:, cc * lanes:(cc + 1) * lanes] = _head_norm_rope(
            pa_ref[:, cc * lanes:(cc + 1) * lanes], qn_ref[...], cos, sin, ones_ref) * (HEAD_DIM ** -0.5)
    k_ref[...] = _head_norm_rope(pa_ref[:, aw:aw + lanes], kn_ref[...], cos, sin, ones_ref)


def _qk_prep(pa, cos, sin, qn, kn, ones, aw):
    n = pa.shape[0]
    return pl.pallas_call(
        _qk_prep_kernel,
        out_shape=[jax.ShapeDtypeStruct((n, aw), F32), jax.ShapeDtypeStruct((n, N_KV_HEADS * HEAD_DIM), F32)],
        compiler_params=pltpu.CompilerParams(vmem_limit_bytes=_vmem_limit(SMALL_KERNEL_VMEM_BYTES)),
        name="qk_prep_sample",
    )(pa, cos, sin, qn, kn, ones)


def _attn_cached_kernel(q_ref, kn_ref, vn_ref, ck_ref, cv_ref, sk_ref, on_ref,
                        out_ref, nk_ref, nv_ref, kk_ref, vv_ref):
    sb, t, aw = q_ref.shape
    w = ck_ref.shape[1]
    lanes = V7X_LANES
    jpad = kk_ref.shape[1]

    kk_ref[:, 0:w, :] = ck_ref[...]
    vv_ref[:, 0:w, :] = cv_ref[...]
    kk_ref[:, w:w + t, :] = kn_ref[...]
    vv_ref[:, w:w + t, :] = vn_ref[...]
    kk_ref[:, w + t:jpad, :] = jnp.zeros((sb, jpad - w - t, lanes), F32)
    vv_ref[:, w + t:jpad, :] = jnp.zeros((sb, jpad - w - t, lanes), F32)
    nk_ref[...] = kk_ref[:, t:t + w, :]
    nv_ref[...] = vv_ref[:, t:t + w, :]

    kk = kk_ref[...]
    vv = vv_ref[...]
    k_sw = pltpu.roll(kk, HEAD_DIM, 2)
    v_sw = pltpu.roll(vv, HEAD_DIM, 2)
    lane_k = lax.broadcasted_iota(jnp.int32, kk.shape, 2) < HEAD_DIM
    lane_q = lax.broadcasted_iota(jnp.int32, (sb, t, lanes), 2) < HEAD_DIM
    k2 = (jnp.where(lane_k, kk, k_sw).astype(BF16), jnp.where(lane_k, k_sw, kk).astype(BF16))
    v2 = (vv.astype(BF16), v_sw.astype(BF16))

    qcols = [q_ref[:, :, cc * lanes:(cc + 1) * lanes] for cc in range(aw // lanes)]
    tq = lax.broadcasted_iota(jnp.int32, (sb, 4 * t, jpad), 1) % t
    j = lax.broadcasted_iota(jnp.int32, (sb, 4 * t, jpad), 2)
    valid = jnp.logical_and(j > tq, j <= tq + w)
    neg_inf = jnp.float32(-jnp.inf)

    ocols = []
    for g in range(N_KV_HEADS):
        lhs = _group_lhs(qcols, g, lane_q, 1)
        s = jnp.einsum('bqd,bjd->bqj', lhs, k2[g], preferred_element_type=F32)
        s = jnp.where(valid, s, neg_inf)
        heads = (4 * g, 4 * g + 2, 4 * g + 1, 4 * g + 3)
        sk = jnp.concatenate([jnp.broadcast_to(sk_ref[h:h + 1, 0:1].reshape(1, 1, 1), (sb, t, 1)) for h in heads],
                             axis=1)
        mx = jnp.maximum(jnp.max(s, axis=-1, keepdims=True), sk)
        p = jnp.exp(s - mx)
        den = jnp.sum(p, axis=-1, keepdims=True) + jnp.exp(sk - mx)
        pb = p.astype(BF16)
        o_even = jnp.einsum('bqj,bjd->bqd', pb[:, 0:2 * t], v2[g], preferred_element_type=F32) / den[:, 0:2 * t]
        o_odd = jnp.einsum('bqj,bjd->bqd', pb[:, 2 * t:4 * t], v2[1 - g], preferred_element_type=F32) / den[:, 2 * t:4 * t]
        for half in range(2):
            ocols.append(jnp.where(lane_q, o_even[:, half * t:(half + 1) * t], o_odd[:, half * t:(half + 1) * t]))
    o = jnp.concatenate(ocols, axis=2)
    out_ref[...] = _rms(o, on_ref[...].reshape(1, 1, aw)).astype(out_ref.dtype)


def _attn_cached(q3, kn3, vn3, ck, cv, sk, on):
    s, t, aw = q3.shape
    w, kvw = ck.shape[1], ck.shape[2]
    sb = min(SAMPLE_SEQ_TILE, s)
    jpad = -(-(w + t) // V7X_SUBLANES) * V7X_SUBLANES
    seq = lambda a, c: pl.BlockSpec((sb, a, c), lambda i: (i, 0, 0))
    return pl.pallas_call(
        _attn_cached_kernel,
        grid=(s // sb,),
        in_specs=[seq(t, aw), seq(t, kvw), seq(t, kvw), seq(w, kvw), seq(w, kvw), _resident(sk.shape), _resident(on.shape)],
        out_specs=[seq(t, aw), seq(w, kvw), seq(w, kvw)],
        out_shape=[jax.ShapeDtypeStruct((s, t, aw), BF16), jax.ShapeDtypeStruct((s, w, kvw), F32),
                   jax.ShapeDtypeStruct((s, w, kvw), F32)],
        scratch_shapes=[pltpu.VMEM((sb, jpad, kvw), F32), pltpu.VMEM((sb, jpad, kvw), F32)],
        compiler_params=pltpu.CompilerParams(dimension_semantics=("arbitrary",),
                                             vmem_limit_bytes=_vmem_limit(SMALL_KERNEL_VMEM_BYTES)),
        name="attn_sample",
    )(q3, kn3, vn3, ck, cv, sk, on)


ROPE_SPLIT = 128


def _rope_tables(start, n):
    half = HEAD_DIM // 2
    lane = jnp.arange(V7X_LANES, dtype=jnp.int32)
    inv_freq = ROPE_THETA ** (-(lane % half).astype(F32) / half)
    sign = jnp.where((lane % HEAD_DIM) < half, -1.0, 1.0).astype(F32)
    if n % ROPE_SPLIT:
        pos = start + jnp.arange(n, dtype=jnp.int32)
        ang = pos.astype(F32)[:, None] * inv_freq[None, :]
        return jnp.cos(ang), jnp.sin(ang) * sign[None, :]
    lo = jnp.arange(ROPE_SPLIT, dtype=jnp.int32)
    hi = start + ROPE_SPLIT * jnp.arange(n // ROPE_SPLIT, dtype=jnp.int32)
    a_hi = hi.astype(F32)[:, None] * inv_freq[None, :]
    a_lo = lo.astype(F32)[:, None] * inv_freq[None, :]
    ch, sh = jnp.cos(a_hi)[:, None, :], jnp.sin(a_hi)[:, None, :]
    cl, sl = jnp.cos(a_lo)[None, :, :], jnp.sin(a_lo)[None, :, :]
    cos = (ch * cl - sh * sl).reshape(n, V7X_LANES)
    sin = (sh * cl + ch * sl).reshape(n, V7X_LANES)
    return cos, sin * sign[None, :]


def _block_diag(w):
    n, c, d = w.shape
    eye = jnp.eye(n, dtype=w.dtype)
    return (w[:, :, None, :] * eye[:, None, :, None]).reshape(n * c, n * d)


def _row(v):
    return v.reshape(1, -1).astype(F32)


def kernel(x_prompt, x_sample, state_lru_h, state_conv, cache_k, cache_v, ffn1_norm, ffn1_w_gate, ffn1_w_up, ffn1_w_down, mix_norm, w_in, conv_w, conv_b, lru_wa, lru_ba, lru_wx, lru_bx, lru_lambda, q_norm, k_norm, attn_sinks, lru_out_norm, attn_out_norm, w_out, ffn2_norm, ffn2_w_gate, ffn2_w_up, ffn2_w_down):
    depth = w_in.shape[0]
    b, t, d = x_prompt.shape
    s, ts, _ = x_sample.shape
    c = lru_lambda.shape[1]
    aw = attn_out_norm.shape[1]
    kvw = N_KV_HEADS * HEAD_DIM
    win = cache_k.shape[2]
    assert t >= WINDOW and win == WINDOW
    assert GATE_DIAG_TILE % lru_wa.shape[2] == 0 and c % min(GATE_DIAG_TILE, c) == 0

    cos_p, sin_p = _rope_tables(0, t)
    cos_s, sin_s = _rope_tables(PAST_LEN, ts)
    cos_s, sin_s = jnp.tile(cos_s, (s, 1)), jnp.tile(sin_s, (s, 1))
    ones = _block_diag(jnp.full((V7X_LANES // HEAD_DIM, HEAD_DIM, HEAD_DIM), 1.0 / HEAD_DIM, BF16))
    log2e = 1.0 / jnp.log(jnp.float32(2.0))

    yp = x_prompt.reshape(b * t, d)
    ys = x_sample.reshape(s * ts, d)
    outs = [[] for _ in range(8)]
    for l in range(depth):
        ffn1 = (_row(ffn1_norm[l]), (0.5 * ffn1_w_gate[l]).astype(BF16), ffn1_w_up[l].astype(BF16),
                ffn1_w_down[l].astype(BF16))
        ffn2 = (_row(ffn2_norm[l]), (0.5 * ffn2_w_gate[l]).astype(BF16), ffn2_w_up[l].astype(BF16),
                ffn2_w_down[l].astype(BF16))
        win_b = w_in[l].astype(BF16)
        ffn_in_p = ffn1 + (_row(mix_norm[l]), win_b[:, :2 * c], win_b[:, 2 * c:])
        wout_b = w_out[l].astype(BF16)
        wol, woa = wout_b[:c], wout_b[c:]
        lru_p = (conv_w[l], _row(conv_b[l]), (0.5 * _block_diag(lru_wa[l])).astype(BF16), _row(0.5 * lru_ba[l]),
                 (0.5 * _block_diag(lru_wx[l])).astype(BF16), _row(0.5 * lru_bx[l]), _row(lru_lambda[l]),
                 _row(lru_out_norm[l]))
        reps = V7X_LANES // HEAD_DIM
        qn = _row(jnp.tile(q_norm[l], reps))
        kn = _row(jnp.tile(k_norm[l], reps))
        sk = jnp.broadcast_to(attn_sinks[l].astype(F32)[:, None], (attn_sinks.shape[1], V7X_LANES))
        qn_p = qn * (HEAD_DIM ** -0.5 * log2e)
        sk_rows = jnp.repeat(attn_sinks[l].astype(F32).reshape(N_KV_HEADS, -1), WINDOW, axis=1) * log2e
        a_on = _row(attn_out_norm[l])

        x1, lo, ao, hl, cs, pk, pv = _mixer_in(yp, t, cos_p, sin_p, ffn_in_p, lru_p, (qn_p, kn, sk_rows, ones, a_on))
        yp = _out_ffn(lo, ao, x1, wol, woa, *ffn2)
        outs[0].append(hl.reshape(b, c))
        outs[1].append(cs)
        outs[2].append(pk.reshape(b, WINDOW, N_KV_HEADS, HEAD_DIM))
        outs[3].append(pv.reshape(b, WINDOW, N_KV_HEADS, HEAD_DIM))

        x1s, pl_s, pa_s = _ffn_in(ys, *ffn_in_p)
        x_tm = pl_s.reshape(s, ts, 2 * c).transpose(1, 0, 2)
        cst_tm = state_conv[l].transpose(1, 0, 2)
        lo_tm, hls, cs_tm = _lru_sample(x_tm, cst_tm, state_lru_h[l], *lru_p)
        q_s, k_s = _qk_prep(pa_s, cos_s, sin_s, qn, kn, ones, aw)
        v_s = pa_s[:, aw + kvw:]
        aos, nk, nv = _attn_cached(q_s.reshape(s, ts, aw), k_s.reshape(s, ts, kvw), v_s.reshape(s, ts, kvw),
                                   cache_k[l].reshape(s, win, kvw), cache_v[l].reshape(s, win, kvw), sk, a_on)
        ys = _out_ffn(lo_tm.transpose(1, 0, 2).reshape(s * ts, c), aos.reshape(s * ts, aw), x1s, wol, woa, *ffn2)
        outs[4].append(hls)
        outs[5].append(cs_tm.transpose(1, 0, 2))
        outs[6].append(nk.reshape(s, win, N_KV_HEADS, HEAD_DIM))
        outs[7].append(nv.reshape(s, win, N_KV_HEADS, HEAD_DIM))

    st = [jnp.stack(o) for o in outs]
    return (yp.reshape(b, t, d), ys.reshape(s, ts, d), st[0], st[1], st[2], st[3], st[4], st[5], st[6], st[7])
```

```python
import functools

import jax
import jax.numpy as jnp
from jax import lax
from jax.experimental import pallas as pl
from jax.experimental.pallas import tpu as pltpu

F32 = jnp.float32
BF16 = jnp.bfloat16

NORM_EPS = 1e-6
LRU_C = 8.0
CONV_W = 4
HEAD_DIM = 64
N_KV_HEADS = 2
WINDOW = 128
ROPE_THETA = 10000.0
PAST_LEN = 16384

V7X_LANES = 128
V7X_SUBLANES = 8
V7X_MXU_DIM = 256
V7X_VMEM_BYTES = 64 * 1024 * 1024

TOKEN_TILE = 512
OUT_FFN_TILE = 1024
SAMPLE_SEQ_TILE = 16
FF_CHUNK = V7X_MXU_DIM
GATE_DIAG_TILE = V7X_MXU_DIM


MIB = 1024 * 1024
VMEM_KEEP_FREE_BYTES = 4 * MIB
FFN_TEMP_BYTES = 8 * MIB
MIXER_TEMP_BYTES = 20 * MIB
SMALL_KERNEL_VMEM_BYTES = 40 * MIB


def _vmem_limit(nbytes):
    return int(min(nbytes, V7X_VMEM_BYTES - VMEM_KEEP_FREE_BYTES))


def _resident(shape):
    nd = len(shape)
    return pl.BlockSpec(shape, lambda *_: (0,) * nd, pipeline_mode=pl.Buffered(1))


def _rms(x, g):
    ms = jnp.mean(x * x, axis=-1, keepdims=True)
    return (x * lax.rsqrt(ms + NORM_EPS)) * g


def _ffn(xn_ref, wg_ref, wu_ref, wd_ref, acc_ref):
    acc_ref[...] = jnp.zeros_like(acc_ref)
    for c0 in range(0, wg_ref.shape[1], FF_CHUNK):
        xn = xn_ref[...]
        h = jnp.dot(xn, wg_ref[:, c0:c0 + FF_CHUNK], preferred_element_type=F32)
        u = jnp.dot(xn, wu_ref[:, c0:c0 + FF_CHUNK], preferred_element_type=F32)
        a = ((h + h * jnp.tanh(h)) * u).astype(BF16)
        acc_ref[...] += jnp.dot(a, wd_ref[c0:c0 + FF_CHUNK, :], preferred_element_type=F32)
    return acc_ref[...]


def _ffn_in_tile(x_ref, ffn_refs, gm_ref, winl_ref, wina_ref, x1_ref, pl_ref, pa_ref, xn_ref, acc_ref):
    g1_ref, wg_ref, wu_ref, wd_ref = ffn_refs
    xn_ref[...] = _rms(x_ref[...], g1_ref[...]).astype(BF16)
    x1 = x_ref[...] + 0.5 * _ffn(xn_ref, wg_ref, wu_ref, wd_ref, acc_ref)
    x1_ref[...] = x1
    xn_ref[...] = _rms(x1, gm_ref[...]).astype(BF16)
    pl_ref[...] = jnp.dot(xn_ref[...], winl_ref[...], preferred_element_type=F32)
    pa_ref[...] = jnp.dot(xn_ref[...], wina_ref[...], preferred_element_type=F32)


def _lru_preact(xc, gate_w):
    wa_ref, ba_ref, wx_ref, bx_ref = gate_w
    xcb = xc.astype(BF16)
    c = xc.shape[-1]
    dt = min(GATE_DIAG_TILE, c)

    def blockdiag_dot(w_ref):
        return jnp.concatenate([jnp.dot(xcb[:, s:s + dt], w_ref[s:s + dt, s:s + dt], preferred_element_type=F32)
                                for s in range(0, c, dt)], axis=1)

    zr = blockdiag_dot(wa_ref) + ba_ref[...]
    zi = blockdiag_dot(wx_ref) + bx_ref[...]
    return zr, zi


def _lru_terms(xc, zr, zi, lam_ref):
    i = 0.5 * jnp.tanh(zi) + 0.5
    half_c = (-0.5 * LRU_C) * jax.nn.softplus(-lam_ref[...])
    log_a = half_c * jnp.tanh(zr) + half_c
    a = jnp.exp(log_a)
    w = -jnp.tanh(log_a) * (a * a + 1.0)
    root = jnp.where(w > 0.0, w * lax.rsqrt(w), 0.0)
    u = root * (i * xc)
    return a, u


def _lru_prep(p_ref, first, lru_refs, xh_ref, xc_ref, zr_ref, zi_ref, gate_ref, ctail_ref):
    cw_ref, cb_ref, wa_ref, ba_ref, wx_ref, bx_ref, _, _ = lru_refs
    tt = p_ref.shape[0]
    c = xc_ref.shape[1]
    hist = V7X_SUBLANES

    xh_ref[0:hist, :] = jnp.where(first, jnp.zeros((hist, c), F32), xh_ref[0:hist, :])
    x = p_ref[:, 0:c]
    xh_ref[hist:hist + tt, :] = x
    xg = xh_ref[...].reshape(tt // hist + 1, hist, c)
    sub = lax.broadcasted_iota(jnp.int32, (tt // hist, hist, c), 1)
    w = cw_ref[...]
    xc = cb_ref[...].reshape(1, 1, c)
    for j in range(CONV_W - 1):
        k = CONV_W - 1 - j
        rolled = pltpu.roll(xg, k, 1)
        xc = xc + jnp.where(sub >= k, rolled[1:], rolled[:-1]) * w[j:j + 1, :].reshape(1, 1, c)
    xc = (xc + xg[1:] * w[CONV_W - 1:CONV_W, :].reshape(1, 1, c)).reshape(tt, c)
    xh_ref[0:hist, :] = x[tt - hist:tt, :]
    ctail_ref[...] = x[tt - (CONV_W - 1):tt, :]

    xc_ref[...] = xc
    zr, zi = _lru_preact(xc, (wa_ref, ba_ref, wx_ref, bx_ref))
    zr_ref[...] = zr
    zi_ref[...] = zi
    gate_ref[...] = p_ref[:, c:2 * c]


def _lru_core(first, lru_refs, xc_ref, zr_ref, zi_ref, gate_ref, ctail_ref, out_ref, hl_ref, cs_ref, h_ref):
    lam_ref, on_ref = lru_refs[6], lru_refs[7]
    tt, c = xc_ref.shape

    a, u = _lru_terms(xc_ref[...], zr_ref[...], zi_ref[...], lam_ref)

    groups = tt // V7X_SUBLANES
    a3 = a.reshape(groups, V7X_SUBLANES, c)
    u3 = u.reshape(groups, V7X_SUBLANES, c)
    sub = lax.broadcasted_iota(jnp.int32, a3.shape, 1)
    k = 1
    while k < V7X_SUBLANES:
        m = sub >= k
        a_prev = pltpu.roll(a3, k, 1)
        u_prev = pltpu.roll(u3, k, 1)
        u3 = jnp.where(m, a3 * u_prev + u3, u3)
        a3 = jnp.where(m, a3 * a_prev, a3)
        k *= 2

    h = jnp.where(first, jnp.zeros((1, c), F32), h_ref[...])
    hs = []
    for g in range(groups):
        hr = a3[g] * h + u3[g]
        hs.append(hr)
        h = hr[V7X_SUBLANES - 1:V7X_SUBLANES, :]
    h_ref[...] = h

    y = jnp.concatenate(hs, axis=0) * jax.nn.gelu(gate_ref[...])
    out_ref[...] = _rms(y, on_ref[...]).astype(out_ref.dtype)
    hl_ref[0] = h
    cs_ref[0] = ctail_ref[...]


def _lru_sample_kernel(x_ref, cst_ref, h0_ref, cw_ref, cb_ref, wa_ref, ba_ref, wx_ref, bx_ref, lam_ref, on_ref,
                       out_ref, hl_ref, cs_ref):
    t = x_ref.shape[0]
    c = out_ref.shape[2]
    w = cw_ref[...]
    rows = [cst_ref[j] for j in range(CONV_W - 1)] + [x_ref[j, :, 0:c] for j in range(t)]
    h = h0_ref[...]
    for s in range(t):
        xc = cb_ref[...]
        for j in range(CONV_W):
            xc = xc + rows[s + j] * w[j:j + 1, :]
        zr, zi = _lru_preact(xc, (wa_ref, ba_ref, wx_ref, bx_ref))
        a, u = _lru_terms(xc, zr, zi, lam_ref)
        h = a * h + u
        y = h * jax.nn.gelu(x_ref[s, :, c:2 * c])
        out_ref[s] = _rms(y, on_ref[...]).astype(out_ref.dtype)
    hl_ref[...] = h
    for j in range(CONV_W - 1):
        cs_ref[j] = rows[t + j]


def _lru_sample(x_tm, cst_tm, h0, cw, cb, wa, ba, wx, bx, lam, on):
    t, s, w2 = x_tm.shape
    c = w2 // 2
    return pl.pallas_call(
        _lru_sample_kernel,
        out_shape=[jax.ShapeDtypeStruct((t, s, c), BF16), jax.ShapeDtypeStruct((s, c), F32),
                   jax.ShapeDtypeStruct((CONV_W - 1, s, c), F32)],
        compiler_params=pltpu.CompilerParams(vmem_limit_bytes=_vmem_limit(SMALL_KERNEL_VMEM_BYTES)),
        name="lru_sample",
    )(x_tm, cst_tm, h0, cw, cb, wa, ba, wx, bx, lam, on)


def _head_norm_rope(xc, gain, cos, sin, ones_ref):
    x2 = xc * xc
    hi = x2.astype(BF16)
    lo = (x2 - hi.astype(F32)).astype(BF16)
    ss = jnp.dot(hi, ones_ref[...], preferred_element_type=F32) + jnp.dot(lo, ones_ref[...], preferred_element_type=F32)
    xn = (xc * lax.rsqrt(ss + NORM_EPS)) * gain
    lane = lax.broadcasted_iota(jnp.int32, xn.shape, xn.ndim - 1)
    first_half = (lane & (HEAD_DIM // 2)) == 0
    ax = xn.ndim - 1
    swapped = jnp.where(first_half, pltpu.roll(xn, V7X_LANES - HEAD_DIM // 2, ax), pltpu.roll(xn, HEAD_DIM // 2, ax))
    return xn * cos + swapped * sin


def _group_lhs(qcols, g, lane_lo, axis):
    zero = jnp.zeros_like(qcols[0])
    parts = [jnp.where(lane_lo, qcols[2 * g], zero), jnp.where(lane_lo, qcols[2 * g + 1], zero),
             jnp.where(lane_lo, zero, qcols[2 * g]), jnp.where(lane_lo, zero, qcols[2 * g + 1])]
    return jnp.concatenate(parts, axis=axis).astype(BF16)


def _attn_prep(pa_ref, cos_ref, sin_ref, attn_refs, qh_ref, k2_ref, vt_ref, kvl_ref):
    qn_ref, kn_ref, _, ones_ref, _ = attn_refs
    tq = pa_ref.shape[0]
    blk = WINDOW
    lanes = V7X_LANES
    ncol = qh_ref.shape[0] // 2
    aw = ncol * lanes

    cos = cos_ref[...]
    sin = sin_ref[...]
    cols = [pa_ref[:, cc * lanes:(cc + 1) * lanes] for cc in range(ncol + 1)]
    sq = [x * x for x in cols]
    hi = [x.astype(BF16) for x in sq]
    lo = [(x - h.astype(F32)).astype(BF16) for x, h in zip(sq, hi)]
    ss = (jnp.dot(jnp.concatenate(hi, axis=0), ones_ref[...], preferred_element_type=F32)
          + jnp.dot(jnp.concatenate(lo, axis=0), ones_ref[...], preferred_element_type=F32))
    lane = lax.broadcasted_iota(jnp.int32, (tq, lanes), 1)
    first_half = (lane & (HEAD_DIM // 2)) == 0
    lane_lo = lane < HEAD_DIM
    rot = []
    for cc, x in enumerate(cols):
        gain = qn_ref[...] if cc < ncol else kn_ref[...]
        xn = (x * lax.rsqrt(ss[cc * tq:(cc + 1) * tq] + NORM_EPS)) * gain
        swapped = jnp.where(first_half, pltpu.roll(xn, lanes - HEAD_DIM // 2, 1), pltpu.roll(xn, HEAD_DIM // 2, 1))
        rot.append(xn * cos + swapped * sin)
    k = rot[ncol]
    v = pa_ref[:, aw + lanes:aw + 2 * lanes]

    zero = jnp.zeros((tq, lanes), F32)
    for cc in range(ncol):
        qh_ref[2 * cc] = jnp.where(lane_lo, rot[cc], zero).astype(BF16)
        qh_ref[2 * cc + 1] = jnp.where(lane_lo, zero, rot[cc]).astype(BF16)

    k_sw = pltpu.roll(k, HEAD_DIM, 1)
    k2_ref[0, blk:blk + tq, :] = jnp.where(lane_lo, k, k_sw).astype(BF16)
    k2_ref[1, blk:blk + tq, :] = jnp.where(lane_lo, k_sw, k).astype(BF16)
    for jb in range(tq // blk):
        vt_ref[:, (jb + 1) * blk:(jb + 2) * blk] = v[jb * blk:(jb + 1) * blk].T.astype(BF16)
    kvl_ref[0] = k[tq - blk:tq]
    kvl_ref[1] = v[tq - blk:tq]


def _attn_core(first, attn_refs, out_ref, pk_ref, pv_ref, qh_ref, k2_ref, vt_ref, kvl_ref, st_ref, pt_ref, rd_ref):
    _, _, sk_ref, _, on_ref = attn_refs
    tq = qh_ref.shape[1]
    blk = WINDOW
    nblk = tq // blk
    heads_per_group = qh_ref.shape[0] // N_KV_HEADS
    pairs = [(jb, g) for jb in range(nblk) for g in range(N_KV_HEADS)]

    for idx, (jb, g) in enumerate(pairs):
        r0 = jb * blk
        lhs = jnp.concatenate([qh_ref[h, r0:r0 + blk, :]
                               for h in range(g * heads_per_group, (g + 1) * heads_per_group)], axis=0)
        st_ref[idx] = lax.dot_general(k2_ref[g, r0:r0 + 2 * blk, :], lhs, (((1,), (1,)), ((), ())),
                                      preferred_element_type=F32)

    ncolq = heads_per_group * blk
    kj = lax.broadcasted_iota(jnp.int32, (blk, ncolq), 0)
    qi = lax.broadcasted_iota(jnp.int32, (blk, ncolq), 1) % blk
    tri = kj <= qi
    for idx, (jb, g) in enumerate(pairs):
        s_prev = st_ref[idx, 0:blk, :]
        if jb == 0:
            s_prev = s_prev + jnp.where(first, jnp.float32(-jnp.inf), jnp.float32(0.0))
        m = jnp.where(tri, st_ref[idx, blk:2 * blk, :], s_prev)
        sk = sk_ref[g:g + 1, :]
        mx = jnp.maximum(jnp.max(m, axis=0, keepdims=True), sk)
        p = jnp.exp2(m - mx)
        den = jnp.sum(p, axis=0, keepdims=True) + jnp.exp2(sk - mx)
        pz = jnp.zeros_like(p)
        pt_ref[idx] = jnp.concatenate([jnp.where(tri, pz, p), jnp.where(tri, p, pz)], axis=0).astype(BF16)
        rd_ref[idx] = 1.0 / den

    for jb in range(nblk):
        r0 = jb * blk
        ocols = []
        for g in range(N_KV_HEADS):
            idx = jb * N_KV_HEADS + g
            vt_g = vt_ref[g * HEAD_DIM:(g + 1) * HEAD_DIM, r0:r0 + 2 * blk]
            og = jnp.dot(vt_g, pt_ref[idx], preferred_element_type=F32) * rd_ref[idx]
            for half in range(heads_per_group // 2):
                pair = jnp.concatenate([og[:, (2 * half) * blk:(2 * half + 1) * blk],
                                        og[:, (2 * half + 1) * blk:(2 * half + 2) * blk]], axis=0)
                ocols.append(pair.T)
        o = jnp.concatenate(ocols, axis=1)
        out_ref[r0:r0 + blk, :] = _rms(o, on_ref[...]).astype(out_ref.dtype)

    k2_ref[:, 0:blk, :] = k2_ref[:, tq:tq + blk, :]
    vt_ref[:, 0:blk] = vt_ref[:, tq:tq + blk]
    pk_ref[0] = kvl_ref[0]
    pv_ref[0] = kvl_ref[1]


N_FFN_IN = 7
N_LRU = 8
N_ATTN = 5


def _mixer_in_kernel(*refs, tiles_per_seq, ntile):
    x_ref, cos_ref, sin_ref = refs[0:3]
    p0 = 3
    g1_ref, wg_ref, wu_ref, wd_ref, gm_ref, winl_ref, wina_ref = refs[p0:p0 + N_FFN_IN]
    lru_refs = refs[p0 + N_FFN_IN:p0 + N_FFN_IN + N_LRU]
    attn_refs = refs[p0 + N_FFN_IN + N_LRU:p0 + N_FFN_IN + N_LRU + N_ATTN]
    o0 = p0 + N_FFN_IN + N_LRU + N_ATTN
    x1_ref, lo_ref, ao_ref, hl_ref, cs_ref, pk_ref, pv_ref = refs[o0:o0 + 7]
    (xn_ref, acc_ref, pl_ref, pa_ref, xh_ref, h_ref, xc_ref, zr_ref, zi_ref, gate_ref, ctail_ref,
     qh_ref, k2_ref, vt_ref, kvl_ref, st_ref, pt_ref, rd_ref) = refs[o0 + 7:]

    i = pl.program_id(0)

    @pl.when(i == 0)
    def _():
        for ref in (pl_ref, pa_ref, xh_ref, h_ref, xc_ref, zr_ref, zi_ref, gate_ref, ctail_ref,
                    qh_ref, k2_ref, vt_ref, kvl_ref):
            ref[...] = jnp.zeros_like(ref)

    def step(with_ffn):
        first = lax.rem(jnp.maximum(i - 1, 0), tiles_per_seq) == 0
        _lru_prep(pl_ref, first, lru_refs, xh_ref, xc_ref, zr_ref, zi_ref, gate_ref, ctail_ref)
        _attn_prep(pa_ref, cos_ref, sin_ref, attn_refs, qh_ref, k2_ref, vt_ref, kvl_ref)
        _lru_core(first, lru_refs, xc_ref, zr_ref, zi_ref, gate_ref, ctail_ref, lo_ref, hl_ref, cs_ref, h_ref)
        _attn_core(first, attn_refs, ao_ref, pk_ref, pv_ref, qh_ref, k2_ref, vt_ref, kvl_ref, st_ref, pt_ref, rd_ref)
        if with_ffn:
            _ffn_in_tile(x_ref, (g1_ref, wg_ref, wu_ref, wd_ref), gm_ref, winl_ref, wina_ref,
                         x1_ref, pl_ref, pa_ref, xn_ref, acc_ref)

    pl.when(i < ntile)(functools.partial(step, True))
    pl.when(i >= ntile)(functools.partial(step, False))


def _mixer_in(x, seq_len, cos, sin, ffn_in_p, lru_p, attn_p):
    n, d = x.shape
    tm = TOKEN_TILE
    assert seq_len % tm == 0 and n % seq_len == 0 and tm % WINDOW == 0
    nseq = n // seq_len
    tps = seq_len // tm
    ntile = n // tm
    wg, winl, wina = ffn_in_p[1], ffn_in_p[5], ffn_in_p[6]
    w2, wa = winl.shape[1], wina.shape[1]
    c = w2 // 2
    aw = attn_p[4].shape[1]
    kvw = N_KV_HEADS * HEAD_DIM
    npair = tm // WINDOW * N_KV_HEADS
    heads_per_group = aw // HEAD_DIM // N_KV_HEADS
    cur = lambda i: jnp.minimum(i, ntile - 1)
    mix = lambda i: jnp.maximum(i - 1, 0)
    params = list(ffn_in_p) + list(lru_p) + list(attn_p)
    weights = sum(p.size * p.dtype.itemsize for p in params)
    tiles = (2 * 4 * tm * 2 * d + 2 * 2 * tm * (c + aw) + 4 * tm * (w2 + wa) + tm * d * (2 + 4) + 6 * 4 * tm * c
             + npair * 2 * WINDOW * heads_per_group * WINDOW * (4 + 2))
    return pl.pallas_call(
        functools.partial(_mixer_in_kernel, tiles_per_seq=tps, ntile=ntile),
        grid=(ntile + 1,),
        in_specs=[pl.BlockSpec((tm, d), lambda i: (cur(i), 0)),
                  pl.BlockSpec((tm, V7X_LANES), lambda i: (lax.rem(mix(i), tps), 0)),
                  pl.BlockSpec((tm, V7X_LANES), lambda i: (lax.rem(mix(i), tps), 0))]
                 + [_resident(p.shape) for p in params],
        out_specs=[pl.BlockSpec((tm, d), lambda i: (cur(i), 0)),
                   pl.BlockSpec((tm, c), lambda i: (mix(i), 0)),
                   pl.BlockSpec((tm, aw), lambda i: (mix(i), 0)),
                   pl.BlockSpec((1, 1, c), lambda i: (mix(i) // tps, 0, 0)),
                   pl.BlockSpec((1, CONV_W - 1, c), lambda i: (mix(i) // tps, 0, 0)),
                   pl.BlockSpec((1, WINDOW, kvw), lambda i: (mix(i) // tps, 0, 0)),
                   pl.BlockSpec((1, WINDOW, kvw), lambda i: (mix(i) // tps, 0, 0))],
        out_shape=[jax.ShapeDtypeStruct((n, d), F32), jax.ShapeDtypeStruct((n, c), BF16),
                   jax.ShapeDtypeStruct((n, aw), BF16), jax.ShapeDtypeStruct((nseq, 1, c), F32),
                   jax.ShapeDtypeStruct((nseq, CONV_W - 1, c), F32), jax.ShapeDtypeStruct((nseq, WINDOW, kvw), F32),
                   jax.ShapeDtypeStruct((nseq, WINDOW, kvw), F32)],
        scratch_shapes=[pltpu.VMEM((tm, d), BF16), pltpu.VMEM((tm, d), F32),
                        pltpu.VMEM((tm, w2), F32), pltpu.VMEM((tm, wa), F32),
                        pltpu.VMEM((tm + V7X_SUBLANES, c), F32), pltpu.VMEM((1, c), F32),
                        pltpu.VMEM((tm, c), F32), pltpu.VMEM((tm, c), F32), pltpu.VMEM((tm, c), F32),
                        pltpu.VMEM((tm, c), F32), pltpu.VMEM((CONV_W - 1, c), F32),
                        pltpu.VMEM((2 * aw // V7X_LANES, tm, V7X_LANES), BF16),
                        pltpu.VMEM((N_KV_HEADS, WINDOW + tm, V7X_LANES), BF16),
                        pltpu.VMEM((V7X_LANES, WINDOW + tm), BF16),
                        pltpu.VMEM((2, WINDOW, kvw), F32),
                        pltpu.VMEM((npair, 2 * WINDOW, heads_per_group * WINDOW), F32),
                        pltpu.VMEM((npair, 2 * WINDOW, heads_per_group * WINDOW), BF16),
                        pltpu.VMEM((npair, 1, heads_per_group * WINDOW), F32)],
        compiler_params=pltpu.CompilerParams(dimension_semantics=("arbitrary",),
                                             vmem_limit_bytes=_vmem_limit(weights + tiles + MIXER_TEMP_BYTES)),
        name="mixer_in",
    )(x, cos, sin, *params)


def _ffn_in_kernel(x_ref, g1_ref, wg_ref, wu_ref, wd_ref, gm_ref, winl_ref, wina_ref,
                   x1_ref, pl_ref, pa_ref, xn_ref, acc_ref):
    _ffn_in_tile(x_ref, (g1_ref, wg_ref, wu_ref, wd_ref), gm_ref, winl_ref, wina_ref,
                 x1_ref, pl_ref, pa_ref, xn_ref, acc_ref)


def _ffn_in(x, g1, wg, wu, wd, gm, winl, wina):
    n, d = x.shape
    tm = min(TOKEN_TILE, n)
    wl, wa = winl.shape[1], wina.shape[1]
    row = lambda w: pl.BlockSpec((tm, w), lambda i: (i, 0))
    weights = 2 * (wg.size + wu.size + wd.size + winl.size + wina.size)
    tiles = 2 * 4 * tm * (2 * d + wl + wa) + tm * d * (2 + 4) + 4 * 4 * tm * max(FF_CHUNK, wl)
    return pl.pallas_call(
        _ffn_in_kernel,
        grid=(n // tm,),
        in_specs=[row(d), _resident(g1.shape), _resident(wg.shape), _resident(wu.shape), _resident(wd.shape),
                  _resident(gm.shape), _resident(winl.shape), _resident(wina.shape)],
        out_specs=[row(d), row(wl), row(wa)],
        out_shape=[jax.ShapeDtypeStruct((n, d), F32), jax.ShapeDtypeStruct((n, wl), F32),
                   jax.ShapeDtypeStruct((n, wa), F32)],
        scratch_shapes=[pltpu.VMEM((tm, d), BF16), pltpu.VMEM((tm, d), F32)],
        compiler_params=pltpu.CompilerParams(dimension_semantics=("arbitrary",),
                                             vmem_limit_bytes=_vmem_limit(weights + tiles + FFN_TEMP_BYTES)),
        name="ffn_in",
    )(x, g1, wg, wu, wd, gm, winl, wina)


def _out_ffn_kernel(lo_ref, ao_ref, x1_ref, wol_ref, woa_ref, g2_ref, wg_ref, wu_ref, wd_ref,
                    out_ref, xn_ref, acc_ref):
    y = jnp.dot(lo_ref[...], wol_ref[...], preferred_element_type=F32)
    y = y + jnp.dot(ao_ref[...], woa_ref[...], preferred_element_type=F32)
    x2 = x1_ref[...] + y
    out_ref[...] = x2
    xn_ref[...] = _rms(x2, g2_ref[...]).astype(BF16)
    out_ref[...] = out_ref[...] + 0.5 * _ffn(xn_ref, wg_ref, wu_ref, wd_ref, acc_ref)


def _out_ffn(lo, ao, x1, wol, woa, g2, wg, wu, wd):
    n, d = x1.shape
    tm = OUT_FFN_TILE if n % OUT_FFN_TILE == 0 else min(TOKEN_TILE, n)
    row = lambda w: pl.BlockSpec((tm, w), lambda i: (i, 0))
    weights = 2 * (wg.size + wu.size + wd.size + wol.size + woa.size)
    tiles = 2 * tm * (2 * lo.shape[1] + 2 * ao.shape[1] + 8 * d) + tm * d * (2 + 4) + 4 * 4 * tm * d
    return pl.pallas_call(
        _out_ffn_kernel,
        grid=(n // tm,),
        in_specs=[row(lo.shape[1]), row(ao.shape[1]), row(d), _resident(wol.shape), _resident(woa.shape),
                  _resident(g2.shape), _resident(wg.shape), _resident(wu.shape), _resident(wd.shape)],
        out_specs=row(d),
        out_shape=jax.ShapeDtypeStruct((n, d), F32),
        scratch_shapes=[pltpu.VMEM((tm, d), BF16), pltpu.VMEM((tm, d), F32)],
        compiler_params=pltpu.CompilerParams(dimension_semantics=("arbitrary",),
                                             vmem_limit_bytes=_vmem_limit(weights + tiles + FFN_TEMP_BYTES)),
        name="out_ffn",
    )(lo, ao, x1, wol, woa, g2, wg, wu, wd)


def _qk_prep_kernel(pa_ref, cos_ref, sin_ref, qn_ref, kn_ref, ones_ref, q_ref, k_ref):
    lanes = V7X_LANES
    aw = q_ref.shape[1]
    cos = cos_ref[...]
    sin = sin_ref[...]
    for cc in range(aw // lanes):
        q_ref[:, cc * lanes:(cc + 1) * lanes] = _head_norm_rope(
            pa_ref[:, cc * lanes:(cc + 1) * lanes], qn_ref[...], cos, sin, ones_ref) * (HEAD_DIM ** -0.5)
    k_ref[...] = _head_norm_rope(pa_ref[:, aw:aw + lanes], kn_ref[...], cos, sin, ones_ref)


def _qk_prep(pa, cos, sin, qn, kn, ones, aw):
    n = pa.shape[0]
    return pl.pallas_call(
        _qk_prep_kernel,
        out_shape=[jax.ShapeDtypeStruct((n, aw), F32), jax.ShapeDtypeStruct((n, N_KV_HEADS * HEAD_DIM), F32)],
        compiler_params=pltpu.CompilerParams(vmem_limit_bytes=_vmem_limit(SMALL_KERNEL_VMEM_BYTES)),
        name="qk_prep_sample",
    )(pa, cos, sin, qn, kn, ones)


def _attn_cached_kernel(q_ref, kn_ref, vn_ref, ck_ref, cv_ref, sk_ref, on_ref,
                        out_ref, nk_ref, nv_ref, kk_ref, vv_ref):
    sb, t, aw = q_ref.shape
    w = ck_ref.shape[1]
    lanes = V7X_LANES
    jpad = kk_ref.shape[1]

    kk_ref[:, 0:w, :] = ck_ref[...]
    vv_ref[:, 0:w, :] = cv_ref[...]
    kk_ref[:, w:w + t, :] = kn_ref[...]
    vv_ref[:, w:w + t, :] = vn_ref[...]
    kk_ref[:, w + t:jpad, :] = jnp.zeros((sb, jpad - w - t, lanes), F32)
    vv_ref[:, w + t:jpad, :] = jnp.zeros((sb, jpad - w - t, lanes), F32)
    nk_ref[...] = kk_ref[:, t:t + w, :]
    nv_ref[...] = vv_ref[:, t:t + w, :]

    kk = kk_ref[...]
    vv = vv_ref[...]
    k_sw = pltpu.roll(kk, HEAD_DIM, 2)
    v_sw = pltpu.roll(vv, HEAD_DIM, 2)
    lane_k = lax.broadcasted_iota(jnp.int32, kk.shape, 2) < HEAD_DIM
    lane_q = lax.broadcasted_iota(jnp.int32, (sb, t, lanes), 2) < HEAD_DIM
    k2 = (jnp.where(lane_k, kk, k_sw).astype(BF16), jnp.where(lane_k, k_sw, kk).astype(BF16))
    v2 = (vv.astype(BF16), v_sw.astype(BF16))

    qcols = [q_ref[:, :, cc * lanes:(cc + 1) * lanes] for cc in range(aw // lanes)]
    tq = lax.broadcasted_iota(jnp.int32, (sb, 4 * t, jpad), 1) % t
    j = lax.broadcasted_iota(jnp.int32, (sb, 4 * t, jpad), 2)
    valid = jnp.logical_and(j > tq, j <= tq + w)
    neg_inf = jnp.float32(-jnp.inf)

    ocols = []
    for g in range(N_KV_HEADS):
        lhs = _group_lhs(qcols, g, lane_q, 1)
        s = jnp.einsum('bqd,bjd->bqj', lhs, k2[g], preferred_element_type=F32)
        s = jnp.where(valid, s, neg_inf)
        heads = (4 * g, 4 * g + 2, 4 * g + 1, 4 * g + 3)
        sk = jnp.concatenate([jnp.broadcast_to(sk_ref[h:h + 1, 0:1].reshape(1, 1, 1), (sb, t, 1)) for h in heads],
                             axis=1)
        mx = jnp.maximum(jnp.max(s, axis=-1, keepdims=True), sk)
        p = jnp.exp(s - mx)
        den = jnp.sum(p, axis=-1, keepdims=True) + jnp.exp(sk - mx)
        pb = p.astype(BF16)
        o_even = jnp.einsum('bqj,bjd->bqd', pb[:, 0:2 * t], v2[g], preferred_element_type=F32) / den[:, 0:2 * t]
        o_odd = jnp.einsum('bqj,bjd->bqd', pb[:, 2 * t:4 * t], v2[1 - g], preferred_element_type=F32) / den[:, 2 * t:4 * t]
        for half in range(2):
            ocols.append(jnp.where(lane_q, o_even[:, half * t:(half + 1) * t], o_odd[:, half * t:(half + 1) * t]))
    o = jnp.concatenate(ocols, axis=2)
    out_ref[...] = _rms(o, on_ref[...].reshape(1, 1, aw)).astype(out_ref.dtype)


def _attn_cached(q3, kn3, vn3, ck, cv, sk, on):
    s, t, aw = q3.shape
    w, kvw = ck.shape[1], ck.shape[2]
    sb = min(SAMPLE_SEQ_TILE, s)
    jpad = -(-(w + t) // V7X_SUBLANES) * V7X_SUBLANES
    seq = lambda a, c: pl.BlockSpec((sb, a, c), lambda i: (i, 0, 0))
    return pl.pallas_call(
        _attn_cached_kernel,
        grid=(s // sb,),
        in_specs=[seq(t, aw), seq(t, kvw), seq(t, kvw), seq(w, kvw), seq(w, kvw), _resident(sk.shape), _resident(on.shape)],
        out_specs=[seq(t, aw), seq(w, kvw), seq(w, kvw)],
        out_shape=[jax.ShapeDtypeStruct((s, t, aw), BF16), jax.ShapeDtypeStruct((s, w, kvw), F32),
                   jax.ShapeDtypeStruct((s, w, kvw), F32)],
        scratch_shapes=[pltpu.VMEM((sb, jpad, kvw), F32), pltpu.VMEM((sb, jpad, kvw), F32)],
        compiler_params=pltpu.CompilerParams(dimension_semantics=("arbitrary",),
                                             vmem_limit_bytes=_vmem_limit(SMALL_KERNEL_VMEM_BYTES)),
        name="attn_sample",
    )(q3, kn3, vn3, ck, cv, sk, on)


ROPE_SPLIT = 128


def _rope_tables(start, n):
    half = HEAD_DIM // 2
    lane = jnp.arange(V7X_LANES, dtype=jnp.int32)
    inv_freq = ROPE_THETA ** (-(lane % half).astype(F32) / half)
    sign = jnp.where((lane % HEAD_DIM) < half, -1.0, 1.0).astype(F32)
    if n % ROPE_SPLIT:
        pos = start + jnp.arange(n, dtype=jnp.int32)
        ang = pos.astype(F32)[:, None] * inv_freq[None, :]
        return jnp.cos(ang), jnp.sin(ang) * sign[None, :]
    lo = jnp.arange(ROPE_SPLIT, dtype=jnp.int32)
    hi = start + ROPE_SPLIT * jnp.arange(n // ROPE_SPLIT, dtype=jnp.int32)
    a_hi = hi.astype(F32)[:, None] * inv_freq[None, :]
    a_lo = lo.astype(F32)[:, None] * inv_freq[None, :]
    ch, sh = jnp.cos(a_hi)[:, None, :], jnp.sin(a_hi)[:, None, :]
    cl, sl = jnp.cos(a_lo)[None, :, :], jnp.sin(a_lo)[None, :, :]
    cos = (ch * cl - sh * sl).reshape(n, V7X_LANES)
    sin = (sh * cl + ch * sl).reshape(n, V7X_LANES)
    return cos, sin * sign[None, :]


def _block_diag(w):
    n, c, d = w.shape
    eye = jnp.eye(n, dtype=w.dtype)
    return (w[:, :, None, :] * eye[:, None, :, None]).reshape(n * c, n * d)


def _row(v):
    return v.reshape(1, -1).astype(F32)


def kernel(x_prompt, x_sample, state_lru_h, state_conv, cache_k, cache_v, ffn1_norm, ffn1_w_gate, ffn1_w_up, ffn1_w_down, mix_norm, w_in, conv_w, conv_b, lru_wa, lru_ba, lru_wx, lru_bx, lru_lambda, q_norm, k_norm, attn_sinks, lru_out_norm, attn_out_norm, w_out, ffn2_norm, ffn2_w_gate, ffn2_w_up, ffn2_w_down):
    depth = w_in.shape[0]
    b, t, d = x_prompt.shape
    s, ts, _ = x_sample.shape
    c = lru_lambda.shape[1]
    aw = attn_out_norm.shape[1]
    kvw = N_KV_HEADS * HEAD_DIM
    win = cache_k.shape[2]
    assert t >= WINDOW and win == WINDOW
    assert GATE_DIAG_TILE % lru_wa.shape[2] == 0 and c % min(GATE_DIAG_TILE, c) == 0

    cos_p, sin_p = _rope_tables(0, t)
    cos_s, sin_s = _rope_tables(PAST_LEN, ts)
    cos_s, sin_s = jnp.tile(cos_s, (s, 1)), jnp.tile(sin_s, (s, 1))
    ones = _block_diag(jnp.full((V7X_LANES // HEAD_DIM, HEAD_DIM, HEAD_DIM), 1.0 / HEAD_DIM, BF16))
    log2e = 1.0 / jnp.log(jnp.float32(2.0))

    yp = x_prompt.reshape(b * t, d)
    ys = x_sample.reshape(s * ts, d)
    outs = [[] for _ in range(8)]
    for l in range(depth):
        ffn1 = (_row(ffn1_norm[l]), (0.5 * ffn1_w_gate[l]).astype(BF16), ffn1_w_up[l].astype(BF16),
                ffn1_w_down[l].astype(BF16))
        ffn2 = (_row(ffn2_norm[l]), (0.5 * ffn2_w_gate[l]).astype(BF16), ffn2_w_up[l].astype(BF16),
                ffn2_w_down[l].astype(BF16))
        win_b = w_in[l].astype(BF16)
        ffn_in_p = ffn1 + (_row(mix_norm[l]), win_b[:, :2 * c], win_b[:, 2 * c:])
        wout_b = w_out[l].astype(BF16)
        wol, woa = wout_b[:c], wout_b[c:]
        lru_p = (conv_w[l], _row(conv_b[l]), (0.5 * _block_diag(lru_wa[l])).astype(BF16), _row(0.5 * lru_ba[l]),
                 (0.5 * _block_diag(lru_wx[l])).astype(BF16), _row(0.5 * lru_bx[l]), _row(lru_lambda[l]),
                 _row(lru_out_norm[l]))
        reps = V7X_LANES // HEAD_DIM
        qn = _row(jnp.tile(q_norm[l], reps))
        kn = _row(jnp.tile(k_norm[l], reps))
        sk = jnp.broadcast_to(attn_sinks[l].astype(F32)[:, None], (attn_sinks.shape[1], V7X_LANES))
        qn_p = qn * (HEAD_DIM ** -0.5 * log2e)
        sk_rows = jnp.repeat(attn_sinks[l].astype(F32).reshape(N_KV_HEADS, -1), WINDOW, axis=1) * log2e
        a_on = _row(attn_out_norm[l])

        x1, lo, ao, hl, cs, pk, pv = _mixer_in(yp, t, cos_p, sin_p, ffn_in_p, lru_p, (qn_p, kn, sk_rows, ones, a_on))
        yp = _out_ffn(lo, ao, x1, wol, woa, *ffn2)
        outs[0].append(hl.reshape(b, c))
        outs[1].append(cs)
        outs[2].append(pk.reshape(b, WINDOW, N_KV_HEADS, HEAD_DIM))
        outs[3].append(pv.reshape(b, WINDOW, N_KV_HEADS, HEAD_DIM))

        x1s, pl_s, pa_s = _ffn_in(ys, *ffn_in_p)
        x_tm = pl_s.reshape(s, ts, 2 * c).transpose(1, 0, 2)
        cst_tm = state_conv[l].transpose(1, 0, 2)
        lo_tm, hls, cs_tm = _lru_sample(x_tm, cst_tm, state_lru_h[l], *lru_p)
        q_s, k_s = _qk_prep(pa_s, cos_s, sin_s, qn, kn, ones, aw)
        v_s = pa_s[:, aw + kvw:]
        aos, nk, nv = _attn_cached(q_s.reshape(s, ts, aw), k_s.reshape(s, ts, kvw), v_s.reshape(s, ts, kvw),
                                   cache_k[l].reshape(s, win, kvw), cache_v[l].reshape(s, win, kvw), sk, a_on)
        ys = _out_ffn(lo_tm.transpose(1, 0, 2).reshape(s * ts, c), aos.reshape(s * ts, aw), x1s, wol, woa, *ffn2)
        outs[4].append(hls)
        outs[5].append(cs_tm.transpose(1, 0, 2))
        outs[6].append(nk.reshape(s, win, N_KV_HEADS, HEAD_DIM))
        outs[7].append(nv.reshape(s, win, N_KV_HEADS, HEAD_DIM))

    st = [jnp.stack(o) for o in outs]
    return (yp.reshape(b, t, d), ys.reshape(s, ts, d), st[0], st[1], st[2], st[3], st[4], st[5], st[6], st[7])
```

```python
import functools

import jax
import jax.numpy as jnp
from jax import lax
from jax.experimental import pallas as pl
from jax.experimental.pallas import tpu as pltpu

F32 = jnp.float32
BF16 = jnp.bfloat16

NORM_EPS = 1e-6
LRU_C = 8.0
CONV_W = 4
HEAD_DIM = 64
N_KV_HEADS = 2
WINDOW = 128
ROPE_THETA = 10000.0
PAST_LEN = 16384

V7X_LANES = 128
V7X_SUBLANES = 8
V7X_MXU_DIM = 256
V7X_VMEM_BYTES = 64 * 1024 * 1024

TOKEN_TILE = 512
OUT_FFN_TILE = 1024
SAMPLE_SEQ_TILE = 32
FF_CHUNK = V7X_MXU_DIM
GATE_DIAG_TILE = V7X_MXU_DIM


MIB = 1024 * 1024
VMEM_KEEP_FREE_BYTES = 4 * MIB
FFN_TEMP_BYTES = 8 * MIB
MIXER_TEMP_BYTES = 20 * MIB
SMALL_KERNEL_VMEM_BYTES = 40 * MIB


def _vmem_limit(nbytes):
    return int(min(nbytes, V7X_VMEM_BYTES - VMEM_KEEP_FREE_BYTES))


def _resident(shape):
    nd = len(shape)
    return pl.BlockSpec(shape, lambda *_: (0,) * nd, pipeline_mode=pl.Buffered(1))


def _rms(x, g):
    ms = jnp.mean(x * x, axis=-1, keepdims=True)
    return (x * lax.rsqrt(ms + NORM_EPS)) * g


def _ffn(xn_ref, wg_ref, wu_ref, wd_ref, acc_ref):
    acc_ref[...] = jnp.zeros_like(acc_ref)
    for c0 in range(0, wg_ref.shape[1], FF_CHUNK):
        xn = xn_ref[...]
        h = jnp.dot(xn, wg_ref[:, c0:c0 + FF_CHUNK], preferred_element_type=F32)
        u = jnp.dot(xn, wu_ref[:, c0:c0 + FF_CHUNK], preferred_element_type=F32)
        a = ((h + h * jnp.tanh(h)) * u).astype(BF16)
        acc_ref[...] += jnp.dot(a, wd_ref[c0:c0 + FF_CHUNK, :], preferred_element_type=F32)
    return acc_ref[...]


def _ffn_in_tile(x_ref, ffn_refs, gm_ref, winl_ref, wina_ref, x1_ref, pl_ref, pa_ref, xn_ref, acc_ref):
    g1_ref, wg_ref, wu_ref, wd_ref = ffn_refs
    xn_ref[...] = _rms(x_ref[...], g1_ref[...]).astype(BF16)
    x1 = x_ref[...] + 0.5 * _ffn(xn_ref, wg_ref, wu_ref, wd_ref, acc_ref)
    x1_ref[...] = x1
    xn_ref[...] = _rms(x1, gm_ref[...]).astype(BF16)
    pl_ref[...] = jnp.dot(xn_ref[...], winl_ref[...], preferred_element_type=F32)
    pa_ref[...] = jnp.dot(xn_ref[...], wina_ref[...], preferred_element_type=F32)


def _lru_preact(xc, gate_w):
    wa_ref, ba_ref, wx_ref, bx_ref = gate_w
    xcb = xc.astype(BF16)
    c = xc.shape[-1]
    dt = min(GATE_DIAG_TILE, c)

    def blockdiag_dot(w_ref):
        return jnp.concatenate([jnp.dot(xcb[:, s:s + dt], w_ref[s:s + dt, s:s + dt], preferred_element_type=F32)
                                for s in range(0, c, dt)], axis=1)

    zr = blockdiag_dot(wa_ref) + ba_ref[...]
    zi = blockdiag_dot(wx_ref) + bx_ref[...]
    return zr, zi


def _lru_terms(xc, zr, zi, lam_ref):
    i = 0.5 * jnp.tanh(zi) + 0.5
    half_c = (-0.5 * LRU_C) * jax.nn.softplus(-lam_ref[...])
    log_a = half_c * jnp.tanh(zr) + half_c
    a = jnp.exp(log_a)
    w = -jnp.tanh(log_a) * (a * a + 1.0)
    root = jnp.where(w > 0.0, w * lax.rsqrt(w), 0.0)
    u = root * (i * xc)
    return a, u


def _lru_prep(p_ref, first, lru_refs, xh_ref, xc_ref, zr_ref, zi_ref, gate_ref, ctail_ref):
    cw_ref, cb_ref, wa_ref, ba_ref, wx_ref, bx_ref, _, _ = lru_refs
    tt = p_ref.shape[0]
    c = xc_ref.shape[1]
    hist = V7X_SUBLANES

    xh_ref[0:hist, :] = jnp.where(first, jnp.zeros((hist, c), F32), xh_ref[0:hist, :])
    x = p_ref[:, 0:c]
    xh_ref[hist:hist + tt, :] = x
    xg = xh_ref[...].reshape(tt // hist + 1, hist, c)
    sub = lax.broadcasted_iota(jnp.int32, (tt // hist, hist, c), 1)
    w = cw_ref[...]
    xc = cb_ref[...].reshape(1, 1, c)
    for j in range(CONV_W - 1):
        k = CONV_W - 1 - j
        rolled = pltpu.roll(xg, k, 1)
        xc = xc + jnp.where(sub >= k, rolled[1:], rolled[:-1]) * w[j:j + 1, :].reshape(1, 1, c)
    xc = (xc + xg[1:] * w[CONV_W - 1:CONV_W, :].reshape(1, 1, c)).reshape(tt, c)
    xh_ref[0:hist, :] = x[tt - hist:tt, :]
    ctail_ref[...] = x[tt - (CONV_W - 1):tt, :]

    xc_ref[...] = xc
    zr, zi = _lru_preact(xc, (wa_ref, ba_ref, wx_ref, bx_ref))
    zr_ref[...] = zr
    zi_ref[...] = zi
    gate_ref[...] = p_ref[:, c:2 * c]


def _lru_core(first, lru_refs, xc_ref, zr_ref, zi_ref, gate_ref, ctail_ref, out_ref, hl_ref, cs_ref, h_ref):
    lam_ref, on_ref = lru_refs[6], lru_refs[7]
    tt, c = xc_ref.shape

    a, u = _lru_terms(xc_ref[...], zr_ref[...], zi_ref[...], lam_ref)

    groups = tt // V7X_SUBLANES
    a3 = a.reshape(groups, V7X_SUBLANES, c)
    u3 = u.reshape(groups, V7X_SUBLANES, c)
    sub = lax.broadcasted_iota(jnp.int32, a3.shape, 1)
    k = 1
    while k < V7X_SUBLANES:
        m = sub >= k
        a_prev = pltpu.roll(a3, k, 1)
        u_prev = pltpu.roll(u3, k, 1)
        u3 = jnp.where(m, a3 * u_prev + u3, u3)
        a3 = jnp.where(m, a3 * a_prev, a3)
        k *= 2

    h = jnp.where(first, jnp.zeros((1, c), F32), h_ref[...])
    hs = []
    for g in range(groups):
        hr = a3[g] * h + u3[g]
        hs.append(hr)
        h = hr[V7X_SUBLANES - 1:V7X_SUBLANES, :]
    h_ref[...] = h

    y = jnp.concatenate(hs, axis=0) * jax.nn.gelu(gate_ref[...])
    out_ref[...] = _rms(y, on_ref[...]).astype(out_ref.dtype)
    hl_ref[0] = h
    cs_ref[0] = ctail_ref[...]


def _lru_sample_kernel(x_ref, cst_ref, h0_ref, cw_ref, cb_ref, wa_ref, ba_ref, wx_ref, bx_ref, lam_ref, on_ref,
                       out_ref, hl_ref, cs_ref):
    t = x_ref.shape[0]
    c = out_ref.shape[2]
    w = cw_ref[...]
    rows = [cst_ref[j] for j in range(CONV_W - 1)] + [x_ref[j, :, 0:c] for j in range(t)]
    h = h0_ref[...]
    for s in range(t):
        xc = cb_ref[...]
        for j in range(CONV_W):
            xc = xc + rows[s + j] * w[j:j + 1, :]
        zr, zi = _lru_preact(xc, (wa_ref, ba_ref, wx_ref, bx_ref))
        a, u = _lru_terms(xc, zr, zi, lam_ref)
        h = a * h + u
        y = h * jax.nn.gelu(x_ref[s, :, c:2 * c])
        out_ref[s] = _rms(y, on_ref[...]).astype(out_ref.dtype)
    hl_ref[...] = h
    for j in range(CONV_W - 1):
        cs_ref[j] = rows[t + j]


def _lru_sample(x_tm, cst_tm, h0, cw, cb, wa, ba, wx, bx, lam, on):
    t, s, w2 = x_tm.shape
    c = w2 // 2
    return pl.pallas_call(
        _lru_sample_kernel,
        out_shape=[jax.ShapeDtypeStruct((t, s, c), BF16), jax.ShapeDtypeStruct((s, c), F32),
                   jax.ShapeDtypeStruct((CONV_W - 1, s, c), F32)],
        compiler_params=pltpu.CompilerParams(vmem_limit_bytes=_vmem_limit(SMALL_KERNEL_VMEM_BYTES)),
        name="lru_sample",
    )(x_tm, cst_tm, h0, cw, cb, wa, ba, wx, bx, lam, on)


def _head_norm_rope(xc, gain, cos, sin, ones_ref):
    x2 = xc * xc
    hi = x2.astype(BF16)
    lo = (x2 - hi.astype(F32)).astype(BF16)
    ss = jnp.dot(hi, ones_ref[...], preferred_element_type=F32) + jnp.dot(lo, ones_ref[...], preferred_element_type=F32)
    xn = (xc * lax.rsqrt(ss + NORM_EPS)) * gain
    lane = lax.broadcasted_iota(jnp.int32, xn.shape, xn.ndim - 1)
    first_half = (lane & (HEAD_DIM // 2)) == 0
    ax = xn.ndim - 1
    swapped = jnp.where(first_half, pltpu.roll(xn, V7X_LANES - HEAD_DIM // 2, ax), pltpu.roll(xn, HEAD_DIM // 2, ax))
    return xn * cos + swapped * sin


def _group_lhs(qcols, g, lane_lo, axis):
    zero = jnp.zeros_like(qcols[0])
    parts = [jnp.where(lane_lo, qcols[2 * g], zero), jnp.where(lane_lo, qcols[2 * g + 1], zero),
             jnp.where(lane_lo, zero, qcols[2 * g]), jnp.where(lane_lo, zero, qcols[2 * g + 1])]
    return jnp.concatenate(parts, axis=axis).astype(BF16)


def _attn_prep(pa_ref, cos_ref, sin_ref, attn_refs, qh_ref, k2_ref, vt_ref, kvl_ref):
    qn_ref, kn_ref, _, ones_ref, _ = attn_refs
    tq = pa_ref.shape[0]
    blk = WINDOW
    lanes = V7X_LANES
    ncol = qh_ref.shape[0] // 2
    aw = ncol * lanes

    cos = cos_ref[...]
    sin = sin_ref[...]
    cols = [pa_ref[:, cc * lanes:(cc + 1) * lanes] for cc in range(ncol + 1)]
    sq = [x * x for x in cols]
    hi = [x.astype(BF16) for x in sq]
    lo = [(x - h.astype(F32)).astype(BF16) for x, h in zip(sq, hi)]
    ss = (jnp.dot(jnp.concatenate(hi, axis=0), ones_ref[...], preferred_element_type=F32)
          + jnp.dot(jnp.concatenate(lo, axis=0), ones_ref[...], preferred_element_type=F32))
    lane = lax.broadcasted_iota(jnp.int32, (tq, lanes), 1)
    first_half = (lane & (HEAD_DIM // 2)) == 0
    lane_lo = lane < HEAD_DIM
    rot = []
    for cc, x in enumerate(cols):
        gain = qn_ref[...] if cc < ncol else kn_ref[...]
        xn = (x * lax.rsqrt(ss[cc * tq:(cc + 1) * tq] + NORM_EPS)) * gain
        swapped = jnp.where(first_half, pltpu.roll(xn, lanes - HEAD_DIM // 2, 1), pltpu.roll(xn, HEAD_DIM // 2, 1))
        rot.append(xn * cos + swapped * sin)
    k = rot[ncol]
    v = pa_ref[:, aw + lanes:aw + 2 * lanes]

    zero = jnp.zeros((tq, lanes), F32)
    for cc in range(ncol):
        qh_ref[2 * cc] = jnp.where(lane_lo, rot[cc], zero).astype(BF16)
        qh_ref[2 * cc + 1] = jnp.where(lane_lo, zero, rot[cc]).astype(BF16)

    k_sw = pltpu.roll(k, HEAD_DIM, 1)
    k2_ref[0, blk:blk + tq, :] = jnp.where(lane_lo, k, k_sw).astype(BF16)
    k2_ref[1, blk:blk + tq, :] = jnp.where(lane_lo, k_sw, k).astype(BF16)
    for jb in range(tq // blk):
        vt_ref[:, (jb + 1) * blk:(jb + 2) * blk] = v[jb * blk:(jb + 1) * blk].T.astype(BF16)
    kvl_ref[0] = k[tq - blk:tq]
    kvl_ref[1] = v[tq - blk:tq]


def _attn_core(first, attn_refs, out_ref, pk_ref, pv_ref, qh_ref, k2_ref, vt_ref, kvl_ref, st_ref, pt_ref, rd_ref):
    _, _, sk_ref, _, on_ref = attn_refs
    tq = qh_ref.shape[1]
    blk = WINDOW
    nblk = tq // blk
    heads_per_group = qh_ref.shape[0] // N_KV_HEADS
    pairs = [(jb, g) for jb in range(nblk) for g in range(N_KV_HEADS)]

    for idx, (jb, g) in enumerate(pairs):
        r0 = jb * blk
        lhs = jnp.concatenate([qh_ref[h, r0:r0 + blk, :]
                               for h in range(g * heads_per_group, (g + 1) * heads_per_group)], axis=0)
        st_ref[idx] = lax.dot_general(k2_ref[g, r0:r0 + 2 * blk, :], lhs, (((1,), (1,)), ((), ())),
                                      preferred_element_type=F32)

    ncolq = heads_per_group * blk
    kj = lax.broadcasted_iota(jnp.int32, (blk, ncolq), 0)
    qi = lax.broadcasted_iota(jnp.int32, (blk, ncolq), 1) % blk
    tri = kj <= qi
    for idx, (jb, g) in enumerate(pairs):
        s_prev = st_ref[idx, 0:blk, :]
        if jb == 0:
            s_prev = s_prev + jnp.where(first, jnp.float32(-jnp.inf), jnp.float32(0.0))
        m = jnp.where(tri, st_ref[idx, blk:2 * blk, :], s_prev)
        sk = sk_ref[g:g + 1, :]
        mx = jnp.maximum(jnp.max(m, axis=0, keepdims=True), sk)
        p = jnp.exp2(m - mx)
        den = jnp.sum(p, axis=0, keepdims=True) + jnp.exp2(sk - mx)
        pz = jnp.zeros_like(p)
        pt_ref[idx] = jnp.concatenate([jnp.where(tri, pz, p), jnp.where(tri, p, pz)], axis=0).astype(BF16)
        rd_ref[idx] = 1.0 / den

    for jb in range(nblk):
        r0 = jb * blk
        ocols = []
        for g in range(N_KV_HEADS):
            idx = jb * N_KV_HEADS + g
            vt_g = vt_ref[g * HEAD_DIM:(g + 1) * HEAD_DIM, r0:r0 + 2 * blk]
            og = jnp.dot(vt_g, pt_ref[idx], preferred_element_type=F32) * rd_ref[idx]
            for half in range(heads_per_group // 2):
                pair = jnp.concatenate([og[:, (2 * half) * blk:(2 * half + 1) * blk],
                                        og[:, (2 * half + 1) * blk:(2 * half + 2) * blk]], axis=0)
                ocols.append(pair.T)
        o = jnp.concatenate(ocols, axis=1)
        out_ref[r0:r0 + blk, :] = _rms(o, on_ref[...]).astype(out_ref.dtype)

    k2_ref[:, 0:blk, :] = k2_ref[:, tq:tq + blk, :]
    vt_ref[:, 0:blk] = vt_ref[:, tq:tq + blk]
    pk_ref[0] = kvl_ref[0]
    pv_ref[0] = kvl_ref[1]


N_FFN_IN = 7
N_LRU = 8
N_ATTN = 5


def _mixer_in_kernel(*refs, tiles_per_seq, ntile):
    x_ref, cos_ref, sin_ref = refs[0:3]
    p0 = 3
    g1_ref, wg_ref, wu_ref, wd_ref, gm_ref, winl_ref, wina_ref = refs[p0:p0 + N_FFN_IN]
    lru_refs = refs[p0 + N_FFN_IN:p0 + N_FFN_IN + N_LRU]
    attn_refs = refs[p0 + N_FFN_IN + N_LRU:p0 + N_FFN_IN + N_LRU + N_ATTN]
    o0 = p0 + N_FFN_IN + N_LRU + N_ATTN
    x1_ref, lo_ref, ao_ref, hl_ref, cs_ref, pk_ref, pv_ref = refs[o0:o0 + 7]
    (xn_ref, acc_ref, pl_ref, pa_ref, xh_ref, h_ref, xc_ref, zr_ref, zi_ref, gate_ref, ctail_ref,
     qh_ref, k2_ref, vt_ref, kvl_ref, st_ref, pt_ref, rd_ref) = refs[o0 + 7:]

    i = pl.program_id(0)

    @pl.when(i == 0)
    def _():
        for ref in (pl_ref, pa_ref, xh_ref, h_ref, xc_ref, zr_ref, zi_ref, gate_ref, ctail_ref,
                    qh_ref, k2_ref, vt_ref, kvl_ref):
            ref[...] = jnp.zeros_like(ref)

    def step(with_ffn):
        first = lax.rem(jnp.maximum(i - 1, 0), tiles_per_seq) == 0
        _lru_prep(pl_ref, first, lru_refs, xh_ref, xc_ref, zr_ref, zi_ref, gate_ref, ctail_ref)
        _attn_prep(pa_ref, cos_ref, sin_ref, attn_refs, qh_ref, k2_ref, vt_ref, kvl_ref)
        _lru_core(first, lru_refs, xc_ref, zr_ref, zi_ref, gate_ref, ctail_ref, lo_ref, hl_ref, cs_ref, h_ref)
        _attn_core(first, attn_refs, ao_ref, pk_ref, pv_ref, qh_ref, k2_ref, vt_ref, kvl_ref, st_ref, pt_ref, rd_ref)
        if with_ffn:
            _ffn_in_tile(x_ref, (g1_ref, wg_ref, wu_ref, wd_ref), gm_ref, winl_ref, wina_ref,
                         x1_ref, pl_ref, pa_ref, xn_ref, acc_ref)

    pl.when(i < ntile)(functools.partial(step, True))
    pl.when(i >= ntile)(functools.partial(step, False))


def _mixer_in(x, seq_len, cos, sin, ffn_in_p, lru_p, attn_p):
    n, d = x.shape
    tm = TOKEN_TILE
    assert seq_len % tm == 0 and n % seq_len == 0 and tm % WINDOW == 0
    nseq = n // seq_len
    tps = seq_len // tm
    ntile = n // tm
    wg, winl, wina = ffn_in_p[1], ffn_in_p[5], ffn_in_p[6]
    w2, wa = winl.shape[1], wina.shape[1]
    c = w2 // 2
    aw = attn_p[4].shape[1]
    kvw = N_KV_HEADS * HEAD_DIM
    npair = tm // WINDOW * N_KV_HEADS
    heads_per_group = aw // HEAD_DIM // N_KV_HEADS
    cur = lambda i: jnp.minimum(i, ntile - 1)
    mix = lambda i: jnp.maximum(i - 1, 0)
    params = list(ffn_in_p) + list(lru_p) + list(attn_p)
    weights = sum(p.size * p.dtype.itemsize for p in params)
    tiles = (2 * 4 * tm * 2 * d + 2 * 2 * tm * (c + aw) + 4 * tm * (w2 + wa) + tm * d * (2 + 4) + 6 * 4 * tm * c
             + npair * 2 * WINDOW * heads_per_group * WINDOW * (4 + 2))
    return pl.pallas_call(
        functools.partial(_mixer_in_kernel, tiles_per_seq=tps, ntile=ntile),
        grid=(ntile + 1,),
        in_specs=[pl.BlockSpec((tm, d), lambda i: (cur(i), 0)),
                  pl.BlockSpec((tm, V7X_LANES), lambda i: (lax.rem(mix(i), tps), 0)),
                  pl.BlockSpec((tm, V7X_LANES), lambda i: (lax.rem(mix(i), tps), 0))]
                 + [_resident(p.shape) for p in params],
        out_specs=[pl.BlockSpec((tm, d), lambda i: (cur(i), 0)),
                   pl.BlockSpec((tm, c), lambda i: (mix(i), 0)),
                   pl.BlockSpec((tm, aw), lambda i: (mix(i), 0)),
                   pl.BlockSpec((1, 1, c), lambda i: (mix(i) // tps, 0, 0)),
                   pl.BlockSpec((1, CONV_W - 1, c), lambda i: (mix(i) // tps, 0, 0)),
                   pl.BlockSpec((1, WINDOW, kvw), lambda i: (mix(i) // tps, 0, 0)),
                   pl.BlockSpec((1, WINDOW, kvw), lambda i: (mix(i) // tps, 0, 0))],
        out_shape=[jax.ShapeDtypeStruct((n, d), F32), jax.ShapeDtypeStruct((n, c), BF16),
                   jax.ShapeDtypeStruct((n, aw), BF16), jax.ShapeDtypeStruct((nseq, 1, c), F32),
                   jax.ShapeDtypeStruct((nseq, CONV_W - 1, c), F32), jax.ShapeDtypeStruct((nseq, WINDOW, kvw), F32),
                   jax.ShapeDtypeStruct((nseq, WINDOW, kvw), F32)],
        scratch_shapes=[pltpu.VMEM((tm, d), BF16), pltpu.VMEM((tm, d), F32),
                        pltpu.VMEM((tm, w2), F32), pltpu.VMEM((tm, wa), F32),
                        pltpu.VMEM((tm + V7X_SUBLANES, c), F32), pltpu.VMEM((1, c), F32),
                        pltpu.VMEM((tm, c), F32), pltpu.VMEM((tm, c), F32), pltpu.VMEM((tm, c), F32),
                        pltpu.VMEM((tm, c), F32), pltpu.VMEM((CONV_W - 1, c), F32),
                        pltpu.VMEM((2 * aw // V7X_LANES, tm, V7X_LANES), BF16),
                        pltpu.VMEM((N_KV_HEADS, WINDOW + tm, V7X_LANES), BF16),
                        pltpu.VMEM((V7X_LANES, WINDOW + tm), BF16),
                        pltpu.VMEM((2, WINDOW, kvw), F32),
                        pltpu.VMEM((npair, 2 * WINDOW, heads_per_group * WINDOW), F32),
                        pltpu.VMEM((npair, 2 * WINDOW, heads_per_group * WINDOW), BF16),
                        pltpu.VMEM((npair, 1, heads_per_group * WINDOW), F32)],
        compiler_params=pltpu.CompilerParams(dimension_semantics=("arbitrary",),
                                             vmem_limit_bytes=_vmem_limit(weights + tiles + MIXER_TEMP_BYTES)),
        name="mixer_in",
    )(x, cos, sin, *params)


def _ffn_in_kernel(x_ref, g1_ref, wg_ref, wu_ref, wd_ref, gm_ref, winl_ref, wina_ref,
                   x1_ref, pl_ref, pa_ref, xn_ref, acc_ref):
    _ffn_in_tile(x_ref, (g1_ref, wg_ref, wu_ref, wd_ref), gm_ref, winl_ref, wina_ref,
                 x1_ref, pl_ref, pa_ref, xn_ref, acc_ref)


def _ffn_in(x, g1, wg, wu, wd, gm, winl, wina):
    n, d = x.shape
    tm = min(TOKEN_TILE, n)
    wl, wa = winl.shape[1], wina.shape[1]
    row = lambda w: pl.BlockSpec((tm, w), lambda i: (i, 0))
    weights = 2 * (wg.size + wu.size + wd.size + winl.size + wina.size)
    tiles = 2 * 4 * tm * (2 * d + wl + wa) + tm * d * (2 + 4) + 4 * 4 * tm * max(FF_CHUNK, wl)
    return pl.pallas_call(
        _ffn_in_kernel,
        grid=(n // tm,),
        in_specs=[row(d), _resident(g1.shape), _resident(wg.shape), _resident(wu.shape), _resident(wd.shape),
                  _resident(gm.shape), _resident(winl.shape), _resident(wina.shape)],
        out_specs=[row(d), row(wl), row(wa)],
        out_shape=[jax.ShapeDtypeStruct((n, d), F32), jax.ShapeDtypeStruct((n, wl), F32),
                   jax.ShapeDtypeStruct((n, wa), F32)],
        scratch_shapes=[pltpu.VMEM((tm, d), BF16), pltpu.VMEM((tm, d), F32)],
        compiler_params=pltpu.CompilerParams(dimension_semantics=("arbitrary",),
                                             vmem_limit_bytes=_vmem_limit(weights + tiles + FFN_TEMP_BYTES)),
        name="ffn_in",
    )(x, g1, wg, wu, wd, gm, winl, wina)


def _out_ffn_kernel(lo_ref, ao_ref, x1_ref, wol_ref, woa_ref, g2_ref, wg_ref, wu_ref, wd_ref,
                    out_ref, xn_ref, acc_ref):
    y = jnp.dot(lo_ref[...], wol_ref[...], preferred_element_type=F32)
    y = y + jnp.dot(ao_ref[...], woa_ref[...], preferred_element_type=F32)
    x2 = x1_ref[...] + y
    out_ref[...] = x2
    xn_ref[...] = _rms(x2, g2_ref[...]).astype(BF16)
    out_ref[...] = out_ref[...] + 0.5 * _ffn(xn_ref, wg_ref, wu_ref, wd_ref, acc_ref)


def _out_ffn(lo, ao, x1, wol, woa, g2, wg, wu, wd):
    n, d = x1.shape
    tm = OUT_FFN_TILE if n % OUT_FFN_TILE == 0 else min(TOKEN_TILE, n)
    row = lambda w: pl.BlockSpec((tm, w), lambda i: (i, 0))
    weights = 2 * (wg.size + wu.size + wd.size + wol.size + woa.size)
    tiles = 2 * tm * (2 * lo.shape[1] + 2 * ao.shape[1] + 8 * d) + tm * d * (2 + 4) + 4 * 4 * tm * d
    return pl.pallas_call(
        _out_ffn_kernel,
        grid=(n // tm,),
        in_specs=[row(lo.shape[1]), row(ao.shape[1]), row(d), _resident(wol.shape), _resident(woa.shape),
                  _resident(g2.shape), _resident(wg.shape), _resident(wu.shape), _resident(wd.shape)],
        out_specs=row(d),
        out_shape=jax.ShapeDtypeStruct((n, d), F32),
        scratch_shapes=[pltpu.VMEM((tm, d), BF16), pltpu.VMEM((tm, d), F32)],
        compiler_params=pltpu.CompilerParams(dimension_semantics=("arbitrary",),
                                             vmem_limit_bytes=_vmem_limit(weights + tiles + FFN_TEMP_BYTES)),
        name="out_ffn",
    )(lo, ao, x1, wol, woa, g2, wg, wu, wd)


def _qk_prep_kernel(pa_ref, cos_ref, sin_ref, qn_ref, kn_ref, ones_ref, q_ref, k_ref):
    lanes = V7X_LANES
    aw = q_ref.shape[1]
    cos = cos_ref[...]
    sin = sin_ref[...]
    for cc in range(aw // lanes):
        q_ref[:, cc * lanes:(cc + 1) * lanes] = _head_norm_rope(
            pa_ref[:, cc * lanes:(cc + 1) * lanes], qn_ref[...], cos, sin, ones_ref) * (HEAD_DIM ** -0.5)
    k_ref[...] = _head_norm_rope(pa_ref[:, aw:aw + lanes], kn_ref[...], cos, sin, ones_ref)


def _qk_prep(pa, cos, sin, qn, kn, ones, aw):
    n = pa.shape[0]
    return pl.pallas_call(
        _qk_prep_kernel,
        out_shape=[jax.ShapeDtypeStruct((n, aw), F32), jax.ShapeDtypeStruct((n, N_KV_HEADS * HEAD_DIM), F32)],
        compiler_params=pltpu.CompilerParams(vmem_limit_bytes=_vmem_limit(SMALL_KERNEL_VMEM_BYTES)),
        name="qk_prep_sample",
    )(pa, cos, sin, qn, kn, ones)


def _attn_cached_kernel(q_ref, kn_ref, vn_ref, ck_ref, cv_ref, sk_ref, on_ref,
                        out_ref, nk_ref, nv_ref, kk_ref, vv_ref):
    sb, t, aw = q_ref.shape
    w = ck_ref.shape[1]
    lanes = V7X_LANES
    jpad = kk_ref.shape[1]

    kk_ref[:, 0:w, :] = ck_ref[...]
    vv_ref[:, 0:w, :] = cv_ref[...]
    kk_ref[:, w:w + t, :] = kn_ref[...]
    vv_ref[:, w:w + t, :] = vn_ref[...]
    kk_ref[:, w + t:jpad, :] = jnp.zeros((sb, jpad - w - t, lanes), F32)
    vv_ref[:, w + t:jpad, :] = jnp.zeros((sb, jpad - w - t, lanes), F32)
    nk_ref[...] = kk_ref[:, t:t + w, :]
    nv_ref[...] = vv_ref[:, t:t + w, :]

    kk = kk_ref[...]
    vv = vv_ref[...]
    k_sw = pltpu.roll(kk, HEAD_DIM, 2)
    v_sw = pltpu.roll(vv, HEAD_DIM, 2)
    lane_k = lax.broadcasted_iota(jnp.int32, kk.shape, 2) < HEAD_DIM
    lane_q = lax.broadcasted_iota(jnp.int32, (sb, t, lanes), 2) < HEAD_DIM
    k2 = (jnp.where(lane_k, kk, k_sw).astype(BF16), jnp.where(lane_k, k_sw, kk).astype(BF16))
    v2 = (vv.astype(BF16), v_sw.astype(BF16))

    qcols = [q_ref[:, :, cc * lanes:(cc + 1) * lanes] for cc in range(aw // lanes)]
    tq = lax.broadcasted_iota(jnp.int32, (sb, 4 * t, jpad), 1) % t
    j = lax.broadcasted_iota(jnp.int32, (sb, 4 * t, jpad), 2)
    valid = jnp.logical_and(j > tq, j <= tq + w)
    neg_inf = jnp.float32(-jnp.inf)

    ocols = []
    for g in range(N_KV_HEADS):
        lhs = _group_lhs(qcols, g, lane_q, 1)
        s = jnp.einsum('bqd,bjd->bqj', lhs, k2[g], preferred_element_type=F32)
        s = jnp.where(valid, s, neg_inf)
        heads = (4 * g, 4 * g + 2, 4 * g + 1, 4 * g + 3)
        sk = jnp.concatenate([jnp.broadcast_to(sk_ref[h:h + 1, 0:1].reshape(1, 1, 1), (sb, t, 1)) for h in heads],
                             axis=1)
        mx = jnp.maximum(jnp.max(s, axis=-1, keepdims=True), sk)
        p = jnp.exp(s - mx)
        den = jnp.sum(p, axis=-1, keepdims=True) + jnp.exp(sk - mx)
        pb = p.astype(BF16)
        o_even = jnp.einsum('bqj,bjd->bqd', pb[:, 0:2 * t], v2[g], preferred_element_type=F32) / den[:, 0:2 * t]
        o_odd = jnp.einsum('bqj,bjd->bqd', pb[:, 2 * t:4 * t], v2[1 - g], preferred_element_type=F32) / den[:, 2 * t:4 * t]
        for half in range(2):
            ocols.append(jnp.where(lane_q, o_even[:, half * t:(half + 1) * t], o_odd[:, half * t:(half + 1) * t]))
    o = jnp.concatenate(ocols, axis=2)
    out_ref[...] = _rms(o, on_ref[...].reshape(1, 1, aw)).astype(out_ref.dtype)


def _attn_cached(q3, kn3, vn3, ck, cv, sk, on):
    s, t, aw = q3.shape
    w, kvw = ck.shape[1], ck.shape[2]
    sb = min(SAMPLE_SEQ_TILE, s)
    jpad = -(-(w + t) // V7X_SUBLANES) * V7X_SUBLANES
    seq = lambda a, c: pl.BlockSpec((sb, a, c), lambda i: (i, 0, 0))
    return pl.pallas_call(
        _attn_cached_kernel,
        grid=(s // sb,),
        in_specs=[seq(t, aw), seq(t, kvw), seq(t, kvw), seq(w, kvw), seq(w, kvw), _resident(sk.shape), _resident(on.shape)],
        out_specs=[seq(t, aw), seq(w, kvw), seq(w, kvw)],
        out_shape=[jax.ShapeDtypeStruct((s, t, aw), BF16), jax.ShapeDtypeStruct((s, w, kvw), F32),
                   jax.ShapeDtypeStruct((s, w, kvw), F32)],
        scratch_shapes=[pltpu.VMEM((sb, jpad, kvw), F32), pltpu.VMEM((sb, jpad, kvw), F32)],
        compiler_params=pltpu.CompilerParams(dimension_semantics=("arbitrary",),
                                             vmem_limit_bytes=_vmem_limit(SMALL_KERNEL_VMEM_BYTES)),
        name="attn_sample",
    )(q3, kn3, vn3, ck, cv, sk, on)


ROPE_SPLIT = 128


def _rope_tables(start, n):
    half = HEAD_DIM // 2
    lane = jnp.arange(V7X_LANES, dtype=jnp.int32)
    inv_freq = ROPE_THETA ** (-(lane % half).astype(F32) / half)
    sign = jnp.where((lane % HEAD_DIM) < half, -1.0, 1.0).astype(F32)
    if n % ROPE_SPLIT:
        pos = start + jnp.arange(n, dtype=jnp.int32)
        ang = pos.astype(F32)[:, None] * inv_freq[None, :]
        return jnp.cos(ang), jnp.sin(ang) * sign[None, :]
    lo = jnp.arange(ROPE_SPLIT, dtype=jnp.int32)
    hi = start + ROPE_SPLIT * jnp.arange(n // ROPE_SPLIT, dtype=jnp.int32)
    a_hi = hi.astype(F32)[:, None] * inv_freq[None, :]
    a_lo = lo.astype(F32)[:, None] * inv_freq[None, :]
    ch, sh = jnp.cos(a_hi)[:, None, :], jnp.sin(a_hi)[:, None, :]
    cl, sl = jnp.cos(a_lo)[None, :, :], jnp.sin(a_lo)[None, :, :]
    cos = (ch * cl - sh * sl).reshape(n, V7X_LANES)
    sin = (sh * cl + ch * sl).reshape(n, V7X_LANES)
    return cos, sin * sign[None, :]


def _block_diag(w):
    n, c, d = w.shape
    eye = jnp.eye(n, dtype=w.dtype)
    return (w[:, :, None, :] * eye[:, None, :, None]).reshape(n * c, n * d)


def _row(v):
    return v.reshape(1, -1).astype(F32)


def kernel(x_prompt, x_sample, state_lru_h, state_conv, cache_k, cache_v, ffn1_norm, ffn1_w_gate, ffn1_w_up, ffn1_w_down, mix_norm, w_in, conv_w, conv_b, lru_wa, lru_ba, lru_wx, lru_bx, lru_lambda, q_norm, k_norm, attn_sinks, lru_out_norm, attn_out_norm, w_out, ffn2_norm, ffn2_w_gate, ffn2_w_up, ffn2_w_down):
    depth = w_in.shape[0]
    b, t, d = x_prompt.shape
    s, ts, _ = x_sample.shape
    c = lru_lambda.shape[1]
    aw = attn_out_norm.shape[1]
    kvw = N_KV_HEADS * HEAD_DIM
    win = cache_k.shape[2]
    assert t >= WINDOW and win == WINDOW
    assert GATE_DIAG_TILE % lru_wa.shape[2] == 0 and c % min(GATE_DIAG_TILE, c) == 0

    cos_p, sin_p = _rope_tables(0, t)
    cos_s, sin_s = _rope_tables(PAST_LEN, ts)
    cos_s, sin_s = jnp.tile(cos_s, (s, 1)), jnp.tile(sin_s, (s, 1))
    ones = _block_diag(jnp.full((V7X_LANES // HEAD_DIM, HEAD_DIM, HEAD_DIM), 1.0 / HEAD_DIM, BF16))
    log2e = 1.0 / jnp.log(jnp.float32(2.0))

    yp = x_prompt.reshape(b * t, d)
    ys = x_sample.reshape(s * ts, d)
    outs = [[] for _ in range(8)]
    for l in range(depth):
        ffn1 = (_row(ffn1_norm[l]), (0.5 * ffn1_w_gate[l]).astype(BF16), ffn1_w_up[l].astype(BF16),
                ffn1_w_down[l].astype(BF16))
        ffn2 = (_row(ffn2_norm[l]), (0.5 * ffn2_w_gate[l]).astype(BF16), ffn2_w_up[l].astype(BF16),
                ffn2_w_down[l].astype(BF16))
        win_b = w_in[l].astype(BF16)
        ffn_in_p = ffn1 + (_row(mix_norm[l]), win_b[:, :2 * c], win_b[:, 2 * c:])
        wout_b = w_out[l].astype(BF16)
        wol, woa = wout_b[:c], wout_b[c:]
        lru_p = (conv_w[l], _row(conv_b[l]), (0.5 * _block_diag(lru_wa[l])).astype(BF16), _row(0.5 * lru_ba[l]),
                 (0.5 * _block_diag(lru_wx[l])).astype(BF16), _row(0.5 * lru_bx[l]), _row(lru_lambda[l]),
                 _row(lru_out_norm[l]))
        reps = V7X_LANES // HEAD_DIM
        qn = _row(jnp.tile(q_norm[l], reps))
        kn = _row(jnp.tile(k_norm[l], reps))
        sk = jnp.broadcast_to(attn_sinks[l].astype(F32)[:, None], (attn_sinks.shape[1], V7X_LANES))
        qn_p = qn * (HEAD_DIM ** -0.5 * log2e)
        sk_rows = jnp.repeat(attn_sinks[l].astype(F32).reshape(N_KV_HEADS, -1), WINDOW, axis=1) * log2e
        a_on = _row(attn_out_norm[l])

        x1, lo, ao, hl, cs, pk, pv = _mixer_in(yp, t, cos_p, sin_p, ffn_in_p, lru_p, (qn_p, kn, sk_rows, ones, a_on))
        yp = _out_ffn(lo, ao, x1, wol, woa, *ffn2)
        outs[0].append(hl.reshape(b, c))
        outs[1].append(cs)
        outs[2].append(pk.reshape(b, WINDOW, N_KV_HEADS, HEAD_DIM))
        outs[3].append(pv.reshape(b, WINDOW, N_KV_HEADS, HEAD_DIM))

        x1s, pl_s, pa_s = _ffn_in(ys, *ffn_in_p)
        x_tm = pl_s.reshape(s, ts, 2 * c).transpose(1, 0, 2)
        cst_tm = state_conv[l].transpose(1, 0, 2)
        lo_tm, hls, cs_tm = _lru_sample(x_tm, cst_tm, state_lru_h[l], *lru_p)
        q_s, k_s = _qk_prep(pa_s, cos_s, sin_s, qn, kn, ones, aw)
        v_s = pa_s[:, aw + kvw:]
        aos, nk, nv = _attn_cached(q_s.reshape(s, ts, aw), k_s.reshape(s, ts, kvw), v_s.reshape(s, ts, kvw),
                                   cache_k[l].reshape(s, win, kvw), cache_v[l].reshape(s, win, kvw), sk, a_on)
        ys = _out_ffn(lo_tm.transpose(1, 0, 2).reshape(s * ts, c), aos.reshape(s * ts, aw), x1s, wol, woa, *ffn2)
        outs[4].append(hls)
        outs[5].append(cs_tm.transpose(1, 0, 2))
        outs[6].append(nk.reshape(s, win, N_KV_HEADS, HEAD_DIM))
        outs[7].append(nv.reshape(s, win, N_KV_HEADS, HEAD_DIM))

    st = [jnp.stack(o) for o in outs]
    return (yp.reshape(b, t, d), ys.reshape(s, ts, d), st[0], st[1], st[2], st[3], st[4], st[5], st[6], st[7])
```

```python
import functools

import jax
import jax.numpy as jnp
from jax import lax
from jax.experimental import pallas as pl
from jax.experimental.pallas import tpu as pltpu

F32 = jnp.float32
BF16 = jnp.bfloat16

NORM_EPS = 1e-6
LRU_C = 8.0
CONV_W = 4
HEAD_DIM = 64
N_KV_HEADS = 2
WINDOW = 128
ROPE_THETA = 10000.0
PAST_LEN = 16384

V7X_LANES = 128
V7X_SUBLANES = 8
V7X_MXU_DIM = 256
V7X_VMEM_BYTES = 64 * 1024 * 1024

TOKEN_TILE = 512
OUT_FFN_TILE = 1024
SAMPLE_SEQ_TILE = 32
FF_CHUNK = V7X_MXU_DIM
GATE_DIAG_TILE = V7X_MXU_DIM


MIB = 1024 * 1024
VMEM_KEEP_FREE_BYTES = 4 * MIB
FFN_TEMP_BYTES = 8 * MIB
MIXER_TEMP_BYTES = 20 * MIB
SMALL_KERNEL_VMEM_BYTES = 40 * MIB


def _vmem_limit(nbytes):
    return int(min(nbytes, V7X_VMEM_BYTES - VMEM_KEEP_FREE_BYTES))


def _resident(shape):
    nd = len(shape)
    return pl.BlockSpec(shape, lambda *_: (0,) * nd, pipeline_mode=pl.Buffered(1))


def _rms(x, g):
    ms = jnp.mean(x * x, axis=-1, keepdims=True)
    return (x * lax.rsqrt(ms + NORM_EPS)) * g


def _ffn(xn_ref, wg_ref, wu_ref, wd_ref, acc_ref):
    acc_ref[...] = jnp.zeros_like(acc_ref)
    for c0 in range(0, wg_ref.shape[1], FF_CHUNK):
        xn = xn_ref[...]
        h = jnp.dot(xn, wg_ref[:, c0:c0 + FF_CHUNK], preferred_element_type=F32)
        u = jnp.dot(xn, wu_ref[:, c0:c0 + FF_CHUNK], preferred_element_type=F32)
        a = ((h + h * jnp.tanh(h)) * u).astype(BF16)
        acc_ref[...] += jnp.dot(a, wd_ref[c0:c0 + FF_CHUNK, :], preferred_element_type=F32)
    return acc_ref[...]


def _ffn_in_tile(x_ref, ffn_refs, gm_ref, winl_ref, wina_ref, x1_ref, pl_ref, pa_ref, xn_ref, acc_ref):
    g1_ref, wg_ref, wu_ref, wd_ref = ffn_refs
    xn_ref[...] = _rms(x_ref[...], g1_ref[...]).astype(BF16)
    x1 = x_ref[...] + 0.5 * _ffn(xn_ref, wg_ref, wu_ref, wd_ref, acc_ref)
    x1_ref[...] = x1
    xn_ref[...] = _rms(x1, gm_ref[...]).astype(BF16)
    pl_ref[...] = jnp.dot(xn_ref[...], winl_ref[...], preferred_element_type=F32)
    pa_ref[...] = jnp.dot(xn_ref[...], wina_ref[...], preferred_element_type=F32)


def _lru_preact(xc, gate_w):
    wa_ref, ba_ref, wx_ref, bx_ref = gate_w
    xcb = xc.astype(BF16)
    c = xc.shape[-1]
    dt = min(GATE_DIAG_TILE, c)

    def blockdiag_dot(w_ref):
        return jnp.concatenate([jnp.dot(xcb[:, s:s + dt], w_ref[s:s + dt, s:s + dt], preferred_element_type=F32)
                                for s in range(0, c, dt)], axis=1)

    zr = blockdiag_dot(wa_ref) + ba_ref[...]
    zi = blockdiag_dot(wx_ref) + bx_ref[...]
    return zr, zi


def _lru_terms(xc, zr, zi, lam_ref):
    i = 0.5 * jnp.tanh(zi) + 0.5
    half_c = (-0.5 * LRU_C) * jax.nn.softplus(-lam_ref[...])
    log_a = half_c * jnp.tanh(zr) + half_c
    a = jnp.exp(log_a)
    w = -jnp.tanh(log_a) * (a * a + 1.0)
    root = jnp.where(w > 0.0, w * lax.rsqrt(w), 0.0)
    u = root * (i * xc)
    return a, u


def _lru_prep(p_ref, first, lru_refs, xh_ref, xc_ref, zr_ref, zi_ref, gate_ref, ctail_ref):
    cw_ref, cb_ref, wa_ref, ba_ref, wx_ref, bx_ref, _, _ = lru_refs
    tt = p_ref.shape[0]
    c = xc_ref.shape[1]
    hist = V7X_SUBLANES

    xh_ref[0:hist, :] = jnp.where(first, jnp.zeros((hist, c), F32), xh_ref[0:hist, :])
    x = p_ref[:, 0:c]
    xh_ref[hist:hist + tt, :] = x
    xg = xh_ref[...].reshape(tt // hist + 1, hist, c)
    sub = lax.broadcasted_iota(jnp.int32, (tt // hist, hist, c), 1)
    w = cw_ref[...]
    xc = cb_ref[...].reshape(1, 1, c)
    for j in range(CONV_W - 1):
        k = CONV_W - 1 - j
        rolled = pltpu.roll(xg, k, 1)
        xc = xc + jnp.where(sub >= k, rolled[1:], rolled[:-1]) * w[j:j + 1, :].reshape(1, 1, c)
    xc = (xc + xg[1:] * w[CONV_W - 1:CONV_W, :].reshape(1, 1, c)).reshape(tt, c)
    xh_ref[0:hist, :] = x[tt - hist:tt, :]
    ctail_ref[...] = x[tt - (CONV_W - 1):tt, :]

    xc_ref[...] = xc
    zr, zi = _lru_preact(xc, (wa_ref, ba_ref, wx_ref, bx_ref))
    zr_ref[...] = zr
    zi_ref[...] = zi
    gate_ref[...] = p_ref[:, c:2 * c]


def _lru_core(first, lru_refs, xc_ref, zr_ref, zi_ref, gate_ref, ctail_ref, out_ref, hl_ref, cs_ref, h_ref):
    lam_ref, on_ref = lru_refs[6], lru_refs[7]
    tt, c = xc_ref.shape

    a, u = _lru_terms(xc_ref[...], zr_ref[...], zi_ref[...], lam_ref)

    groups = tt // V7X_SUBLANES
    a3 = a.reshape(groups, V7X_SUBLANES, c)
    u3 = u.reshape(groups, V7X_SUBLANES, c)
    sub = lax.broadcasted_iota(jnp.int32, a3.shape, 1)
    k = 1
    while k < V7X_SUBLANES:
        m = sub >= k
        a_prev = pltpu.roll(a3, k, 1)
        u_prev = pltpu.roll(u3, k, 1)
        u3 = jnp.where(m, a3 * u_prev + u3, u3)
        a3 = jnp.where(m, a3 * a_prev, a3)
        k *= 2

    h = jnp.where(first, jnp.zeros((1, c), F32), h_ref[...])
    hs = []
    for g in range(groups):
        hr = a3[g] * h + u3[g]
        hs.append(hr)
        h = hr[V7X_SUBLANES - 1:V7X_SUBLANES, :]
    h_ref[...] = h

    y = jnp.concatenate(hs, axis=0) * jax.nn.gelu(gate_ref[...])
    out_ref[...] = _rms(y, on_ref[...]).astype(out_ref.dtype)
    hl_ref[0] = h
    cs_ref[0] = ctail_ref[...]


def _lru_sample_kernel(x_ref, cst_ref, h0_ref, cw_ref, cb_ref, wa_ref, ba_ref, wx_ref, bx_ref, lam_ref, on_ref,
                       out_ref, hl_ref, cs_ref):
    t = x_ref.shape[0]
    c = out_ref.shape[2]
    w = cw_ref[...]
    rows = [cst_ref[j] for j in range(CONV_W - 1)] + [x_ref[j, :, 0:c] for j in range(t)]
    h = h0_ref[...]
    for s in range(t):
        xc = cb_ref[...]
        for j in range(CONV_W):
            xc = xc + rows[s + j] * w[j:j + 1, :]
        zr, zi = _lru_preact(xc, (wa_ref, ba_ref, wx_ref, bx_ref))
        a, u = _lru_terms(xc, zr, zi, lam_ref)
        h = a * h + u
        y = h * jax.nn.gelu(x_ref[s, :, c:2 * c])
        out_ref[s] = _rms(y, on_ref[...]).astype(out_ref.dtype)
    hl_ref[...] = h
    for j in range(CONV_W - 1):
        cs_ref[j] = rows[t + j]


def _lru_sample(x_tm, cst_tm, h0, cw, cb, wa, ba, wx, bx, lam, on):
    t, s, w2 = x_tm.shape
    c = w2 // 2
    return pl.pallas_call(
        _lru_sample_kernel,
        out_shape=[jax.ShapeDtypeStruct((t, s, c), BF16), jax.ShapeDtypeStruct((s, c), F32),
                   jax.ShapeDtypeStruct((CONV_W - 1, s, c), F32)],
        compiler_params=pltpu.CompilerParams(vmem_limit_bytes=_vmem_limit(SMALL_KERNEL_VMEM_BYTES)),
        name="lru_sample",
    )(x_tm, cst_tm, h0, cw, cb, wa, ba, wx, bx, lam, on)


def _head_norm_rope(xc, gain, cos, sin, ones_ref):
    x2 = xc * xc
    hi = x2.astype(BF16)
    lo = (x2 - hi.astype(F32)).astype(BF16)
    ss = jnp.dot(hi, ones_ref[...], preferred_element_type=F32) + jnp.dot(lo, ones_ref[...], preferred_element_type=F32)
    xn = (xc * lax.rsqrt(ss + NORM_EPS)) * gain
    lane = lax.broadcasted_iota(jnp.int32, xn.shape, xn.ndim - 1)
    first_half = (lane & (HEAD_DIM // 2)) == 0
    ax = xn.ndim - 1
    swapped = jnp.where(first_half, pltpu.roll(xn, V7X_LANES - HEAD_DIM // 2, ax), pltpu.roll(xn, HEAD_DIM // 2, ax))
    return xn * cos + swapped * sin


def _group_lhs(qcols, g, lane_lo, axis):
    zero = jnp.zeros_like(qcols[0])
    parts = [jnp.where(lane_lo, qcols[2 * g], zero), jnp.where(lane_lo, qcols[2 * g + 1], zero),
             jnp.where(lane_lo, zero, qcols[2 * g]), jnp.where(lane_lo, zero, qcols[2 * g + 1])]
    return jnp.concatenate(parts, axis=axis).astype(BF16)


def _attn_prep(pa_ref, cos_ref, sin_ref, attn_refs, qh_ref, k2_ref, vt_ref, kvl_ref):
    qn_ref, kn_ref, _, ones_ref, _ = attn_refs
    tq = pa_ref.shape[0]
    blk = WINDOW
    lanes = V7X_LANES
    ncol = qh_ref.shape[0] // 2
    aw = ncol * lanes

    cos = cos_ref[...]
    sin = sin_ref[...]
    cols = [pa_ref[:, cc * lanes:(cc + 1) * lanes] for cc in range(ncol + 1)]
    sq = [x * x for x in cols]
    hi = [x.astype(BF16) for x in sq]
    lo = [(x - h.astype(F32)).astype(BF16) for x, h in zip(sq, hi)]
    ss = (jnp.dot(jnp.concatenate(hi, axis=0), ones_ref[...], preferred_element_type=F32)
          + jnp.dot(jnp.concatenate(lo, axis=0), ones_ref[...], preferred_element_type=F32))
    lane = lax.broadcasted_iota(jnp.int32, (tq, lanes), 1)
    first_half = (lane & (HEAD_DIM // 2)) == 0
    lane_lo = lane < HEAD_DIM
    rot = []
    for cc, x in enumerate(cols):
        gain = qn_ref[...] if cc < ncol else kn_ref[...]
        xn = (x * lax.rsqrt(ss[cc * tq:(cc + 1) * tq] + NORM_EPS)) * gain
        swapped = jnp.where(first_half, pltpu.roll(xn, lanes - HEAD_DIM // 2, 1), pltpu.roll(xn, HEAD_DIM // 2, 1))
        rot.append(xn * cos + swapped * sin)
    k = rot[ncol]
    v = pa_ref[:, aw + lanes:aw + 2 * lanes]

    zero = jnp.zeros((tq, lanes), F32)
    for cc in range(ncol):
        qh_ref[2 * cc] = jnp.where(lane_lo, rot[cc], zero).astype(BF16)
        qh_ref[2 * cc + 1] = jnp.where(lane_lo, zero, rot[cc]).astype(BF16)

    k_sw = pltpu.roll(k, HEAD_DIM, 1)
    k2_ref[0, blk:blk + tq, :] = jnp.where(lane_lo, k, k_sw).astype(BF16)
    k2_ref[1, blk:blk + tq, :] = jnp.where(lane_lo, k_sw, k).astype(BF16)
    for jb in range(tq // blk):
        vt_ref[:, (jb + 1) * blk:(jb + 2) * blk] = v[jb * blk:(jb + 1) * blk].T.astype(BF16)
    kvl_ref[0] = k[tq - blk:tq]
    kvl_ref[1] = v[tq - blk:tq]


def _attn_core(first, attn_refs, out_ref, pk_ref, pv_ref, qh_ref, k2_ref, vt_ref, kvl_ref, st_ref, pt_ref, rd_ref):
    _, _, sk_ref, _, on_ref = attn_refs
    tq = qh_ref.shape[1]
    blk = WINDOW
    nblk = tq // blk
    heads_per_group = qh_ref.shape[0] // N_KV_HEADS
    pairs = [(jb, g) for jb in range(nblk) for g in range(N_KV_HEADS)]

    for idx, (jb, g) in enumerate(pairs):
        r0 = jb * blk
        lhs = jnp.concatenate([qh_ref[h, r0:r0 + blk, :]
                               for h in range(g * heads_per_group, (g + 1) * heads_per_group)], axis=0)
        st_ref[idx] = lax.dot_general(k2_ref[g, r0:r0 + 2 * blk, :], lhs, (((1,), (1,)), ((), ())),
                                      preferred_element_type=F32)

    ncolq = heads_per_group * blk
    kj = lax.broadcasted_iota(jnp.int32, (blk, ncolq), 0)
    qi = lax.broadcasted_iota(jnp.int32, (blk, ncolq), 1) % blk
    tri = kj <= qi
    for idx, (jb, g) in enumerate(pairs):
        s_prev = st_ref[idx, 0:blk, :]
        if jb == 0:
            s_prev = s_prev + jnp.where(first, jnp.float32(-jnp.inf), jnp.float32(0.0))
        m = jnp.where(tri, st_ref[idx, blk:2 * blk, :], s_prev)
        sk = sk_ref[g:g + 1, :]
        mx = jnp.maximum(jnp.max(m, axis=0, keepdims=True), sk)
        p = jnp.exp2(m - mx)
        den = jnp.sum(p, axis=0, keepdims=True) + jnp.exp2(sk - mx)
        pz = jnp.zeros_like(p)
        pt_ref[idx] = jnp.concatenate([jnp.where(tri, pz, p), jnp.where(tri, p, pz)], axis=0).astype(BF16)
        rd_ref[idx] = 1.0 / den

    for jb in range(nblk):
        r0 = jb * blk
        ocols = []
        for g in range(N_KV_HEADS):
            idx = jb * N_KV_HEADS + g
            vt_g = vt_ref[g * HEAD_DIM:(g + 1) * HEAD_DIM, r0:r0 + 2 * blk]
            og = jnp.dot(vt_g, pt_ref[idx], preferred_element_type=F32) * rd_ref[idx]
            for half in range(heads_per_group // 2):
                pair = jnp.concatenate([og[:, (2 * half) * blk:(2 * half + 1) * blk],
                                        og[:, (2 * half + 1) * blk:(2 * half + 2) * blk]], axis=0)
                ocols.append(pair.T)
        o = jnp.concatenate(ocols, axis=1)
        out_ref[r0:r0 + blk, :] = _rms(o, on_ref[...]).astype(out_ref.dtype)

    k2_ref[:, 0:blk, :] = k2_ref[:, tq:tq + blk, :]
    vt_ref[:, 0:blk] = vt_ref[:, tq:tq + blk]
    pk_ref[0] = kvl_ref[0]
    pv_ref[0] = kvl_ref[1]


N_FFN_IN = 7
N_LRU = 8
N_ATTN = 5


def _mixer_in_kernel(*refs, tiles_per_seq, ntile):
    x_ref, cos_ref, sin_ref = refs[0:3]
    p0 = 3
    g1_ref, wg_ref, wu_ref, wd_ref, gm_ref, winl_ref, wina_ref = refs[p0:p0 + N_FFN_IN]
    lru_refs = refs[p0 + N_FFN_IN:p0 + N_FFN_IN + N_LRU]
    attn_refs = refs[p0 + N_FFN_IN + N_LRU:p0 + N_FFN_IN + N_LRU + N_ATTN]
    o0 = p0 + N_FFN_IN + N_LRU + N_ATTN
    x1_ref, lo_ref, ao_ref, hl_ref, cs_ref, pk_ref, pv_ref = refs[o0:o0 + 7]
    (xn_ref, acc_ref, pl_ref, pa_ref, xh_ref, h_ref, xc_ref, zr_ref, zi_ref, gate_ref, ctail_ref,
     qh_ref, k2_ref, vt_ref, kvl_ref, st_ref, pt_ref, rd_ref) = refs[o0 + 7:]

    i = pl.program_id(0)

    @pl.when(i == 0)
    def _():
        for ref in (pl_ref, pa_ref, xh_ref, h_ref, xc_ref, zr_ref, zi_ref, gate_ref, ctail_ref,
                    qh_ref, k2_ref, vt_ref, kvl_ref):
            ref[...] = jnp.zeros_like(ref)

    def step(with_ffn):
        first = lax.rem(jnp.maximum(i - 1, 0), tiles_per_seq) == 0
        _lru_prep(pl_ref, first, lru_refs, xh_ref, xc_ref, zr_ref, zi_ref, gate_ref, ctail_ref)
        _attn_prep(pa_ref, cos_ref, sin_ref, attn_refs, qh_ref, k2_ref, vt_ref, kvl_ref)
        _lru_core(first, lru_refs, xc_ref, zr_ref, zi_ref, gate_ref, ctail_ref, lo_ref, hl_ref, cs_ref, h_ref)
        _attn_core(first, attn_refs, ao_ref, pk_ref, pv_ref, qh_ref, k2_ref, vt_ref, kvl_ref, st_ref, pt_ref, rd_ref)
        if with_ffn:
            _ffn_in_tile(x_ref, (g1_ref, wg_ref, wu_ref, wd_ref), gm_ref, winl_ref, wina_ref,
                         x1_ref, pl_ref, pa_ref, xn_ref, acc_ref)

    pl.when(i < ntile)(functools.partial(step, True))
    pl.when(i >= ntile)(functools.partial(step, False))


def _mixer_in(x, seq_len, cos, sin, ffn_in_p, lru_p, attn_p):
    n, d = x.shape
    tm = TOKEN_TILE
    assert seq_len % tm == 0 and n % seq_len == 0 and tm % WINDOW == 0
    nseq = n // seq_len
    tps = seq_len // tm
    ntile = n // tm
    wg, winl, wina = ffn_in_p[1], ffn_in_p[5], ffn_in_p[6]
    w2, wa = winl.shape[1], wina.shape[1]
    c = w2 // 2
    aw = attn_p[4].shape[1]
    kvw = N_KV_HEADS * HEAD_DIM
    npair = tm // WINDOW * N_KV_HEADS
    heads_per_group = aw // HEAD_DIM // N_KV_HEADS
    cur = lambda i: jnp.minimum(i, ntile - 1)
    mix = lambda i: jnp.maximum(i - 1, 0)
    params = list(ffn_in_p) + list(lru_p) + list(attn_p)
    weights = sum(p.size * p.dtype.itemsize for p in params)
    tiles = (2 * 4 * tm * 2 * d + 2 * 2 * tm * (c + aw) + 4 * tm * (w2 + wa) + tm * d * (2 + 4) + 6 * 4 * tm * c
             + npair * 2 * WINDOW * heads_per_group * WINDOW * (4 + 2))
    return pl.pallas_call(
        functools.partial(_mixer_in_kernel, tiles_per_seq=tps, ntile=ntile),
        grid=(ntile + 1,),
        in_specs=[pl.BlockSpec((tm, d), lambda i: (cur(i), 0)),
                  pl.BlockSpec((tm, V7X_LANES), lambda i: (lax.rem(mix(i), tps), 0)),
                  pl.BlockSpec((tm, V7X_LANES), lambda i: (lax.rem(mix(i), tps), 0))]
                 + [_resident(p.shape) for p in params],
        out_specs=[pl.BlockSpec((tm, d), lambda i: (cur(i), 0)),
                   pl.BlockSpec((tm, c), lambda i: (mix(i), 0)),
                   pl.BlockSpec((tm, aw), lambda i: (mix(i), 0)),
                   pl.BlockSpec((1, 1, c), lambda i: (mix(i) // tps, 0, 0)),
                   pl.BlockSpec((1, CONV_W - 1, c), lambda i: (mix(i) // tps, 0, 0)),
                   pl.BlockSpec((1, WINDOW, kvw), lambda i: (mix(i) // tps, 0, 0)),
                   pl.BlockSpec((1, WINDOW, kvw), lambda i: (mix(i) // tps, 0, 0))],
        out_shape=[jax.ShapeDtypeStruct((n, d), F32), jax.ShapeDtypeStruct((n, c), BF16),
                   jax.ShapeDtypeStruct((n, aw), BF16), jax.ShapeDtypeStruct((nseq, 1, c), F32),
                   jax.ShapeDtypeStruct((nseq, CONV_W - 1, c), F32), jax.ShapeDtypeStruct((nseq, WINDOW, kvw), F32),
                   jax.ShapeDtypeStruct((nseq, WINDOW, kvw), F32)],
        scratch_shapes=[pltpu.VMEM((tm, d), BF16), pltpu.VMEM((tm, d), F32),
                        pltpu.VMEM((tm, w2), F32), pltpu.VMEM((tm, wa), F32),
                        pltpu.VMEM((tm + V7X_SUBLANES, c), F32), pltpu.VMEM((1, c), F32),
                        pltpu.VMEM((tm, c), F32), pltpu.VMEM((tm, c), F32), pltpu.VMEM((tm, c), F32),
                        pltpu.VMEM((tm, c), F32), pltpu.VMEM((CONV_W - 1, c), F32),
                        pltpu.VMEM((2 * aw // V7X_LANES, tm, V7X_LANES), BF16),
                        pltpu.VMEM((N_KV_HEADS, WINDOW + tm, V7X_LANES), BF16),
                        pltpu.VMEM((V7X_LANES, WINDOW + tm), BF16),
                        pltpu.VMEM((2, WINDOW, kvw), F32),
                        pltpu.VMEM((npair, 2 * WINDOW, heads_per_group * WINDOW), F32),
                        pltpu.VMEM((npair, 2 * WINDOW, heads_per_group * WINDOW), BF16),
                        pltpu.VMEM((npair, 1, heads_per_group * WINDOW), F32)],
        compiler_params=pltpu.CompilerParams(dimension_semantics=("arbitrary",),
                                             vmem_limit_bytes=_vmem_limit(weights + tiles + MIXER_TEMP_BYTES)),
        name="mixer_in",
    )(x, cos, sin, *params)


def _ffn_in_kernel(x_ref, g1_ref, wg_ref, wu_ref, wd_ref, gm_ref, winl_ref, wina_ref,
                   x1_ref, pl_ref, pa_ref, xn_ref, acc_ref):
    _ffn_in_tile(x_ref, (g1_ref, wg_ref, wu_ref, wd_ref), gm_ref, winl_ref, wina_ref,
                 x1_ref, pl_ref, pa_ref, xn_ref, acc_ref)


def _ffn_in(x, g1, wg, wu, wd, gm, winl, wina):
    n, d = x.shape
    tm = min(TOKEN_TILE, n)
    wl, wa = winl.shape[1], wina.shape[1]
    row = lambda w: pl.BlockSpec((tm, w), lambda i: (i, 0))
    weights = 2 * (wg.size + wu.size + wd.size + winl.size + wina.size)
    tiles = 2 * 4 * tm * (2 * d + wl + wa) + tm * d * (2 + 4) + 4 * 4 * tm * max(FF_CHUNK, wl)
    return pl.pallas_call(
        _ffn_in_kernel,
        grid=(n // tm,),
        in_specs=[row(d), _resident(g1.shape), _resident(wg.shape), _resident(wu.shape), _resident(wd.shape),
                  _resident(gm.shape), _resident(winl.shape), _resident(wina.shape)],
        out_specs=[row(d), row(wl), row(wa)],
        out_shape=[jax.ShapeDtypeStruct((n, d), F32), jax.ShapeDtypeStruct((n, wl), F32),
                   jax.ShapeDtypeStruct((n, wa), F32)],
        scratch_shapes=[pltpu.VMEM((tm, d), BF16), pltpu.VMEM((tm, d), F32)],
        compiler_params=pltpu.CompilerParams(dimension_semantics=("arbitrary",),
                                             vmem_limit_bytes=_vmem_limit(weights + tiles + FFN_TEMP_BYTES)),
        name="ffn_in",
    )(x, g1, wg, wu, wd, gm, winl, wina)


def _out_ffn_kernel(lo_ref, ao_ref, x1_ref, wol_ref, woa_ref, g2_ref, wg_ref, wu_ref, wd_ref,
                    out_ref, xn_ref, acc_ref):
    y = jnp.dot(lo_ref[...], wol_ref[...], preferred_element_type=F32)
    y = y + jnp.dot(ao_ref[...], woa_ref[...], preferred_element_type=F32)
    x2 = x1_ref[...] + y
    out_ref[...] = x2
    xn_ref[...] = _rms(x2, g2_ref[...]).astype(BF16)
    out_ref[...] = out_ref[...] + 0.5 * _ffn(xn_ref, wg_ref, wu_ref, wd_ref, acc_ref)


def _out_ffn(lo, ao, x1, wol, woa, g2, wg, wu, wd):
    n, d = x1.shape
    tm = OUT_FFN_TILE if n % OUT_FFN_TILE == 0 else min(TOKEN_TILE, n)
    row = lambda w: pl.BlockSpec((tm, w), lambda i: (i, 0))
    weights = 2 * (wg.size + wu.size + wd.size + wol.size + woa.size)
    tiles = 2 * tm * (2 * lo.shape[1] + 2 * ao.shape[1] + 8 * d) + tm * d * (2 + 4) + 4 * 4 * tm * d
    return pl.pallas_call(
        _out_ffn_kernel,
        grid=(n // tm,),
        in_specs=[row(lo.shape[1]), row(ao.shape[1]), row(d), _resident(wol.shape), _resident(woa.shape),
                  _resident(g2.shape), _resident(wg.shape), _resident(wu.shape), _resident(wd.shape)],
        out_specs=row(d),
        out_shape=jax.ShapeDtypeStruct((n, d), F32),
        scratch_shapes=[pltpu.VMEM((tm, d), BF16), pltpu.VMEM((tm, d), F32)],
        compiler_params=pltpu.CompilerParams(dimension_semantics=("arbitrary",),
                                             vmem_limit_bytes=_vmem_limit(weights + tiles + FFN_TEMP_BYTES)),
        name="out_ffn",
    )(lo, ao, x1, wol, woa, g2, wg, wu, wd)


def _qk_prep_kernel(pa_ref, cos_ref, sin_ref, qn_ref, kn_ref, ones_ref, q_ref, k_ref):
    lanes = V7X_LANES
    aw = q_ref.shape[1]
    cos = cos_ref[...]
    sin = sin_ref[...]
    for cc in range(aw // lanes):
        q_ref[:, cc * lanes:(cc + 1) * lanes] = _head_norm_rope(
            pa_ref[:, cc * lanes:(cc + 1) * lanes], qn_ref[...], cos, sin, ones_ref) * (HEAD_DIM ** -0.5)
    k_ref[...] = _head_norm_rope(pa_ref[:, aw:aw + lanes], kn_ref[...], cos, sin, ones_ref)


def _qk_prep(pa, cos, sin, qn, kn, ones, aw):
    n = pa.shape[0]
    return pl.pallas_call(
        _qk_prep_kernel,
        out_shape=[jax.ShapeDtypeStruct((n, aw), F32), jax.ShapeDtypeStruct((n, N_KV_HEADS * HEAD_DIM), F32)],
        compiler_params=pltpu.CompilerParams(vmem_limit_bytes=_vmem_limit(SMALL_KERNEL_VMEM_BYTES)),
        name="qk_prep_sample",
    )(pa, cos, sin, qn, kn, ones)


def _attn_cached_kernel(q_ref, kn_ref, vn_ref, ck_ref, cv_ref, sk_ref, on_ref,
                        out_ref, nk_ref, nv_ref, kk_ref, vv_ref):
    sb, t, aw = q_ref.shape
    w = ck_ref.shape[3]
    lanes = V7X_LANES
    jpad = kk_ref.shape[1]

    def rows(c_ref):
        return jnp.concatenate([jnp.swapaxes(c_ref[:, g], 1, 2) for g in range(N_KV_HEADS)], axis=2)

    def store_window(n_ref, x):
        for g in range(N_KV_HEADS):
            n_ref[:, g] = jnp.swapaxes(x[:, :, g * HEAD_DIM:(g + 1) * HEAD_DIM], 1, 2)

    kk_ref[:, 0:w, :] = rows(ck_ref)
    vv_ref[:, 0:w, :] = rows(cv_ref)
    kk_ref[:, w:w + t, :] = kn_ref[...]
    vv_ref[:, w:w + t, :] = vn_ref[...]
    kk_ref[:, w + t:jpad, :] = jnp.zeros((sb, jpad - w - t, lanes), F32)
    vv_ref[:, w + t:jpad, :] = jnp.zeros((sb, jpad - w - t, lanes), F32)
    store_window(nk_ref, kk_ref[:, t:t + w, :])
    store_window(nv_ref, vv_ref[:, t:t + w, :])

    kk = kk_ref[...]
    vv = vv_ref[...]
    k_sw = pltpu.roll(kk, HEAD_DIM, 2)
    v_sw = pltpu.roll(vv, HEAD_DIM, 2)
    lane_k = lax.broadcasted_iota(jnp.int32, kk.shape, 2) < HEAD_DIM
    lane_q = lax.broadcasted_iota(jnp.int32, (sb, t, lanes), 2) < HEAD_DIM
    k2 = (jnp.where(lane_k, kk, k_sw).astype(BF16), jnp.where(lane_k, k_sw, kk).astype(BF16))
    v2 = (vv.astype(BF16), v_sw.astype(BF16))

    qcols = [q_ref[:, :, cc * lanes:(cc + 1) * lanes] for cc in range(aw // lanes)]
    tq = lax.broadcasted_iota(jnp.int32, (sb, 4 * t, jpad), 1) % t
    j = lax.broadcasted_iota(jnp.int32, (sb, 4 * t, jpad), 2)
    valid = jnp.logical_and(j > tq, j <= tq + w)
    neg_inf = jnp.float32(-jnp.inf)

    ocols = []
    for g in range(N_KV_HEADS):
        lhs = _group_lhs(qcols, g, lane_q, 1)
        s = jnp.einsum('bqd,bjd->bqj', lhs, k2[g], preferred_element_type=F32)
        s = jnp.where(valid, s, neg_inf)
        heads = (4 * g, 4 * g + 2, 4 * g + 1, 4 * g + 3)
        sk = jnp.concatenate([jnp.broadcast_to(sk_ref[h:h + 1, 0:1].reshape(1, 1, 1), (sb, t, 1)) for h in heads],
                             axis=1)
        mx = jnp.maximum(jnp.max(s, axis=-1, keepdims=True), sk)
        p = jnp.exp(s - mx)
        den = jnp.sum(p, axis=-1, keepdims=True) + jnp.exp(sk - mx)
        pb = p.astype(BF16)
        o_even = jnp.einsum('bqj,bjd->bqd', pb[:, 0:2 * t], v2[g], preferred_element_type=F32) / den[:, 0:2 * t]
        o_odd = jnp.einsum('bqj,bjd->bqd', pb[:, 2 * t:4 * t], v2[1 - g], preferred_element_type=F32) / den[:, 2 * t:4 * t]
        for half in range(2):
            ocols.append(jnp.where(lane_q, o_even[:, half * t:(half + 1) * t], o_odd[:, half * t:(half + 1) * t]))
    o = jnp.concatenate(ocols, axis=2)
    out_ref[...] = _rms(o, on_ref[...].reshape(1, 1, aw)).astype(out_ref.dtype)


def _attn_cached(q3, kn3, vn3, ck, cv, sk, on):
    s, t, aw = q3.shape
    nkv, hd, w = ck.shape[1:]
    kvw = nkv * hd
    sb = min(SAMPLE_SEQ_TILE, s)
    jpad = -(-(w + t) // V7X_SUBLANES) * V7X_SUBLANES
    seq = lambda a, c: pl.BlockSpec((sb, a, c), lambda i: (i, 0, 0))
    win = pl.BlockSpec((sb, nkv, hd, w), lambda i: (i, 0, 0, 0))
    return pl.pallas_call(
        _attn_cached_kernel,
        grid=(s // sb,),
        in_specs=[seq(t, aw), seq(t, kvw), seq(t, kvw), win, win, _resident(sk.shape), _resident(on.shape)],
        out_specs=[seq(t, aw), win, win],
        out_shape=[jax.ShapeDtypeStruct((s, t, aw), BF16), jax.ShapeDtypeStruct((s, nkv, hd, w), F32),
                   jax.ShapeDtypeStruct((s, nkv, hd, w), F32)],
        scratch_shapes=[pltpu.VMEM((sb, jpad, kvw), F32), pltpu.VMEM((sb, jpad, kvw), F32)],
        compiler_params=pltpu.CompilerParams(dimension_semantics=("arbitrary",),
                                             vmem_limit_bytes=_vmem_limit(SMALL_KERNEL_VMEM_BYTES)),
        name="attn_sample",
    )(q3, kn3, vn3, ck, cv, sk, on)


ROPE_SPLIT = 128


def _rope_tables(start, n):
    half = HEAD_DIM // 2
    lane = jnp.arange(V7X_LANES, dtype=jnp.int32)
    inv_freq = ROPE_THETA ** (-(lane % half).astype(F32) / half)
    sign = jnp.where((lane % HEAD_DIM) < half, -1.0, 1.0).astype(F32)
    if n % ROPE_SPLIT:
        pos = start + jnp.arange(n, dtype=jnp.int32)
        ang = pos.astype(F32)[:, None] * inv_freq[None, :]
        return jnp.cos(ang), jnp.sin(ang) * sign[None, :]
    lo = jnp.arange(ROPE_SPLIT, dtype=jnp.int32)
    hi = start + ROPE_SPLIT * jnp.arange(n // ROPE_SPLIT, dtype=jnp.int32)
    a_hi = hi.astype(F32)[:, None] * inv_freq[None, :]
    a_lo = lo.astype(F32)[:, None] * inv_freq[None, :]
    ch, sh = jnp.cos(a_hi)[:, None, :], jnp.sin(a_hi)[:, None, :]
    cl, sl = jnp.cos(a_lo)[None, :, :], jnp.sin(a_lo)[None, :, :]
    cos = (ch * cl - sh * sl).reshape(n, V7X_LANES)
    sin = (sh * cl + ch * sl).reshape(n, V7X_LANES)
    return cos, sin * sign[None, :]


def _block_diag(w):
    n, c, d = w.shape
    eye = jnp.eye(n, dtype=w.dtype)
    return (w[:, :, None, :] * eye[:, None, :, None]).reshape(n * c, n * d)


def _row(v):
    return v.reshape(1, -1).astype(F32)


def kernel(x_prompt, x_sample, state_lru_h, state_conv, cache_k, cache_v, ffn1_norm, ffn1_w_gate, ffn1_w_up, ffn1_w_down, mix_norm, w_in, conv_w, conv_b, lru_wa, lru_ba, lru_wx, lru_bx, lru_lambda, q_norm, k_norm, attn_sinks, lru_out_norm, attn_out_norm, w_out, ffn2_norm, ffn2_w_gate, ffn2_w_up, ffn2_w_down):
    depth = w_in.shape[0]
    b, t, d = x_prompt.shape
    s, ts, _ = x_sample.shape
    c = lru_lambda.shape[1]
    aw = attn_out_norm.shape[1]
    kvw = N_KV_HEADS * HEAD_DIM
    win = cache_k.shape[2]
    assert t >= WINDOW and win == WINDOW
    assert GATE_DIAG_TILE % lru_wa.shape[2] == 0 and c % min(GATE_DIAG_TILE, c) == 0

    cos_p, sin_p = _rope_tables(0, t)
    cos_s, sin_s = _rope_tables(PAST_LEN, ts)
    cos_s, sin_s = jnp.tile(cos_s, (s, 1)), jnp.tile(sin_s, (s, 1))
    ones = _block_diag(jnp.full((V7X_LANES // HEAD_DIM, HEAD_DIM, HEAD_DIM), 1.0 / HEAD_DIM, BF16))
    log2e = 1.0 / jnp.log(jnp.float32(2.0))

    yp = x_prompt.reshape(b * t, d)
    ys = x_sample.reshape(s * ts, d)
    outs = [[] for _ in range(8)]
    for l in range(depth):
        ffn1 = (_row(ffn1_norm[l]), (0.5 * ffn1_w_gate[l]).astype(BF16), ffn1_w_up[l].astype(BF16),
                ffn1_w_down[l].astype(BF16))
        ffn2 = (_row(ffn2_norm[l]), (0.5 * ffn2_w_gate[l]).astype(BF16), ffn2_w_up[l].astype(BF16),
                ffn2_w_down[l].astype(BF16))
        win_b = w_in[l].astype(BF16)
        ffn_in_p = ffn1 + (_row(mix_norm[l]), win_b[:, :2 * c], win_b[:, 2 * c:])
        wout_b = w_out[l].astype(BF16)
        wol, woa = wout_b[:c], wout_b[c:]
        lru_p = (conv_w[l], _row(conv_b[l]), (0.5 * _block_diag(lru_wa[l])).astype(BF16), _row(0.5 * lru_ba[l]),
                 (0.5 * _block_diag(lru_wx[l])).astype(BF16), _row(0.5 * lru_bx[l]), _row(lru_lambda[l]),
                 _row(lru_out_norm[l]))
        reps = V7X_LANES // HEAD_DIM
        qn = _row(jnp.tile(q_norm[l], reps))
        kn = _row(jnp.tile(k_norm[l], reps))
        sk = jnp.broadcast_to(attn_sinks[l].astype(F32)[:, None], (attn_sinks.shape[1], V7X_LANES))
        qn_p = qn * (HEAD_DIM ** -0.5 * log2e)
        sk_rows = jnp.repeat(attn_sinks[l].astype(F32).reshape(N_KV_HEADS, -1), WINDOW, axis=1) * log2e
        a_on = _row(attn_out_norm[l])

        x1, lo, ao, hl, cs, pk, pv = _mixer_in(yp, t, cos_p, sin_p, ffn_in_p, lru_p, (qn_p, kn, sk_rows, ones, a_on))
        yp = _out_ffn(lo, ao, x1, wol, woa, *ffn2)
        outs[0].append(hl.reshape(b, c))
        outs[1].append(cs)
        outs[2].append(pk.reshape(b, WINDOW, N_KV_HEADS, HEAD_DIM))
        outs[3].append(pv.reshape(b, WINDOW, N_KV_HEADS, HEAD_DIM))

        x1s, pl_s, pa_s = _ffn_in(ys, *ffn_in_p)
        x_tm = pl_s.reshape(s, ts, 2 * c).transpose(1, 0, 2)
        cst_tm = state_conv[l].transpose(1, 0, 2)
        lo_tm, hls, cs_tm = _lru_sample(x_tm, cst_tm, state_lru_h[l], *lru_p)
        q_s, k_s = _qk_prep(pa_s, cos_s, sin_s, qn, kn, ones, aw)
        v_s = pa_s[:, aw + kvw:]
        aos, nk, nv = _attn_cached(q_s.reshape(s, ts, aw), k_s.reshape(s, ts, kvw), v_s.reshape(s, ts, kvw),
                                   jnp.transpose(cache_k[l], (0, 2, 3, 1)), jnp.transpose(cache_v[l], (0, 2, 3, 1)),
                                   sk, a_on)
        ys = _out_ffn(lo_tm.transpose(1, 0, 2).reshape(s * ts, c), aos.reshape(s * ts, aw), x1s, wol, woa, *ffn2)
        outs[4].append(hls)
        outs[5].append(cs_tm.transpose(1, 0, 2))
        outs[6].append(jnp.transpose(nk, (0, 3, 1, 2)))
        outs[7].append(jnp.transpose(nv, (0, 3, 1, 2)))

    st = [jnp.stack(o) for o in outs]
    return (yp.reshape(b, t, d), ys.reshape(s, ts, d), st[0], st[1], st[2], st[3], st[4], st[5], st[6], st[7])
```

```python
import functools

import jax
import jax.numpy as jnp
from jax import lax
from jax.experimental import pallas as pl
from jax.experimental.pallas import tpu as pltpu

F32 = jnp.float32
BF16 = jnp.bfloat16

NORM_EPS = 1e-6
LRU_C = 8.0
CONV_W = 4
HEAD_DIM = 64
N_KV_HEADS = 2
WINDOW = 128
ROPE_THETA = 10000.0
PAST_LEN = 16384

V7X_LANES = 128
V7X_SUBLANES = 8
V7X_MXU_DIM = 256
V7X_VMEM_BYTES = 64 * 1024 * 1024

TOKEN_TILE = 512
OUT_FFN_TILE = 1024
SAMPLE_SEQ_TILE = 32
FF_CHUNK = V7X_MXU_DIM
GATE_DIAG_TILE = V7X_MXU_DIM


MIB = 1024 * 1024
VMEM_KEEP_FREE_BYTES = 4 * MIB
FFN_TEMP_BYTES = 8 * MIB
MIXER_TEMP_BYTES = 20 * MIB
SMALL_KERNEL_VMEM_BYTES = 40 * MIB


def _vmem_limit(nbytes):
    return int(min(nbytes, V7X_VMEM_BYTES - VMEM_KEEP_FREE_BYTES))


def _resident(shape):
    nd = len(shape)
    return pl.BlockSpec(shape, lambda *_: (0,) * nd, pipeline_mode=pl.Buffered(1))


def _rms(x, g):
    ms = jnp.mean(x * x, axis=-1, keepdims=True)
    return (x * lax.rsqrt(ms + NORM_EPS)) * g


def _ffn(xn_ref, wg_ref, wu_ref, wd_ref, acc_ref):
    acc_ref[...] = jnp.zeros_like(acc_ref)
    for c0 in range(0, wg_ref.shape[1], FF_CHUNK):
        xn = xn_ref[...]
        h = jnp.dot(xn, wg_ref[:, c0:c0 + FF_CHUNK], preferred_element_type=F32)
        u = jnp.dot(xn, wu_ref[:, c0:c0 + FF_CHUNK], preferred_element_type=F32)
        a = ((h + h * jnp.tanh(h)) * u).astype(BF16)
        acc_ref[...] += jnp.dot(a, wd_ref[c0:c0 + FF_CHUNK, :], preferred_element_type=F32)
    return acc_ref[...]


def _ffn_in_tile(x_ref, ffn_refs, gm_ref, winl_ref, wina_ref, x1_ref, pl_ref, pa_ref, xn_ref, acc_ref):
    g1_ref, wg_ref, wu_ref, wd_ref = ffn_refs
    xn_ref[...] = _rms(x_ref[...], g1_ref[...]).astype(BF16)
    x1 = x_ref[...] + 0.5 * _ffn(xn_ref, wg_ref, wu_ref, wd_ref, acc_ref)
    x1_ref[...] = x1
    xn_ref[...] = _rms(x1, gm_ref[...]).astype(BF16)
    pl_ref[...] = jnp.dot(xn_ref[...], winl_ref[...], preferred_element_type=F32)
    pa_ref[...] = jnp.dot(xn_ref[...], wina_ref[...], preferred_element_type=F32)


def _lru_preact(xc, gate_w):
    wa_ref, ba_ref, wx_ref, bx_ref = gate_w
    xcb = xc.astype(BF16)
    c = xc.shape[-1]
    dt = min(GATE_DIAG_TILE, c)

    def blockdiag_dot(w_ref):
        return jnp.concatenate([jnp.dot(xcb[:, s:s + dt], w_ref[s:s + dt, s:s + dt], preferred_element_type=F32)
                                for s in range(0, c, dt)], axis=1)

    zr = blockdiag_dot(wa_ref) + ba_ref[...]
    zi = blockdiag_dot(wx_ref) + bx_ref[...]
    return zr, zi


def _lru_terms(xc, zr, zi, lam_ref):
    i = 0.5 * jnp.tanh(zi) + 0.5
    half_c = (-0.5 * LRU_C) * jax.nn.softplus(-lam_ref[...])
    log_a = half_c * jnp.tanh(zr) + half_c
    a = jnp.exp(log_a)
    w = -jnp.tanh(log_a) * (a * a + 1.0)
    root = jnp.where(w > 0.0, w * lax.rsqrt(w), 0.0)
    u = root * (i * xc)
    return a, u


def _lru_prep(p_ref, first, lru_refs, xh_ref, xc_ref, zr_ref, zi_ref, gate_ref, ctail_ref):
    cw_ref, cb_ref, wa_ref, ba_ref, wx_ref, bx_ref, _, _ = lru_refs
    tt = p_ref.shape[0]
    c = xc_ref.shape[1]
    hist = V7X_SUBLANES

    xh_ref[0:hist, :] = jnp.where(first, jnp.zeros((hist, c), F32), xh_ref[0:hist, :])
    x = p_ref[:, 0:c]
    xh_ref[hist:hist + tt, :] = x
    xg = xh_ref[...].reshape(tt // hist + 1, hist, c)
    sub = lax.broadcasted_iota(jnp.int32, (tt // hist, hist, c), 1)
    w = cw_ref[...]
    xc = cb_ref[...].reshape(1, 1, c)
    for j in range(CONV_W - 1):
        k = CONV_W - 1 - j
        rolled = pltpu.roll(xg, k, 1)
        xc = xc + jnp.where(sub >= k, rolled[1:], rolled[:-1]) * w[j:j + 1, :].reshape(1, 1, c)
    xc = (xc + xg[1:] * w[CONV_W - 1:CONV_W, :].reshape(1, 1, c)).reshape(tt, c)
    xh_ref[0:hist, :] = x[tt - hist:tt, :]
    ctail_ref[...] = x[tt - (CONV_W - 1):tt, :]

    xc_ref[...] = xc
    zr, zi = _lru_preact(xc, (wa_ref, ba_ref, wx_ref, bx_ref))
    zr_ref[...] = zr
    zi_ref[...] = zi
    gate_ref[...] = p_ref[:, c:2 * c]


def _lru_core(first, lru_refs, xc_ref, zr_ref, zi_ref, gate_ref, ctail_ref, out_ref, hl_ref, cs_ref, h_ref):
    lam_ref, on_ref = lru_refs[6], lru_refs[7]
    tt, c = xc_ref.shape

    a, u = _lru_terms(xc_ref[...], zr_ref[...], zi_ref[...], lam_ref)

    groups = tt // V7X_SUBLANES
    a3 = a.reshape(groups, V7X_SUBLANES, c)
    u3 = u.reshape(groups, V7X_SUBLANES, c)
    sub = lax.broadcasted_iota(jnp.int32, a3.shape, 1)
    k = 1
    while k < V7X_SUBLANES:
        m = sub >= k
        a_prev = pltpu.roll(a3, k, 1)
        u_prev = pltpu.roll(u3, k, 1)
        u3 = jnp.where(m, a3 * u_prev + u3, u3)
        a3 = jnp.where(m, a3 * a_prev, a3)
        k *= 2

    h = jnp.where(first, jnp.zeros((1, c), F32), h_ref[...])
    hs = []
    for g in range(groups):
        hr = a3[g] * h + u3[g]
        hs.append(hr)
        h = hr[V7X_SUBLANES - 1:V7X_SUBLANES, :]
    h_ref[...] = h

    y = jnp.concatenate(hs, axis=0) * jax.nn.gelu(gate_ref[...])
    out_ref[...] = _rms(y, on_ref[...]).astype(out_ref.dtype)
    hl_ref[0] = h
    cs_ref[0] = ctail_ref[...]


def _lru_sample_kernel(x_ref, cst_ref, h0_ref, cw_ref, cb_ref, wa_ref, ba_ref, wx_ref, bx_ref, lam_ref, on_ref,
                       out_ref, hl_ref, cs_ref):
    t = x_ref.shape[0]
    c = out_ref.shape[2]
    w = cw_ref[...]
    rows = [cst_ref[j] for j in range(CONV_W - 1)] + [x_ref[j, :, 0:c] for j in range(t)]
    h = h0_ref[...]
    for s in range(t):
        xc = cb_ref[...]
        for j in range(CONV_W):
            xc = xc + rows[s + j] * w[j:j + 1, :]
        zr, zi = _lru_preact(xc, (wa_ref, ba_ref, wx_ref, bx_ref))
        a, u = _lru_terms(xc, zr, zi, lam_ref)
        h = a * h + u
        y = h * jax.nn.gelu(x_ref[s, :, c:2 * c])
        out_ref[s] = _rms(y, on_ref[...]).astype(out_ref.dtype)
    hl_ref[...] = h
    for j in range(CONV_W - 1):
        cs_ref[j] = rows[t + j]


def _lru_sample(x_tm, cst_tm, h0, cw, cb, wa, ba, wx, bx, lam, on):
    t, s, w2 = x_tm.shape
    c = w2 // 2
    return pl.pallas_call(
        _lru_sample_kernel,
        out_shape=[jax.ShapeDtypeStruct((t, s, c), BF16), jax.ShapeDtypeStruct((s, c), F32),
                   jax.ShapeDtypeStruct((CONV_W - 1, s, c), F32)],
        compiler_params=pltpu.CompilerParams(vmem_limit_bytes=_vmem_limit(SMALL_KERNEL_VMEM_BYTES)),
        name="lru_sample",
    )(x_tm, cst_tm, h0, cw, cb, wa, ba, wx, bx, lam, on)


def _head_norm_rope(xc, gain, cos, sin, ones_ref):
    x2 = xc * xc
    hi = x2.astype(BF16)
    lo = (x2 - hi.astype(F32)).astype(BF16)
    ss = jnp.dot(hi, ones_ref[...], preferred_element_type=F32) + jnp.dot(lo, ones_ref[...], preferred_element_type=F32)
    xn = (xc * lax.rsqrt(ss + NORM_EPS)) * gain
    lane = lax.broadcasted_iota(jnp.int32, xn.shape, xn.ndim - 1)
    first_half = (lane & (HEAD_DIM // 2)) == 0
    ax = xn.ndim - 1
    swapped = jnp.where(first_half, pltpu.roll(xn, V7X_LANES - HEAD_DIM // 2, ax), pltpu.roll(xn, HEAD_DIM // 2, ax))
    return xn * cos + swapped * sin


def _group_lhs(qcols, g, lane_lo, axis):
    zero = jnp.zeros_like(qcols[0])
    parts = [jnp.where(lane_lo, qcols[2 * g], zero), jnp.where(lane_lo, qcols[2 * g + 1], zero),
             jnp.where(lane_lo, zero, qcols[2 * g]), jnp.where(lane_lo, zero, qcols[2 * g + 1])]
    return jnp.concatenate(parts, axis=axis).astype(BF16)


def _attn_prep(pa_ref, cos_ref, sin_ref, attn_refs, qh_ref, k2_ref, vt_ref, kvl_ref):
    qn_ref, kn_ref, _, ones_ref, _ = attn_refs
    tq = pa_ref.shape[0]
    blk = WINDOW
    lanes = V7X_LANES
    ncol = qh_ref.shape[0] // 2
    aw = ncol * lanes

    cos = cos_ref[...]
    sin = sin_ref[...]
    cols = [pa_ref[:, cc * lanes:(cc + 1) * lanes] for cc in range(ncol + 1)]
    sq = [x * x for x in cols]
    hi = [x.astype(BF16) for x in sq]
    lo = [(x - h.astype(F32)).astype(BF16) for x, h in zip(sq, hi)]
    ss = (jnp.dot(jnp.concatenate(hi, axis=0), ones_ref[...], preferred_element_type=F32)
          + jnp.dot(jnp.concatenate(lo, axis=0), ones_ref[...], preferred_element_type=F32))
    lane = lax.broadcasted_iota(jnp.int32, (tq, lanes), 1)
    first_half = (lane & (HEAD_DIM // 2)) == 0
    lane_lo = lane < HEAD_DIM
    rot = []
    for cc, x in enumerate(cols):
        gain = qn_ref[...] if cc < ncol else kn_ref[...]
        xn = (x * lax.rsqrt(ss[cc * tq:(cc + 1) * tq] + NORM_EPS)) * gain
        swapped = jnp.where(first_half, pltpu.roll(xn, lanes - HEAD_DIM // 2, 1), pltpu.roll(xn, HEAD_DIM // 2, 1))
        rot.append(xn * cos + swapped * sin)
    k = rot[ncol]
    v = pa_ref[:, aw + lanes:aw + 2 * lanes]

    zero = jnp.zeros((tq, lanes), F32)
    for cc in range(ncol):
        qh_ref[2 * cc] = jnp.where(lane_lo, rot[cc], zero).astype(BF16)
        qh_ref[2 * cc + 1] = jnp.where(lane_lo, zero, rot[cc]).astype(BF16)

    k_sw = pltpu.roll(k, HEAD_DIM, 1)
    k2_ref[0, blk:blk + tq, :] = jnp.where(lane_lo, k, k_sw).astype(BF16)
    k2_ref[1, blk:blk + tq, :] = jnp.where(lane_lo, k_sw, k).astype(BF16)
    for jb in range(tq // blk):
        vt_ref[:, (jb + 1) * blk:(jb + 2) * blk] = v[jb * blk:(jb + 1) * blk].T.astype(BF16)
    kvl_ref[0] = k[tq - blk:tq]
    kvl_ref[1] = v[tq - blk:tq]


def _attn_core(first, attn_refs, out_ref, pk_ref, pv_ref, qh_ref, k2_ref, vt_ref, kvl_ref, st_ref, pt_ref, rd_ref):
    _, _, sk_ref, _, on_ref = attn_refs
    tq = qh_ref.shape[1]
    blk = WINDOW
    nblk = tq // blk
    heads_per_group = qh_ref.shape[0] // N_KV_HEADS
    pairs = [(jb, g) for jb in range(nblk) for g in range(N_KV_HEADS)]

    for idx, (jb, g) in enumerate(pairs):
        r0 = jb * blk
        lhs = jnp.concatenate([qh_ref[h, r0:r0 + blk, :]
                               for h in range(g * heads_per_group, (g + 1) * heads_per_group)], axis=0)
        st_ref[idx] = lax.dot_general(k2_ref[g, r0:r0 + 2 * blk, :], lhs, (((1,), (1,)), ((), ())),
                                      preferred_element_type=F32)

    ncolq = heads_per_group * blk
    kj = lax.broadcasted_iota(jnp.int32, (blk, ncolq), 0)
    qi = lax.broadcasted_iota(jnp.int32, (blk, ncolq), 1) % blk
    tri = kj <= qi
    for idx, (jb, g) in enumerate(pairs):
        s_prev = st_ref[idx, 0:blk, :]
        if jb == 0:
            s_prev = s_prev + jnp.where(first, jnp.float32(-jnp.inf), jnp.float32(0.0))
        m = jnp.where(tri, st_ref[idx, blk:2 * blk, :], s_prev)
        sk = sk_ref[g:g + 1, :]
        mx = jnp.maximum(jnp.max(m, axis=0, keepdims=True), sk)
        p = jnp.exp2(m - mx)
        den = jnp.sum(p, axis=0, keepdims=True) + jnp.exp2(sk - mx)
        pz = jnp.zeros_like(p)
        pt_ref[idx] = jnp.concatenate([jnp.where(tri, pz, p), jnp.where(tri, p, pz)], axis=0).astype(BF16)
        rd_ref[idx] = 1.0 / den

    for jb in range(nblk):
        r0 = jb * blk
        ocols = []
        for g in range(N_KV_HEADS):
            idx = jb * N_KV_HEADS + g
            vt_g = vt_ref[g * HEAD_DIM:(g + 1) * HEAD_DIM, r0:r0 + 2 * blk]
            og = jnp.dot(vt_g, pt_ref[idx], preferred_element_type=F32) * rd_ref[idx]
            for half in range(heads_per_group // 2):
                pair = jnp.concatenate([og[:, (2 * half) * blk:(2 * half + 1) * blk],
                                        og[:, (2 * half + 1) * blk:(2 * half + 2) * blk]], axis=0)
                ocols.append(pair.T)
        o = jnp.concatenate(ocols, axis=1)
        out_ref[r0:r0 + blk, :] = _rms(o, on_ref[...]).astype(out_ref.dtype)

    k2_ref[:, 0:blk, :] = k2_ref[:, tq:tq + blk, :]
    vt_ref[:, 0:blk] = vt_ref[:, tq:tq + blk]
    pk_ref[0] = kvl_ref[0]
    pv_ref[0] = kvl_ref[1]


N_FFN_IN = 7
N_LRU = 8
N_ATTN = 5


def _mixer_in_kernel(*refs, tiles_per_seq, ntile):
    x_ref, cos_ref, sin_ref = refs[0:3]
    p0 = 3
    g1_ref, wg_ref, wu_ref, wd_ref, gm_ref, winl_ref, wina_ref = refs[p0:p0 + N_FFN_IN]
    lru_refs = refs[p0 + N_FFN_IN:p0 + N_FFN_IN + N_LRU]
    attn_refs = refs[p0 + N_FFN_IN + N_LRU:p0 + N_FFN_IN + N_LRU + N_ATTN]
    o0 = p0 + N_FFN_IN + N_LRU + N_ATTN
    x1_ref, lo_ref, ao_ref, hl_ref, cs_ref, pk_ref, pv_ref = refs[o0:o0 + 7]
    (xn_ref, acc_ref, pl_ref, pa_ref, xh_ref, h_ref, xc_ref, zr_ref, zi_ref, gate_ref, ctail_ref,
     qh_ref, k2_ref, vt_ref, kvl_ref, st_ref, pt_ref, rd_ref) = refs[o0 + 7:]

    i = pl.program_id(0)

    @pl.when(i == 0)
    def _():
        for ref in (pl_ref, pa_ref, xh_ref, h_ref, xc_ref, zr_ref, zi_ref, gate_ref, ctail_ref,
                    qh_ref, k2_ref, vt_ref, kvl_ref):
            ref[...] = jnp.zeros_like(ref)

    def step(with_ffn):
        first = lax.rem(jnp.maximum(i - 1, 0), tiles_per_seq) == 0
        _lru_prep(pl_ref, first, lru_refs, xh_ref, xc_ref, zr_ref, zi_ref, gate_ref, ctail_ref)
        _attn_prep(pa_ref, cos_ref, sin_ref, attn_refs, qh_ref, k2_ref, vt_ref, kvl_ref)
        _lru_core(first, lru_refs, xc_ref, zr_ref, zi_ref, gate_ref, ctail_ref, lo_ref, hl_ref, cs_ref, h_ref)
        _attn_core(first, attn_refs, ao_ref, pk_ref, pv_ref, qh_ref, k2_ref, vt_ref, kvl_ref, st_ref, pt_ref, rd_ref)
        if with_ffn:
            _ffn_in_tile(x_ref, (g1_ref, wg_ref, wu_ref, wd_ref), gm_ref, winl_ref, wina_ref,
                         x1_ref, pl_ref, pa_ref, xn_ref, acc_ref)

    pl.when(i < ntile)(functools.partial(step, True))
    pl.when(i >= ntile)(functools.partial(step, False))


def _mixer_in(x, seq_len, cos, sin, ffn_in_p, lru_p, attn_p):
    n, d = x.shape
    tm = TOKEN_TILE
    assert seq_len % tm == 0 and n % seq_len == 0 and tm % WINDOW == 0
    nseq = n // seq_len
    tps = seq_len // tm
    ntile = n // tm
    wg, winl, wina = ffn_in_p[1], ffn_in_p[5], ffn_in_p[6]
    w2, wa = winl.shape[1], wina.shape[1]
    c = w2 // 2
    aw = attn_p[4].shape[1]
    kvw = N_KV_HEADS * HEAD_DIM
    npair = tm // WINDOW * N_KV_HEADS
    heads_per_group = aw // HEAD_DIM // N_KV_HEADS
    cur = lambda i: jnp.minimum(i, ntile - 1)
    mix = lambda i: jnp.maximum(i - 1, 0)
    params = list(ffn_in_p) + list(lru_p) + list(attn_p)
    weights = sum(p.size * p.dtype.itemsize for p in params)
    tiles = (2 * 4 * tm * 2 * d + 2 * 2 * tm * (c + aw) + 4 * tm * (w2 + wa) + tm * d * (2 + 4) + 6 * 4 * tm * c
             + npair * 2 * WINDOW * heads_per_group * WINDOW * (4 + 2))
    return pl.pallas_call(
        functools.partial(_mixer_in_kernel, tiles_per_seq=tps, ntile=ntile),
        grid=(ntile + 1,),
        in_specs=[pl.BlockSpec((tm, d), lambda i: (cur(i), 0)),
                  pl.BlockSpec((tm, V7X_LANES), lambda i: (lax.rem(mix(i), tps), 0)),
                  pl.BlockSpec((tm, V7X_LANES), lambda i: (lax.rem(mix(i), tps), 0))]
                 + [_resident(p.shape) for p in params],
        out_specs=[pl.BlockSpec((tm, d), lambda i: (cur(i), 0)),
                   pl.BlockSpec((tm, c), lambda i: (mix(i), 0)),
                   pl.BlockSpec((tm, aw), lambda i: (mix(i), 0)),
                   pl.BlockSpec((1, 1, c), lambda i: (mix(i) // tps, 0, 0)),
                   pl.BlockSpec((1, CONV_W - 1, c), lambda i: (mix(i) // tps, 0, 0)),
                   pl.BlockSpec((1, WINDOW, kvw), lambda i: (mix(i) // tps, 0, 0)),
                   pl.BlockSpec((1, WINDOW, kvw), lambda i: (mix(i) // tps, 0, 0))],
        out_shape=[jax.ShapeDtypeStruct((n, d), F32), jax.ShapeDtypeStruct((n, c), BF16),
                   jax.ShapeDtypeStruct((n, aw), BF16), jax.ShapeDtypeStruct((nseq, 1, c), F32),
                   jax.ShapeDtypeStruct((nseq, CONV_W - 1, c), F32), jax.ShapeDtypeStruct((nseq, WINDOW, kvw), F32),
                   jax.ShapeDtypeStruct((nseq, WINDOW, kvw), F32)],
        scratch_shapes=[pltpu.VMEM((tm, d), BF16), pltpu.VMEM((tm, d), F32),
                        pltpu.VMEM((tm, w2), F32), pltpu.VMEM((tm, wa), F32),
                        pltpu.VMEM((tm + V7X_SUBLANES, c), F32), pltpu.VMEM((1, c), F32),
                        pltpu.VMEM((tm, c), F32), pltpu.VMEM((tm, c), F32), pltpu.VMEM((tm, c), F32),
                        pltpu.VMEM((tm, c), F32), pltpu.VMEM((CONV_W - 1, c), F32),
                        pltpu.VMEM((2 * aw // V7X_LANES, tm, V7X_LANES), BF16),
                        pltpu.VMEM((N_KV_HEADS, WINDOW + tm, V7X_LANES), BF16),
                        pltpu.VMEM((V7X_LANES, WINDOW + tm), BF16),
                        pltpu.VMEM((2, WINDOW, kvw), F32),
                        pltpu.VMEM((npair, 2 * WINDOW, heads_per_group * WINDOW), F32),
                        pltpu.VMEM((npair, 2 * WINDOW, heads_per_group * WINDOW), BF16),
                        pltpu.VMEM((npair, 1, heads_per_group * WINDOW), F32)],
        compiler_params=pltpu.CompilerParams(dimension_semantics=("arbitrary",),
                                             vmem_limit_bytes=_vmem_limit(weights + tiles + MIXER_TEMP_BYTES)),
        name="mixer_in",
    )(x, cos, sin, *params)


def _ffn_in_kernel(x_ref, g1_ref, wg_ref, wu_ref, wd_ref, gm_ref, winl_ref, wina_ref,
                   x1_ref, pl_ref, pa_ref, xn_ref, acc_ref):
    _ffn_in_tile(x_ref, (g1_ref, wg_ref, wu_ref, wd_ref), gm_ref, winl_ref, wina_ref,
                 x1_ref, pl_ref, pa_ref, xn_ref, acc_ref)


def _ffn_in(x, g1, wg, wu, wd, gm, winl, wina):
    n, d = x.shape
    tm = min(TOKEN_TILE, n)
    wl, wa = winl.shape[1], wina.shape[1]
    row = lambda w: pl.BlockSpec((tm, w), lambda i: (i, 0))
    weights = 2 * (wg.size + wu.size + wd.size + winl.size + wina.size)
    tiles = 2 * 4 * tm * (2 * d + wl + wa) + tm * d * (2 + 4) + 4 * 4 * tm * max(FF_CHUNK, wl)
    return pl.pallas_call(
        _ffn_in_kernel,
        grid=(n // tm,),
        in_specs=[row(d), _resident(g1.shape), _resident(wg.shape), _resident(wu.shape), _resident(wd.shape),
                  _resident(gm.shape), _resident(winl.shape), _resident(wina.shape)],
        out_specs=[row(d), row(wl), row(wa)],
        out_shape=[jax.ShapeDtypeStruct((n, d), F32), jax.ShapeDtypeStruct((n, wl), F32),
                   jax.ShapeDtypeStruct((n, wa), F32)],
        scratch_shapes=[pltpu.VMEM((tm, d), BF16), pltpu.VMEM((tm, d), F32)],
        compiler_params=pltpu.CompilerParams(dimension_semantics=("arbitrary",),
                                             vmem_limit_bytes=_vmem_limit(weights + tiles + FFN_TEMP_BYTES)),
        name="ffn_in",
    )(x, g1, wg, wu, wd, gm, winl, wina)


def _out_ffn_kernel(lo_ref, ao_ref, x1_ref, wol_ref, woa_ref, g2_ref, wg_ref, wu_ref, wd_ref,
                    out_ref, xn_ref, acc_ref):
    y = jnp.dot(lo_ref[...], wol_ref[...], preferred_element_type=F32)
    y = y + jnp.dot(ao_ref[...], woa_ref[...], preferred_element_type=F32)
    x2 = x1_ref[...] + y
    out_ref[...] = x2
    xn_ref[...] = _rms(x2, g2_ref[...]).astype(BF16)
    out_ref[...] = out_ref[...] + 0.5 * _ffn(xn_ref, wg_ref, wu_ref, wd_ref, acc_ref)


def _out_ffn(lo, ao, x1, wol, woa, g2, wg, wu, wd):
    n, d = x1.shape
    tm = OUT_FFN_TILE if n % OUT_FFN_TILE == 0 else min(TOKEN_TILE, n)
    row = lambda w: pl.BlockSpec((tm, w), lambda i: (i, 0))
    weights = 2 * (wg.size + wu.size + wd.size + wol.size + woa.size)
    tiles = 2 * tm * (2 * lo.shape[1] + 2 * ao.shape[1] + 8 * d) + tm * d * (2 + 4) + 4 * 4 * tm * d
    return pl.pallas_call(
        _out_ffn_kernel,
        grid=(n // tm,),
        in_specs=[row(lo.shape[1]), row(ao.shape[1]), row(d), _resident(wol.shape), _resident(woa.shape),
                  _resident(g2.shape), _resident(wg.shape), _resident(wu.shape), _resident(wd.shape)],
        out_specs=row(d),
        out_shape=jax.ShapeDtypeStruct((n, d), F32),
        scratch_shapes=[pltpu.VMEM((tm, d), BF16), pltpu.VMEM((tm, d), F32)],
        compiler_params=pltpu.CompilerParams(dimension_semantics=("arbitrary",),
                                             vmem_limit_bytes=_vmem_limit(weights + tiles + FFN_TEMP_BYTES)),
        name="out_ffn",
    )(lo, ao, x1, wol, woa, g2, wg, wu, wd)


def _qk_prep_kernel(pa_ref, cos_ref, sin_ref, qn_ref, kn_ref, ones_ref, q_ref, k_ref):
    lanes = V7X_LANES
    aw = q_ref.shape[1]
    cos = cos_ref[...]
    sin = sin_ref[...]
    for cc in range(aw // lanes):
        q_ref[:, cc * lanes:(cc + 1) * lanes] = _head_norm_rope(
            pa_ref[:, cc * lanes:(cc + 1) * lanes], qn_ref[...], cos, sin, ones_ref) * (HEAD_DIM ** -0.5)
    k_ref[...] = _head_norm_rope(pa_ref[:, aw:aw + lanes], kn_ref[...], cos, sin, ones_ref)


def _qk_prep(pa, cos, sin, qn, kn, ones, aw):
    n = pa.shape[0]
    return pl.pallas_call(
        _qk_prep_kernel,
        out_shape=[jax.ShapeDtypeStruct((n, aw), F32), jax.ShapeDtypeStruct((n, N_KV_HEADS * HEAD_DIM), F32)],
        compiler_params=pltpu.CompilerParams(vmem_limit_bytes=_vmem_limit(SMALL_KERNEL_VMEM_BYTES)),
        name="qk_prep_sample",
    )(pa, cos, sin, qn, kn, ones)


def _attn_cached_kernel(q_ref, kn_ref, vn_ref, ck_ref, cv_ref, sk_ref, on_ref,
                        out_ref, nk_ref, nv_ref, kk_ref, vv_ref):
    sb, t, aw = q_ref.shape
    w = ck_ref.shape[3]
    lanes = V7X_LANES
    jpad = kk_ref.shape[1]

    def rows(c_ref):
        return jnp.swapaxes(c_ref[...].reshape(sb, lanes, w), 1, 2)

    def store_window(n_ref, x):
        n_ref[...] = jnp.swapaxes(x, 1, 2).reshape(n_ref.shape)

    kk_ref[:, 0:w, :] = rows(ck_ref)
    vv_ref[:, 0:w, :] = rows(cv_ref)
    kk_ref[:, w:w + t, :] = kn_ref[...]
    vv_ref[:, w:w + t, :] = vn_ref[...]
    kk_ref[:, w + t:jpad, :] = jnp.zeros((sb, jpad - w - t, lanes), F32)
    vv_ref[:, w + t:jpad, :] = jnp.zeros((sb, jpad - w - t, lanes), F32)
    store_window(nk_ref, kk_ref[:, t:t + w, :])
    store_window(nv_ref, vv_ref[:, t:t + w, :])

    kk = kk_ref[...]
    vv = vv_ref[...]
    k_sw = pltpu.roll(kk, HEAD_DIM, 2)
    v_sw = pltpu.roll(vv, HEAD_DIM, 2)
    lane_k = lax.broadcasted_iota(jnp.int32, kk.shape, 2) < HEAD_DIM
    lane_q = lax.broadcasted_iota(jnp.int32, (sb, t, lanes), 2) < HEAD_DIM
    k2 = (jnp.where(lane_k, kk, k_sw).astype(BF16), jnp.where(lane_k, k_sw, kk).astype(BF16))
    v2 = (vv.astype(BF16), v_sw.astype(BF16))

    qcols = [q_ref[:, :, cc * lanes:(cc + 1) * lanes] for cc in range(aw // lanes)]
    tq = lax.broadcasted_iota(jnp.int32, (sb, 4 * t, jpad), 1) % t
    j = lax.broadcasted_iota(jnp.int32, (sb, 4 * t, jpad), 2)
    valid = jnp.logical_and(j > tq, j <= tq + w)
    neg_inf = jnp.float32(-jnp.inf)

    ocols = []
    for g in range(N_KV_HEADS):
        lhs = _group_lhs(qcols, g, lane_q, 1)
        s = jnp.einsum('bqd,bjd->bqj', lhs, k2[g], preferred_element_type=F32)
        s = jnp.where(valid, s, neg_inf)
        heads = (4 * g, 4 * g + 2, 4 * g + 1, 4 * g + 3)
        sk = jnp.concatenate([jnp.broadcast_to(sk_ref[h:h + 1, 0:1].reshape(1, 1, 1), (sb, t, 1)) for h in heads],
                             axis=1)
        mx = jnp.maximum(jnp.max(s, axis=-1, keepdims=True), sk)
        p = jnp.exp(s - mx)
        den = jnp.sum(p, axis=-1, keepdims=True) + jnp.exp(sk - mx)
        pb = p.astype(BF16)
        o_even = jnp.einsum('bqj,bjd->bqd', pb[:, 0:2 * t], v2[g], preferred_element_type=F32) / den[:, 0:2 * t]
        o_odd = jnp.einsum('bqj,bjd->bqd', pb[:, 2 * t:4 * t], v2[1 - g], preferred_element_type=F32) / den[:, 2 * t:4 * t]
        for half in range(2):
            ocols.append(jnp.where(lane_q, o_even[:, half * t:(half + 1) * t], o_odd[:, half * t:(half + 1) * t]))
    o = jnp.concatenate(ocols, axis=2)
    out_ref[...] = _rms(o, on_ref[...].reshape(1, 1, aw)).astype(out_ref.dtype)


def _attn_cached(q3, kn3, vn3, ck, cv, sk, on):
    s, t, aw = q3.shape
    nkv, hd, w = ck.shape[1:]
    kvw = nkv * hd
    sb = min(SAMPLE_SEQ_TILE, s)
    jpad = -(-(w + t) // V7X_SUBLANES) * V7X_SUBLANES
    seq = lambda a, c: pl.BlockSpec((sb, a, c), lambda i: (i, 0, 0))
    win = pl.BlockSpec((sb, nkv, hd, w), lambda i: (i, 0, 0, 0))
    return pl.pallas_call(
        _attn_cached_kernel,
        grid=(s // sb,),
        in_specs=[seq(t, aw), seq(t, kvw), seq(t, kvw), win, win, _resident(sk.shape), _resident(on.shape)],
        out_specs=[seq(t, aw), win, win],
        out_shape=[jax.ShapeDtypeStruct((s, t, aw), BF16), jax.ShapeDtypeStruct((s, nkv, hd, w), F32),
                   jax.ShapeDtypeStruct((s, nkv, hd, w), F32)],
        scratch_shapes=[pltpu.VMEM((sb, jpad, kvw), F32), pltpu.VMEM((sb, jpad, kvw), F32)],
        compiler_params=pltpu.CompilerParams(dimension_semantics=("arbitrary",),
                                             vmem_limit_bytes=_vmem_limit(SMALL_KERNEL_VMEM_BYTES)),
        name="attn_sample",
    )(q3, kn3, vn3, ck, cv, sk, on)


ROPE_SPLIT = 128


def _rope_tables(start, n):
    half = HEAD_DIM // 2
    lane = jnp.arange(V7X_LANES, dtype=jnp.int32)
    inv_freq = ROPE_THETA ** (-(lane % half).astype(F32) / half)
    sign = jnp.where((lane % HEAD_DIM) < half, -1.0, 1.0).astype(F32)
    if n % ROPE_SPLIT:
        pos = start + jnp.arange(n, dtype=jnp.int32)
        ang = pos.astype(F32)[:, None] * inv_freq[None, :]
        return jnp.cos(ang), jnp.sin(ang) * sign[None, :]
    lo = jnp.arange(ROPE_SPLIT, dtype=jnp.int32)
    hi = start + ROPE_SPLIT * jnp.arange(n // ROPE_SPLIT, dtype=jnp.int32)
    a_hi = hi.astype(F32)[:, None] * inv_freq[None, :]
    a_lo = lo.astype(F32)[:, None] * inv_freq[None, :]
    ch, sh = jnp.cos(a_hi)[:, None, :], jnp.sin(a_hi)[:, None, :]
    cl, sl = jnp.cos(a_lo)[None, :, :], jnp.sin(a_lo)[None, :, :]
    cos = (ch * cl - sh * sl).reshape(n, V7X_LANES)
    sin = (sh * cl + ch * sl).reshape(n, V7X_LANES)
    return cos, sin * sign[None, :]


def _block_diag(w):
    n, c, d = w.shape
    eye = jnp.eye(n, dtype=w.dtype)
    return (w[:, :, None, :] * eye[:, None, :, None]).reshape(n * c, n * d)


def _row(v):
    return v.reshape(1, -1).astype(F32)


def kernel(x_prompt, x_sample, state_lru_h, state_conv, cache_k, cache_v, ffn1_norm, ffn1_w_gate, ffn1_w_up, ffn1_w_down, mix_norm, w_in, conv_w, conv_b, lru_wa, lru_ba, lru_wx, lru_bx, lru_lambda, q_norm, k_norm, attn_sinks, lru_out_norm, attn_out_norm, w_out, ffn2_norm, ffn2_w_gate, ffn2_w_up, ffn2_w_down):
    depth = w_in.shape[0]
    b, t, d = x_prompt.shape
    s, ts, _ = x_sample.shape
    c = lru_lambda.shape[1]
    aw = attn_out_norm.shape[1]
    kvw = N_KV_HEADS * HEAD_DIM
    win = cache_k.shape[2]
    assert t >= WINDOW and win == WINDOW
    assert GATE_DIAG_TILE % lru_wa.shape[2] == 0 and c % min(GATE_DIAG_TILE, c) == 0

    cos_p, sin_p = _rope_tables(0, t)
    cos_s, sin_s = _rope_tables(PAST_LEN, ts)
    cos_s, sin_s = jnp.tile(cos_s, (s, 1)), jnp.tile(sin_s, (s, 1))
    ones = _block_diag(jnp.full((V7X_LANES // HEAD_DIM, HEAD_DIM, HEAD_DIM), 1.0 / HEAD_DIM, BF16))
    log2e = 1.0 / jnp.log(jnp.float32(2.0))

    yp = x_prompt.reshape(b * t, d)
    ys = x_sample.reshape(s * ts, d)
    outs = [[] for _ in range(8)]
    for l in range(depth):
        ffn1 = (_row(ffn1_norm[l]), (0.5 * ffn1_w_gate[l]).astype(BF16), ffn1_w_up[l].astype(BF16),
                ffn1_w_down[l].astype(BF16))
        ffn2 = (_row(ffn2_norm[l]), (0.5 * ffn2_w_gate[l]).astype(BF16), ffn2_w_up[l].astype(BF16),
                ffn2_w_down[l].astype(BF16))
        win_b = w_in[l].astype(BF16)
        ffn_in_p = ffn1 + (_row(mix_norm[l]), win_b[:, :2 * c], win_b[:, 2 * c:])
        wout_b = w_out[l].astype(BF16)
        wol, woa = wout_b[:c], wout_b[c:]
        lru_p = (conv_w[l], _row(conv_b[l]), (0.5 * _block_diag(lru_wa[l])).astype(BF16), _row(0.5 * lru_ba[l]),
                 (0.5 * _block_diag(lru_wx[l])).astype(BF16), _row(0.5 * lru_bx[l]), _row(lru_lambda[l]),
                 _row(lru_out_norm[l]))
        reps = V7X_LANES // HEAD_DIM
        qn = _row(jnp.tile(q_norm[l], reps))
        kn = _row(jnp.tile(k_norm[l], reps))
        sk = jnp.broadcast_to(attn_sinks[l].astype(F32)[:, None], (attn_sinks.shape[1], V7X_LANES))
        qn_p = qn * (HEAD_DIM ** -0.5 * log2e)
        sk_rows = jnp.repeat(attn_sinks[l].astype(F32).reshape(N_KV_HEADS, -1), WINDOW, axis=1) * log2e
        a_on = _row(attn_out_norm[l])

        x1, lo, ao, hl, cs, pk, pv = _mixer_in(yp, t, cos_p, sin_p, ffn_in_p, lru_p, (qn_p, kn, sk_rows, ones, a_on))
        yp = _out_ffn(lo, ao, x1, wol, woa, *ffn2)
        outs[0].append(hl.reshape(b, c))
        outs[1].append(cs)
        outs[2].append(pk.reshape(b, WINDOW, N_KV_HEADS, HEAD_DIM))
        outs[3].append(pv.reshape(b, WINDOW, N_KV_HEADS, HEAD_DIM))

        x1s, pl_s, pa_s = _ffn_in(ys, *ffn_in_p)
        x_tm = pl_s.reshape(s, ts, 2 * c).transpose(1, 0, 2)
        cst_tm = state_conv[l].transpose(1, 0, 2)
        lo_tm, hls, cs_tm = _lru_sample(x_tm, cst_tm, state_lru_h[l], *lru_p)
        q_s, k_s = _qk_prep(pa_s, cos_s, sin_s, qn, kn, ones, aw)
        v_s = pa_s[:, aw + kvw:]
        aos, nk, nv = _attn_cached(q_s.reshape(s, ts, aw), k_s.reshape(s, ts, kvw), v_s.reshape(s, ts, kvw),
                                   jnp.transpose(cache_k[l], (0, 2, 3, 1)), jnp.transpose(cache_v[l], (0, 2, 3, 1)),
                                   sk, a_on)
        ys = _out_ffn(lo_tm.transpose(1, 0, 2).reshape(s * ts, c), aos.reshape(s * ts, aw), x1s, wol, woa, *ffn2)
        outs[4].append(hls)
        outs[5].append(cs_tm.transpose(1, 0, 2))
        outs[6].append(jnp.transpose(nk, (0, 3, 1, 2)))
        outs[7].append(jnp.transpose(nv, (0, 3, 1, 2)))

    st = [jnp.stack(o) for o in outs]
    return (yp.reshape(b, t, d), ys.reshape(s, ts, d), st[0], st[1], st[2], st[3], st[4], st[5], st[6], st[7])
```

```python
import functools

import jax
import jax.numpy as jnp
from jax import lax
from jax.experimental import pallas as pl
from jax.experimental.pallas import tpu as pltpu

F32 = jnp.float32
BF16 = jnp.bfloat16

NORM_EPS = 1e-6
LRU_C = 8.0
CONV_W = 4
HEAD_DIM = 64
N_KV_HEADS = 2
WINDOW = 128
ROPE_THETA = 10000.0
PAST_LEN = 16384

V7X_LANES = 128
V7X_SUBLANES = 8
V7X_MXU_DIM = 256
V7X_VMEM_BYTES = 64 * 1024 * 1024

TOKEN_TILE = 512
OUT_FFN_TILE = 1024
SAMPLE_SEQ_TILE = 32
FF_CHUNK = V7X_MXU_DIM
GATE_DIAG_TILE = V7X_MXU_DIM


MIB = 1024 * 1024
VMEM_KEEP_FREE_BYTES = 4 * MIB
FFN_TEMP_BYTES = 8 * MIB
MIXER_TEMP_BYTES = 20 * MIB
SMALL_KERNEL_VMEM_BYTES = 40 * MIB


def _vmem_limit(nbytes):
    return int(min(nbytes, V7X_VMEM_BYTES - VMEM_KEEP_FREE_BYTES))


def _resident(shape):
    nd = len(shape)
    return pl.BlockSpec(shape, lambda *_: (0,) * nd, pipeline_mode=pl.Buffered(1))


def _rms(x, g):
    ms = jnp.mean(x * x, axis=-1, keepdims=True)
    return (x * lax.rsqrt(ms + NORM_EPS)) * g


def _ffn(xn_ref, wg_ref, wu_ref, wd_ref, acc_ref):
    acc_ref[...] = jnp.zeros_like(acc_ref)
    for c0 in range(0, wg_ref.shape[1], FF_CHUNK):
        xn = xn_ref[...]
        h = jnp.dot(xn, wg_ref[:, c0:c0 + FF_CHUNK], preferred_element_type=F32)
        u = jnp.dot(xn, wu_ref[:, c0:c0 + FF_CHUNK], preferred_element_type=F32)
        a = ((h + h * jnp.tanh(h)) * u).astype(BF16)
        acc_ref[...] += jnp.dot(a, wd_ref[c0:c0 + FF_CHUNK, :], preferred_element_type=F32)
    return acc_ref[...]


def _ffn_in_tile(x_ref, ffn_refs, gm_ref, winl_ref, wina_ref, x1_ref, pl_ref, pa_ref, xn_ref, acc_ref):
    g1_ref, wg_ref, wu_ref, wd_ref = ffn_refs
    xn_ref[...] = _rms(x_ref[...], g1_ref[...]).astype(BF16)
    x1 = x_ref[...] + 0.5 * _ffn(xn_ref, wg_ref, wu_ref, wd_ref, acc_ref)
    x1_ref[...] = x1
    xn_ref[...] = _rms(x1, gm_ref[...]).astype(BF16)
    pl_ref[...] = jnp.dot(xn_ref[...], winl_ref[...], preferred_element_type=F32)
    pa_ref[...] = jnp.dot(xn_ref[...], wina_ref[...], preferred_element_type=F32)


def _lru_preact(xc, gate_w):
    wa_ref, ba_ref, wx_ref, bx_ref = gate_w
    xcb = xc.astype(BF16)
    c = xc.shape[-1]
    dt = min(GATE_DIAG_TILE, c)

    def blockdiag_dot(w_ref):
        return jnp.concatenate([jnp.dot(xcb[:, s:s + dt], w_ref[s:s + dt, s:s + dt], preferred_element_type=F32)
                                for s in range(0, c, dt)], axis=1)

    zr = blockdiag_dot(wa_ref) + ba_ref[...]
    zi = blockdiag_dot(wx_ref) + bx_ref[...]
    return zr, zi


def _lru_terms(xc, zr, zi, lam_ref):
    i = 0.5 * jnp.tanh(zi) + 0.5
    half_c = (-0.5 * LRU_C) * jax.nn.softplus(-lam_ref[...])
    log_a = half_c * jnp.tanh(zr) + half_c
    a = jnp.exp(log_a)
    w = -jnp.tanh(log_a) * (a * a + 1.0)
    root = jnp.where(w > 0.0, w * lax.rsqrt(w), 0.0)
    u = root * (i * xc)
    return a, u


def _lru_prep(p_ref, first, lru_refs, xh_ref, xc_ref, zr_ref, zi_ref, gate_ref, ctail_ref):
    cw_ref, cb_ref, wa_ref, ba_ref, wx_ref, bx_ref, _, _ = lru_refs
    tt = p_ref.shape[0]
    c = xc_ref.shape[1]
    hist = V7X_SUBLANES

    xh_ref[0:hist, :] = jnp.where(first, jnp.zeros((hist, c), F32), xh_ref[0:hist, :])
    x = p_ref[:, 0:c]
    xh_ref[hist:hist + tt, :] = x
    xg = xh_ref[...].reshape(tt // hist + 1, hist, c)
    sub = lax.broadcasted_iota(jnp.int32, (tt // hist, hist, c), 1)
    w = cw_ref[...]
    xc = cb_ref[...].reshape(1, 1, c)
    for j in range(CONV_W - 1):
        k = CONV_W - 1 - j
        rolled = pltpu.roll(xg, k, 1)
        xc = xc + jnp.where(sub >= k, rolled[1:], rolled[:-1]) * w[j:j + 1, :].reshape(1, 1, c)
    xc = (xc + xg[1:] * w[CONV_W - 1:CONV_W, :].reshape(1, 1, c)).reshape(tt, c)
    xh_ref[0:hist, :] = x[tt - hist:tt, :]
    ctail_ref[...] = x[tt - (CONV_W - 1):tt, :]

    xc_ref[...] = xc
    zr, zi = _lru_preact(xc, (wa_ref, ba_ref, wx_ref, bx_ref))
    zr_ref[...] = zr
    zi_ref[...] = zi
    gate_ref[...] = p_ref[:, c:2 * c]


def _lru_core(first, lru_refs, xc_ref, zr_ref, zi_ref, gate_ref, ctail_ref, out_ref, hl_ref, cs_ref, h_ref):
    lam_ref, on_ref = lru_refs[6], lru_refs[7]
    tt, c = xc_ref.shape

    a, u = _lru_terms(xc_ref[...], zr_ref[...], zi_ref[...], lam_ref)

    groups = tt // V7X_SUBLANES
    a3 = a.reshape(groups, V7X_SUBLANES, c)
    u3 = u.reshape(groups, V7X_SUBLANES, c)
    sub = lax.broadcasted_iota(jnp.int32, a3.shape, 1)
    k = 1
    while k < V7X_SUBLANES:
        m = sub >= k
        a_prev = pltpu.roll(a3, k, 1)
        u_prev = pltpu.roll(u3, k, 1)
        u3 = jnp.where(m, a3 * u_prev + u3, u3)
        a3 = jnp.where(m, a3 * a_prev, a3)
        k *= 2

    h = jnp.where(first, jnp.zeros((1, c), F32), h_ref[...])
    hs = []
    for g in range(groups):
        hr = a3[g] * h + u3[g]
        hs.append(hr)
        h = hr[V7X_SUBLANES - 1:V7X_SUBLANES, :]
    h_ref[...] = h

    y = jnp.concatenate(hs, axis=0) * jax.nn.gelu(gate_ref[...])
    out_ref[...] = _rms(y, on_ref[...]).astype(out_ref.dtype)
    hl_ref[0] = h
    cs_ref[0] = ctail_ref[...]


def _lru_sample_kernel(x_ref, cst_ref, h0_ref, cw_ref, cb_ref, wa_ref, ba_ref, wx_ref, bx_ref, lam_ref, on_ref,
                       out_ref, hl_ref, cs_ref):
    t = x_ref.shape[0]
    c = out_ref.shape[2]
    w = cw_ref[...]
    rows = [cst_ref[j] for j in range(CONV_W - 1)] + [x_ref[j, :, 0:c] for j in range(t)]
    h = h0_ref[...]
    for s in range(t):
        xc = cb_ref[...]
        for j in range(CONV_W):
            xc = xc + rows[s + j] * w[j:j + 1, :]
        zr, zi = _lru_preact(xc, (wa_ref, ba_ref, wx_ref, bx_ref))
        a, u = _lru_terms(xc, zr, zi, lam_ref)
        h = a * h + u
        y = h * jax.nn.gelu(x_ref[s, :, c:2 * c])
        out_ref[s] = _rms(y, on_ref[...]).astype(out_ref.dtype)
    hl_ref[...] = h
    for j in range(CONV_W - 1):
        cs_ref[j] = rows[t + j]


def _lru_sample(x_tm, cst_tm, h0, cw, cb, wa, ba, wx, bx, lam, on):
    t, s, w2 = x_tm.shape
    c = w2 // 2
    return pl.pallas_call(
        _lru_sample_kernel,
        out_shape=[jax.ShapeDtypeStruct((t, s, c), BF16), jax.ShapeDtypeStruct((s, c), F32),
                   jax.ShapeDtypeStruct((CONV_W - 1, s, c), F32)],
        compiler_params=pltpu.CompilerParams(vmem_limit_bytes=_vmem_limit(SMALL_KERNEL_VMEM_BYTES)),
        name="lru_sample",
    )(x_tm, cst_tm, h0, cw, cb, wa, ba, wx, bx, lam, on)


def _head_norm_rope(xc, gain, cos, sin, ones_ref):
    x2 = xc * xc
    hi = x2.astype(BF16)
    lo = (x2 - hi.astype(F32)).astype(BF16)
    ss = jnp.dot(hi, ones_ref[...], preferred_element_type=F32) + jnp.dot(lo, ones_ref[...], preferred_element_type=F32)
    xn = (xc * lax.rsqrt(ss + NORM_EPS)) * gain
    lane = lax.broadcasted_iota(jnp.int32, xn.shape, xn.ndim - 1)
    first_half = (lane & (HEAD_DIM // 2)) == 0
    ax = xn.ndim - 1
    swapped = jnp.where(first_half, pltpu.roll(xn, V7X_LANES - HEAD_DIM // 2, ax), pltpu.roll(xn, HEAD_DIM // 2, ax))
    return xn * cos + swapped * sin


def _group_lhs(qcols, g, lane_lo, axis):
    zero = jnp.zeros_like(qcols[0])
    parts = [jnp.where(lane_lo, qcols[2 * g], zero), jnp.where(lane_lo, qcols[2 * g + 1], zero),
             jnp.where(lane_lo, zero, qcols[2 * g]), jnp.where(lane_lo, zero, qcols[2 * g + 1])]
    return jnp.concatenate(parts, axis=axis).astype(BF16)


def _attn_prep(pa_ref, cos_ref, sin_ref, attn_refs, qh_ref, k2_ref, vt_ref, kvl_ref):
    qn_ref, kn_ref, _, ones_ref, _ = attn_refs
    tq = pa_ref.shape[0]
    blk = WINDOW
    lanes = V7X_LANES
    ncol = qh_ref.shape[0] // 2
    aw = ncol * lanes

    cos = cos_ref[...]
    sin = sin_ref[...]
    cols = [pa_ref[:, cc * lanes:(cc + 1) * lanes] for cc in range(ncol + 1)]
    sq = [x * x for x in cols]
    hi = [x.astype(BF16) for x in sq]
    lo = [(x - h.astype(F32)).astype(BF16) for x, h in zip(sq, hi)]
    ss = (jnp.dot(jnp.concatenate(hi, axis=0), ones_ref[...], preferred_element_type=F32)
          + jnp.dot(jnp.concatenate(lo, axis=0), ones_ref[...], preferred_element_type=F32))
    lane = lax.broadcasted_iota(jnp.int32, (tq, lanes), 1)
    first_half = (lane & (HEAD_DIM // 2)) == 0
    lane_lo = lane < HEAD_DIM
    rot = []
    for cc, x in enumerate(cols):
        gain = qn_ref[...] if cc < ncol else kn_ref[...]
        xn = (x * lax.rsqrt(ss[cc * tq:(cc + 1) * tq] + NORM_EPS)) * gain
        swapped = jnp.where(first_half, pltpu.roll(xn, lanes - HEAD_DIM // 2, 1), pltpu.roll(xn, HEAD_DIM // 2, 1))
        rot.append(xn * cos + swapped * sin)
    k = rot[ncol]
    v = pa_ref[:, aw + lanes:aw + 2 * lanes]

    zero = jnp.zeros((tq, lanes), F32)
    for cc in range(ncol):
        qh_ref[2 * cc] = jnp.where(lane_lo, rot[cc], zero).astype(BF16)
        qh_ref[2 * cc + 1] = jnp.where(lane_lo, zero, rot[cc]).astype(BF16)

    k_sw = pltpu.roll(k, HEAD_DIM, 1)
    k2_ref[0, blk:blk + tq, :] = jnp.where(lane_lo, k, k_sw).astype(BF16)
    k2_ref[1, blk:blk + tq, :] = jnp.where(lane_lo, k_sw, k).astype(BF16)
    for jb in range(tq // blk):
        vt_ref[:, (jb + 1) * blk:(jb + 2) * blk] = v[jb * blk:(jb + 1) * blk].T.astype(BF16)
    kvl_ref[0] = k[tq - blk:tq]
    kvl_ref[1] = v[tq - blk:tq]


def _attn_core(first, attn_refs, out_ref, pk_ref, pv_ref, qh_ref, k2_ref, vt_ref, kvl_ref, st_ref, pt_ref, rd_ref):
    _, _, sk_ref, _, on_ref = attn_refs
    tq = qh_ref.shape[1]
    blk = WINDOW
    nblk = tq // blk
    heads_per_group = qh_ref.shape[0] // N_KV_HEADS
    pairs = [(jb, g) for jb in range(nblk) for g in range(N_KV_HEADS)]

    for idx, (jb, g) in enumerate(pairs):
        r0 = jb * blk
        lhs = jnp.concatenate([qh_ref[h, r0:r0 + blk, :]
                               for h in range(g * heads_per_group, (g + 1) * heads_per_group)], axis=0)
        st_ref[idx] = lax.dot_general(k2_ref[g, r0:r0 + 2 * blk, :], lhs, (((1,), (1,)), ((), ())),
                                      preferred_element_type=F32)

    ncolq = heads_per_group * blk
    kj = lax.broadcasted_iota(jnp.int32, (blk, ncolq), 0)
    qi = lax.broadcasted_iota(jnp.int32, (blk, ncolq), 1) % blk
    tri = kj <= qi
    for idx, (jb, g) in enumerate(pairs):
        s_prev = st_ref[idx, 0:blk, :]
        if jb == 0:
            s_prev = s_prev + jnp.where(first, jnp.float32(-jnp.inf), jnp.float32(0.0))
        m = jnp.where(tri, st_ref[idx, blk:2 * blk, :], s_prev)
        sk = sk_ref[g:g + 1, :]
        mx = jnp.maximum(jnp.max(m, axis=0, keepdims=True), sk)
        p = jnp.exp2(m - mx)
        den = jnp.sum(p, axis=0, keepdims=True) + jnp.exp2(sk - mx)
        pz = jnp.zeros_like(p)
        pt_ref[idx] = jnp.concatenate([jnp.where(tri, pz, p), jnp.where(tri, p, pz)], axis=0).astype(BF16)
        rd_ref[idx] = 1.0 / den

    for jb in range(nblk):
        r0 = jb * blk
        ocols = []
        for g in range(N_KV_HEADS):
            idx = jb * N_KV_HEADS + g
            vt_g = vt_ref[g * HEAD_DIM:(g + 1) * HEAD_DIM, r0:r0 + 2 * blk]
            og = jnp.dot(vt_g, pt_ref[idx], preferred_element_type=F32) * rd_ref[idx]
            for half in range(heads_per_group // 2):
                pair = jnp.concatenate([og[:, (2 * half) * blk:(2 * half + 1) * blk],
                                        og[:, (2 * half + 1) * blk:(2 * half + 2) * blk]], axis=0)
                ocols.append(pair.T)
        o = jnp.concatenate(ocols, axis=1)
        out_ref[r0:r0 + blk, :] = _rms(o, on_ref[...]).astype(out_ref.dtype)

    k2_ref[:, 0:blk, :] = k2_ref[:, tq:tq + blk, :]
    vt_ref[:, 0:blk] = vt_ref[:, tq:tq + blk]
    pk_ref[0] = kvl_ref[0]
    pv_ref[0] = kvl_ref[1]


N_FFN_IN = 7
N_LRU = 8
N_ATTN = 5


def _mixer_in_kernel(*refs, tiles_per_seq, ntile):
    x_ref, cos_ref, sin_ref = refs[0:3]
    p0 = 3
    g1_ref, wg_ref, wu_ref, wd_ref, gm_ref, winl_ref, wina_ref = refs[p0:p0 + N_FFN_IN]
    lru_refs = refs[p0 + N_FFN_IN:p0 + N_FFN_IN + N_LRU]
    attn_refs = refs[p0 + N_FFN_IN + N_LRU:p0 + N_FFN_IN + N_LRU + N_ATTN]
    o0 = p0 + N_FFN_IN + N_LRU + N_ATTN
    x1_ref, lo_ref, ao_ref, hl_ref, cs_ref, pk_ref, pv_ref = refs[o0:o0 + 7]
    (xn_ref, acc_ref, pl_ref, pa_ref, xh_ref, h_ref, xc_ref, zr_ref, zi_ref, gate_ref, ctail_ref,
     qh_ref, k2_ref, vt_ref, kvl_ref, st_ref, pt_ref, rd_ref) = refs[o0 + 7:]

    i = pl.program_id(0)

    @pl.when(i == 0)
    def _():
        for ref in (pl_ref, pa_ref, xh_ref, h_ref, xc_ref, zr_ref, zi_ref, gate_ref, ctail_ref,
                    qh_ref, k2_ref, vt_ref, kvl_ref):
            ref[...] = jnp.zeros_like(ref)

    def step(with_ffn):
        first = lax.rem(jnp.maximum(i - 1, 0), tiles_per_seq) == 0
        _attn_prep(pa_ref, cos_ref, sin_ref, attn_refs, qh_ref, k2_ref, vt_ref, kvl_ref)
        _lru_prep(pl_ref, first, lru_refs, xh_ref, xc_ref, zr_ref, zi_ref, gate_ref, ctail_ref)
        _attn_core(first, attn_refs, ao_ref, pk_ref, pv_ref, qh_ref, k2_ref, vt_ref, kvl_ref, st_ref, pt_ref, rd_ref)
        _lru_core(first, lru_refs, xc_ref, zr_ref, zi_ref, gate_ref, ctail_ref, lo_ref, hl_ref, cs_ref, h_ref)
        if with_ffn:
            _ffn_in_tile(x_ref, (g1_ref, wg_ref, wu_ref, wd_ref), gm_ref, winl_ref, wina_ref,
                         x1_ref, pl_ref, pa_ref, xn_ref, acc_ref)

    pl.when(i < ntile)(functools.partial(step, True))
    pl.when(i >= ntile)(functools.partial(step, False))


def _mixer_in(x, seq_len, cos, sin, ffn_in_p, lru_p, attn_p):
    n, d = x.shape
    tm = TOKEN_TILE
    assert seq_len % tm == 0 and n % seq_len == 0 and tm % WINDOW == 0
    nseq = n // seq_len
    tps = seq_len // tm
    ntile = n // tm
    wg, winl, wina = ffn_in_p[1], ffn_in_p[5], ffn_in_p[6]
    w2, wa = winl.shape[1], wina.shape[1]
    c = w2 // 2
    aw = attn_p[4].shape[1]
    kvw = N_KV_HEADS * HEAD_DIM
    npair = tm // WINDOW * N_KV_HEADS
    heads_per_group = aw // HEAD_DIM // N_KV_HEADS
    cur = lambda i: jnp.minimum(i, ntile - 1)
    mix = lambda i: jnp.maximum(i - 1, 0)
    params = list(ffn_in_p) + list(lru_p) + list(attn_p)
    weights = sum(p.size * p.dtype.itemsize for p in params)
    tiles = (2 * 4 * tm * 2 * d + 2 * 2 * tm * (c + aw) + 4 * tm * (w2 + wa) + tm * d * (2 + 4) + 6 * 4 * tm * c
             + npair * 2 * WINDOW * heads_per_group * WINDOW * (4 + 2))
    return pl.pallas_call(
        functools.partial(_mixer_in_kernel, tiles_per_seq=tps, ntile=ntile),
        grid=(ntile + 1,),
        in_specs=[pl.BlockSpec((tm, d), lambda i: (cur(i), 0)),
                  pl.BlockSpec((tm, V7X_LANES), lambda i: (lax.rem(mix(i), tps), 0)),
                  pl.BlockSpec((tm, V7X_LANES), lambda i: (lax.rem(mix(i), tps), 0))]
                 + [_resident(p.shape) for p in params],
        out_specs=[pl.BlockSpec((tm, d), lambda i: (cur(i), 0)),
                   pl.BlockSpec((tm, c), lambda i: (mix(i), 0)),
                   pl.BlockSpec((tm, aw), lambda i: (mix(i), 0)),
                   pl.BlockSpec((1, 1, c), lambda i: (mix(i) // tps, 0, 0)),
                   pl.BlockSpec((1, CONV_W - 1, c), lambda i: (mix(i) // tps, 0, 0)),
                   pl.BlockSpec((1, WINDOW, kvw), lambda i: (mix(i) // tps, 0, 0)),
                   pl.BlockSpec((1, WINDOW, kvw), lambda i: (mix(i) // tps, 0, 0))],
        out_shape=[jax.ShapeDtypeStruct((n, d), F32), jax.ShapeDtypeStruct((n, c), BF16),
                   jax.ShapeDtypeStruct((n, aw), BF16), jax.ShapeDtypeStruct((nseq, 1, c), F32),
                   jax.ShapeDtypeStruct((nseq, CONV_W - 1, c), F32), jax.ShapeDtypeStruct((nseq, WINDOW, kvw), F32),
                   jax.ShapeDtypeStruct((nseq, WINDOW, kvw), F32)],
        scratch_shapes=[pltpu.VMEM((tm, d), BF16), pltpu.VMEM((tm, d), F32),
                        pltpu.VMEM((tm, w2), F32), pltpu.VMEM((tm, wa), F32),
                        pltpu.VMEM((tm + V7X_SUBLANES, c), F32), pltpu.VMEM((1, c), F32),
                        pltpu.VMEM((tm, c), F32), pltpu.VMEM((tm, c), F32), pltpu.VMEM((tm, c), F32),
                        pltpu.VMEM((tm, c), F32), pltpu.VMEM((CONV_W - 1, c), F32),
                        pltpu.VMEM((2 * aw // V7X_LANES, tm, V7X_LANES), BF16),
                        pltpu.VMEM((N_KV_HEADS, WINDOW + tm, V7X_LANES), BF16),
                        pltpu.VMEM((V7X_LANES, WINDOW + tm), BF16),
                        pltpu.VMEM((2, WINDOW, kvw), F32),
                        pltpu.VMEM((npair, 2 * WINDOW, heads_per_group * WINDOW), F32),
                        pltpu.VMEM((npair, 2 * WINDOW, heads_per_group * WINDOW), BF16),
                        pltpu.VMEM((npair, 1, heads_per_group * WINDOW), F32)],
        compiler_params=pltpu.CompilerParams(dimension_semantics=("arbitrary",),
                                             vmem_limit_bytes=_vmem_limit(weights + tiles + MIXER_TEMP_BYTES)),
        name="mixer_in",
    )(x, cos, sin, *params)


def _ffn_in_kernel(x_ref, g1_ref, wg_ref, wu_ref, wd_ref, gm_ref, winl_ref, wina_ref,
                   x1_ref, pl_ref, pa_ref, xn_ref, acc_ref):
    _ffn_in_tile(x_ref, (g1_ref, wg_ref, wu_ref, wd_ref), gm_ref, winl_ref, wina_ref,
                 x1_ref, pl_ref, pa_ref, xn_ref, acc_ref)


def _ffn_in(x, g1, wg, wu, wd, gm, winl, wina):
    n, d = x.shape
    tm = min(TOKEN_TILE, n)
    wl, wa = winl.shape[1], wina.shape[1]
    row = lambda w: pl.BlockSpec((tm, w), lambda i: (i, 0))
    weights = 2 * (wg.size + wu.size + wd.size + winl.size + wina.size)
    tiles = 2 * 4 * tm * (2 * d + wl + wa) + tm * d * (2 + 4) + 4 * 4 * tm * max(FF_CHUNK, wl)
    return pl.pallas_call(
        _ffn_in_kernel,
        grid=(n // tm,),
        in_specs=[row(d), _resident(g1.shape), _resident(wg.shape), _resident(wu.shape), _resident(wd.shape),
                  _resident(gm.shape), _resident(winl.shape), _resident(wina.shape)],
        out_specs=[row(d), row(wl), row(wa)],
        out_shape=[jax.ShapeDtypeStruct((n, d), F32), jax.ShapeDtypeStruct((n, wl), F32),
                   jax.ShapeDtypeStruct((n, wa), F32)],
        scratch_shapes=[pltpu.VMEM((tm, d), BF16), pltpu.VMEM((tm, d), F32)],
        compiler_params=pltpu.CompilerParams(dimension_semantics=("arbitrary",),
                                             vmem_limit_bytes=_vmem_limit(weights + tiles + FFN_TEMP_BYTES)),
        name="ffn_in",
    )(x, g1, wg, wu, wd, gm, winl, wina)


def _out_ffn_kernel(lo_ref, ao_ref, x1_ref, wol_ref, woa_ref, g2_ref, wg_ref, wu_ref, wd_ref,
                    out_ref, xn_ref, acc_ref):
    y = jnp.dot(lo_ref[...], wol_ref[...], preferred_element_type=F32)
    y = y + jnp.dot(ao_ref[...], woa_ref[...], preferred_element_type=F32)
    x2 = x1_ref[...] + y
    out_ref[...] = x2
    xn_ref[...] = _rms(x2, g2_ref[...]).astype(BF16)
    out_ref[...] = out_ref[...] + 0.5 * _ffn(xn_ref, wg_ref, wu_ref, wd_ref, acc_ref)


def _out_ffn(lo, ao, x1, wol, woa, g2, wg, wu, wd):
    n, d = x1.shape
    tm = OUT_FFN_TILE if n % OUT_FFN_TILE == 0 else min(TOKEN_TILE, n)
    row = lambda w: pl.BlockSpec((tm, w), lambda i: (i, 0))
    weights = 2 * (wg.size + wu.size + wd.size + wol.size + woa.size)
    tiles = 2 * tm * (2 * lo.shape[1] + 2 * ao.shape[1] + 8 * d) + tm * d * (2 + 4) + 4 * 4 * tm * d
    return pl.pallas_call(
        _out_ffn_kernel,
        grid=(n // tm,),
        in_specs=[row(lo.shape[1]), row(ao.shape[1]), row(d), _resident(wol.shape), _resident(woa.shape),
                  _resident(g2.shape), _resident(wg.shape), _resident(wu.shape), _resident(wd.shape)],
        out_specs=row(d),
        out_shape=jax.ShapeDtypeStruct((n, d), F32),
        scratch_shapes=[pltpu.VMEM((tm, d), BF16), pltpu.VMEM((tm, d), F32)],
        compiler_params=pltpu.CompilerParams(dimension_semantics=("arbitrary",),
                                             vmem_limit_bytes=_vmem_limit(weights + tiles + FFN_TEMP_BYTES)),
        name="out_ffn",
    )(lo, ao, x1, wol, woa, g2, wg, wu, wd)


def _qk_prep_kernel(pa_ref, cos_ref, sin_ref, qn_ref, kn_ref, ones_ref, q_ref, k_ref):
    lanes = V7X_LANES
    aw = q_ref.shape[1]
    cos = cos_ref[...]
    sin = sin_ref[...]
    for cc in range(aw // lanes):
        q_ref[:, cc * lanes:(cc + 1) * lanes] = _head_norm_rope(
            pa_ref[:, cc * lanes:(cc + 1) * lanes], qn_ref[...], cos, sin, ones_ref) * (HEAD_DIM ** -0.5)
    k_ref[...] = _head_norm_rope(pa_ref[:, aw:aw + lanes], kn_ref[...], cos, sin, ones_ref)


def _qk_prep(pa, cos, sin, qn, kn, ones, aw):
    n = pa.shape[0]
    return pl.pallas_call(
        _qk_prep_kernel,
        out_shape=[jax.ShapeDtypeStruct((n, aw), F32), jax.ShapeDtypeStruct((n, N_KV_HEADS * HEAD_DIM), F32)],
        compiler_params=pltpu.CompilerParams(vmem_limit_bytes=_vmem_limit(SMALL_KERNEL_VMEM_BYTES)),
        name="qk_prep_sample",
    )(pa, cos, sin, qn, kn, ones)


def _attn_cached_kernel(q_ref, kn_ref, vn_ref, ck_ref, cv_ref, sk_ref, on_ref,
                        out_ref, nk_ref, nv_ref, kk_ref, vv_ref):
    sb, t, aw = q_ref.shape
    w = ck_ref.shape[3]
    lanes = V7X_LANES
    jpad = kk_ref.shape[1]

    def rows(c_ref):
        return jnp.swapaxes(c_ref[...].reshape(sb, lanes, w), 1, 2)

    def store_window(n_ref, x):
        n_ref[...] = jnp.swapaxes(x, 1, 2).reshape(n_ref.shape)

    kk_ref[:, 0:w, :] = rows(ck_ref)
    vv_ref[:, 0:w, :] = rows(cv_ref)
    kk_ref[:, w:w + t, :] = kn_ref[...]
    vv_ref[:, w:w + t, :] = vn_ref[...]
    kk_ref[:, w + t:jpad, :] = jnp.zeros((sb, jpad - w - t, lanes), F32)
    vv_ref[:, w + t:jpad, :] = jnp.zeros((sb, jpad - w - t, lanes), F32)
    store_window(nk_ref, kk_ref[:, t:t + w, :])
    store_window(nv_ref, vv_ref[:, t:t + w, :])

    kk = kk_ref[...]
    vv = vv_ref[...]
    k_sw = pltpu.roll(kk, HEAD_DIM, 2)
    v_sw = pltpu.roll(vv, HEAD_DIM, 2)
    lane_k = lax.broadcasted_iota(jnp.int32, kk.shape, 2) < HEAD_DIM
    lane_q = lax.broadcasted_iota(jnp.int32, (sb, t, lanes), 2) < HEAD_DIM
    k2 = (jnp.where(lane_k, kk, k_sw).astype(BF16), jnp.where(lane_k, k_sw, kk).astype(BF16))
    v2 = (vv.astype(BF16), v_sw.astype(BF16))

    qcols = [q_ref[:, :, cc * lanes:(cc + 1) * lanes] for cc in range(aw // lanes)]
    tq = lax.broadcasted_iota(jnp.int32, (sb, 4 * t, jpad), 1) % t
    j = lax.broadcasted_iota(jnp.int32, (sb, 4 * t, jpad), 2)
    valid = jnp.logical_and(j > tq, j <= tq + w)
    neg_inf = jnp.float32(-jnp.inf)

    ocols = []
    for g in range(N_KV_HEADS):
        lhs = _group_lhs(qcols, g, lane_q, 1)
        s = jnp.einsum('bqd,bjd->bqj', lhs, k2[g], preferred_element_type=F32)
        s = jnp.where(valid, s, neg_inf)
        heads = (4 * g, 4 * g + 2, 4 * g + 1, 4 * g + 3)
        sk = jnp.concatenate([jnp.broadcast_to(sk_ref[h:h + 1, 0:1].reshape(1, 1, 1), (sb, t, 1)) for h in heads],
                             axis=1)
        mx = jnp.maximum(jnp.max(s, axis=-1, keepdims=True), sk)
        p = jnp.exp(s - mx)
        den = jnp.sum(p, axis=-1, keepdims=True) + jnp.exp(sk - mx)
        pb = p.astype(BF16)
        o_even = jnp.einsum('bqj,bjd->bqd', pb[:, 0:2 * t], v2[g], preferred_element_type=F32) / den[:, 0:2 * t]
        o_odd = jnp.einsum('bqj,bjd->bqd', pb[:, 2 * t:4 * t], v2[1 - g], preferred_element_type=F32) / den[:, 2 * t:4 * t]
        for half in range(2):
            ocols.append(jnp.where(lane_q, o_even[:, half * t:(half + 1) * t], o_odd[:, half * t:(half + 1) * t]))
    o = jnp.concatenate(ocols, axis=2)
    out_ref[...] = _rms(o, on_ref[...].reshape(1, 1, aw)).astype(out_ref.dtype)


def _attn_cached(q3, kn3, vn3, ck, cv, sk, on):
    s, t, aw = q3.shape
    nkv, hd, w = ck.shape[1:]
    kvw = nkv * hd
    sb = min(SAMPLE_SEQ_TILE, s)
    jpad = -(-(w + t) // V7X_SUBLANES) * V7X_SUBLANES
    seq = lambda a, c: pl.BlockSpec((sb, a, c), lambda i: (i, 0, 0))
    win = pl.BlockSpec((sb, nkv, hd, w), lambda i: (i, 0, 0, 0))
    return pl.pallas_call(
        _attn_cached_kernel,
        grid=(s // sb,),
        in_specs=[seq(t, aw), seq(t, kvw), seq(t, kvw), win, win, _resident(sk.shape), _resident(on.shape)],
        out_specs=[seq(t, aw), win, win],
        out_shape=[jax.ShapeDtypeStruct((s, t, aw), BF16), jax.ShapeDtypeStruct((s, nkv, hd, w), F32),
                   jax.ShapeDtypeStruct((s, nkv, hd, w), F32)],
        scratch_shapes=[pltpu.VMEM((sb, jpad, kvw), F32), pltpu.VMEM((sb, jpad, kvw), F32)],
        compiler_params=pltpu.CompilerParams(dimension_semantics=("arbitrary",),
                                             vmem_limit_bytes=_vmem_limit(SMALL_KERNEL_VMEM_BYTES)),
        name="attn_sample",
    )(q3, kn3, vn3, ck, cv, sk, on)


ROPE_SPLIT = 128


def _rope_tables(start, n):
    half = HEAD_DIM // 2
    lane = jnp.arange(V7X_LANES, dtype=jnp.int32)
    inv_freq = ROPE_THETA ** (-(lane % half).astype(F32) / half)
    sign = jnp.where((lane % HEAD_DIM) < half, -1.0, 1.0).astype(F32)
    if n % ROPE_SPLIT:
        pos = start + jnp.arange(n, dtype=jnp.int32)
        ang = pos.astype(F32)[:, None] * inv_freq[None, :]
        return jnp.cos(ang), jnp.sin(ang) * sign[None, :]
    lo = jnp.arange(ROPE_SPLIT, dtype=jnp.int32)
    hi = start + ROPE_SPLIT * jnp.arange(n // ROPE_SPLIT, dtype=jnp.int32)
    a_hi = hi.astype(F32)[:, None] * inv_freq[None, :]
    a_lo = lo.astype(F32)[:, None] * inv_freq[None, :]
    ch, sh = jnp.cos(a_hi)[:, None, :], jnp.sin(a_hi)[:, None, :]
    cl, sl = jnp.cos(a_lo)[None, :, :], jnp.sin(a_lo)[None, :, :]
    cos = (ch * cl - sh * sl).reshape(n, V7X_LANES)
    sin = (sh * cl + ch * sl).reshape(n, V7X_LANES)
    return cos, sin * sign[None, :]


def _block_diag(w):
    n, c, d = w.shape
    eye = jnp.eye(n, dtype=w.dtype)
    return (w[:, :, None, :] * eye[:, None, :, None]).reshape(n * c, n * d)


def _row(v):
    return v.reshape(1, -1).astype(F32)


def kernel(x_prompt, x_sample, state_lru_h, state_conv, cache_k, cache_v, ffn1_norm, ffn1_w_gate, ffn1_w_up, ffn1_w_down, mix_norm, w_in, conv_w, conv_b, lru_wa, lru_ba, lru_wx, lru_bx, lru_lambda, q_norm, k_norm, attn_sinks, lru_out_norm, attn_out_norm, w_out, ffn2_norm, ffn2_w_gate, ffn2_w_up, ffn2_w_down):
    depth = w_in.shape[0]
    b, t, d = x_prompt.shape
    s, ts, _ = x_sample.shape
    c = lru_lambda.shape[1]
    aw = attn_out_norm.shape[1]
    kvw = N_KV_HEADS * HEAD_DIM
    win = cache_k.shape[2]
    assert t >= WINDOW and win == WINDOW
    assert GATE_DIAG_TILE % lru_wa.shape[2] == 0 and c % min(GATE_DIAG_TILE, c) == 0

    cos_p, sin_p = _rope_tables(0, t)
    cos_s, sin_s = _rope_tables(PAST_LEN, ts)
    cos_s, sin_s = jnp.tile(cos_s, (s, 1)), jnp.tile(sin_s, (s, 1))
    ones = _block_diag(jnp.full((V7X_LANES // HEAD_DIM, HEAD_DIM, HEAD_DIM), 1.0 / HEAD_DIM, BF16))
    log2e = 1.0 / jnp.log(jnp.float32(2.0))

    yp = x_prompt.reshape(b * t, d)
    ys = x_sample.reshape(s * ts, d)
    outs = [[] for _ in range(8)]
    for l in range(depth):
        ffn1 = (_row(ffn1_norm[l]), (0.5 * ffn1_w_gate[l]).astype(BF16), ffn1_w_up[l].astype(BF16),
                ffn1_w_down[l].astype(BF16))
        ffn2 = (_row(ffn2_norm[l]), (0.5 * ffn2_w_gate[l]).astype(BF16), ffn2_w_up[l].astype(BF16),
                ffn2_w_down[l].astype(BF16))
        win_b = w_in[l].astype(BF16)
        ffn_in_p = ffn1 + (_row(mix_norm[l]), win_b[:, :2 * c], win_b[:, 2 * c:])
        wout_b = w_out[l].astype(BF16)
        wol, woa = wout_b[:c], wout_b[c:]
        lru_p = (conv_w[l], _row(conv_b[l]), (0.5 * _block_diag(lru_wa[l])).astype(BF16), _row(0.5 * lru_ba[l]),
                 (0.5 * _block_diag(lru_wx[l])).astype(BF16), _row(0.5 * lru_bx[l]), _row(lru_lambda[l]),
                 _row(lru_out_norm[l]))
        reps = V7X_LANES // HEAD_DIM
        qn = _row(jnp.tile(q_norm[l], reps))
        kn = _row(jnp.tile(k_norm[l], reps))
        sk = jnp.broadcast_to(attn_sinks[l].astype(F32)[:, None], (attn_sinks.shape[1], V7X_LANES))
        qn_p = qn * (HEAD_DIM ** -0.5 * log2e)
        sk_rows = jnp.repeat(attn_sinks[l].astype(F32).reshape(N_KV_HEADS, -1), WINDOW, axis=1) * log2e
        a_on = _row(attn_out_norm[l])

        x1, lo, ao, hl, cs, pk, pv = _mixer_in(yp, t, cos_p, sin_p, ffn_in_p, lru_p, (qn_p, kn, sk_rows, ones, a_on))
        yp = _out_ffn(lo, ao, x1, wol, woa, *ffn2)
        outs[0].append(hl.reshape(b, c))
        outs[1].append(cs)
        outs[2].append(pk.reshape(b, WINDOW, N_KV_HEADS, HEAD_DIM))
        outs[3].append(pv.reshape(b, WINDOW, N_KV_HEADS, HEAD_DIM))

        x1s, pl_s, pa_s = _ffn_in(ys, *ffn_in_p)
        x_tm = pl_s.reshape(s, ts, 2 * c).transpose(1, 0, 2)
        cst_tm = state_conv[l].transpose(1, 0, 2)
        lo_tm, hls, cs_tm = _lru_sample(x_tm, cst_tm, state_lru_h[l], *lru_p)
        q_s, k_s = _qk_prep(pa_s, cos_s, sin_s, qn, kn, ones, aw)
        v_s = pa_s[:, aw + kvw:]
        aos, nk, nv = _attn_cached(q_s.reshape(s, ts, aw), k_s.reshape(s, ts, kvw), v_s.reshape(s, ts, kvw),
                                   jnp.transpose(cache_k[l], (0, 2, 3, 1)), jnp.transpose(cache_v[l], (0, 2, 3, 1)),
                                   sk, a_on)
        ys = _out_ffn(lo_tm.transpose(1, 0, 2).reshape(s * ts, c), aos.reshape(s * ts, aw), x1s, wol, woa, *ffn2)
        outs[4].append(hls)
        outs[5].append(cs_tm.transpose(1, 0, 2))
        outs[6].append(jnp.transpose(nk, (0, 3, 1, 2)))
        outs[7].append(jnp.transpose(nv, (0, 3, 1, 2)))

    st = [jnp.stack(o) for o in outs]
    return (yp.reshape(b, t, d), ys.reshape(s, ts, d), st[0], st[1], st[2], st[3], st[4], st[5], st[6], st[7])
```

```python
import functools

import jax
import jax.numpy as jnp
from jax import lax
from jax.experimental import pallas as pl
from jax.experimental.pallas import tpu as pltpu

F32 = jnp.float32
BF16 = jnp.bfloat16

NORM_EPS = 1e-6
LRU_C = 8.0
CONV_W = 4
HEAD_DIM = 64
N_KV_HEADS = 2
WINDOW = 128
ROPE_THETA = 10000.0
PAST_LEN = 16384

V7X_LANES = 128
V7X_SUBLANES = 8
V7X_MXU_DIM = 256
V7X_VMEM_BYTES = 64 * 1024 * 1024

TOKEN_TILE = 512
OUT_FFN_TILE = 1024
SAMPLE_SEQ_TILE = 32
FF_CHUNK = V7X_MXU_DIM
GATE_DIAG_TILE = V7X_MXU_DIM


MIB = 1024 * 1024
VMEM_KEEP_FREE_BYTES = 4 * MIB
FFN_TEMP_BYTES = 8 * MIB
MIXER_TEMP_BYTES = 20 * MIB
SMALL_KERNEL_VMEM_BYTES = 40 * MIB


def _vmem_limit(nbytes):
    return int(min(nbytes, V7X_VMEM_BYTES - VMEM_KEEP_FREE_BYTES))


def _resident(shape):
    nd = len(shape)
    return pl.BlockSpec(shape, lambda *_: (0,) * nd, pipeline_mode=pl.Buffered(1))


def _rms(x, g):
    ms = jnp.mean(x * x, axis=-1, keepdims=True)
    return (x * lax.rsqrt(ms + NORM_EPS)) * g


def _ffn(xn_ref, wg_ref, wu_ref, wd_ref, acc_ref):
    acc_ref[...] = jnp.zeros_like(acc_ref)
    for c0 in range(0, wg_ref.shape[1], FF_CHUNK):
        xn = xn_ref[...]
        h = jnp.dot(xn, wg_ref[:, c0:c0 + FF_CHUNK], preferred_element_type=F32)
        u = jnp.dot(xn, wu_ref[:, c0:c0 + FF_CHUNK], preferred_element_type=F32)
        a = ((h + h * jnp.tanh(h)) * u).astype(BF16)
        acc_ref[...] += jnp.dot(a, wd_ref[c0:c0 + FF_CHUNK, :], preferred_element_type=F32)
    return acc_ref[...]


def _ffn_in_tile(x_ref, ffn_refs, gm_ref, winl_ref, wina_ref, x1_ref, pl_ref, pa_ref, xn_ref, acc_ref):
    g1_ref, wg_ref, wu_ref, wd_ref = ffn_refs
    xn_ref[...] = _rms(x_ref[...], g1_ref[...]).astype(BF16)
    x1 = x_ref[...] + 0.5 * _ffn(xn_ref, wg_ref, wu_ref, wd_ref, acc_ref)
    x1_ref[...] = x1
    xn_ref[...] = _rms(x1, gm_ref[...]).astype(BF16)
    pl_ref[...] = jnp.dot(xn_ref[...], winl_ref[...], preferred_element_type=F32)
    pa_ref[...] = jnp.dot(xn_ref[...], wina_ref[...], preferred_element_type=F32)


def _lru_preact(xc, gate_w):
    wa_ref, ba_ref, wx_ref, bx_ref = gate_w
    xcb = xc.astype(BF16)
    c = xc.shape[-1]
    dt = min(GATE_DIAG_TILE, c)

    def blockdiag_dot(w_ref):
        return jnp.concatenate([jnp.dot(xcb[:, s:s + dt], w_ref[s:s + dt, s:s + dt], preferred_element_type=F32)
                                for s in range(0, c, dt)], axis=1)

    zr = blockdiag_dot(wa_ref) + ba_ref[...]
    zi = blockdiag_dot(wx_ref) + bx_ref[...]
    return zr, zi


def _lru_terms(xc, zr, zi, lam_ref):
    i = 0.5 * jnp.tanh(zi) + 0.5
    half_c = (-0.5 * LRU_C) * jax.nn.softplus(-lam_ref[...])
    log_a = half_c * jnp.tanh(zr) + half_c
    a = jnp.exp(log_a)
    w = -jnp.tanh(log_a) * (a * a + 1.0)
    root = jnp.where(w > 0.0, w * lax.rsqrt(w), 0.0)
    u = root * (i * xc)
    return a, u


def _lru_prep(p_ref, first, lru_refs, xh_ref, xc_ref, zr_ref, zi_ref, gate_ref, ctail_ref):
    cw_ref, cb_ref, wa_ref, ba_ref, wx_ref, bx_ref, _, _ = lru_refs
    tt = p_ref.shape[0]
    c = xc_ref.shape[1]
    hist = V7X_SUBLANES

    xh_ref[0:hist, :] = jnp.where(first, jnp.zeros((hist, c), F32), xh_ref[0:hist, :])
    x = p_ref[:, 0:c]
    xh_ref[hist:hist + tt, :] = x
    xg = xh_ref[...].reshape(tt // hist + 1, hist, c)
    sub = lax.broadcasted_iota(jnp.int32, (tt // hist, hist, c), 1)
    w = cw_ref[...]
    xc = cb_ref[...].reshape(1, 1, c)
    for j in range(CONV_W - 1):
        k = CONV_W - 1 - j
        rolled = pltpu.roll(xg, k, 1)
        xc = xc + jnp.where(sub >= k, rolled[1:], rolled[:-1]) * w[j:j + 1, :].reshape(1, 1, c)
    xc = (xc + xg[1:] * w[CONV_W - 1:CONV_W, :].reshape(1, 1, c)).reshape(tt, c)
    xh_ref[0:hist, :] = x[tt - hist:tt, :]
    ctail_ref[...] = x[tt - (CONV_W - 1):tt, :]

    xc_ref[...] = xc
    zr, zi = _lru_preact(xc, (wa_ref, ba_ref, wx_ref, bx_ref))
    zr_ref[...] = zr
    zi_ref[...] = zi
    gate_ref[...] = p_ref[:, c:2 * c]


def _lru_core(first, lru_refs, xc_ref, zr_ref, zi_ref, gate_ref, ctail_ref, out_ref, hl_ref, cs_ref, h_ref):
    lam_ref, on_ref = lru_refs[6], lru_refs[7]
    tt, c = xc_ref.shape

    a, u = _lru_terms(xc_ref[...], zr_ref[...], zi_ref[...], lam_ref)

    groups = tt // V7X_SUBLANES
    a3 = a.reshape(groups, V7X_SUBLANES, c)
    u3 = u.reshape(groups, V7X_SUBLANES, c)
    sub = lax.broadcasted_iota(jnp.int32, a3.shape, 1)
    k = 1
    while k < V7X_SUBLANES:
        m = sub >= k
        a_prev = pltpu.roll(a3, k, 1)
        u_prev = pltpu.roll(u3, k, 1)
        u3 = jnp.where(m, a3 * u_prev + u3, u3)
        a3 = jnp.where(m, a3 * a_prev, a3)
        k *= 2

    h = jnp.where(first, jnp.zeros((1, c), F32), h_ref[...])
    hs = []
    for g in range(groups):
        hr = a3[g] * h + u3[g]
        hs.append(hr)
        h = hr[V7X_SUBLANES - 1:V7X_SUBLANES, :]
    h_ref[...] = h

    y = jnp.concatenate(hs, axis=0) * jax.nn.gelu(gate_ref[...])
    out_ref[...] = _rms(y, on_ref[...]).astype(out_ref.dtype)
    hl_ref[0] = h
    cs_ref[0] = ctail_ref[...]


def _lru_sample_kernel(x_ref, cst_ref, h0_ref, cw_ref, cb_ref, wa_ref, ba_ref, wx_ref, bx_ref, lam_ref, on_ref,
                       out_ref, hl_ref, cs_ref):
    t = x_ref.shape[0]
    c = out_ref.shape[2]
    w = cw_ref[...]
    rows = [cst_ref[j] for j in range(CONV_W - 1)] + [x_ref[j, :, 0:c] for j in range(t)]
    h = h0_ref[...]
    for s in range(t):
        xc = cb_ref[...]
        for j in range(CONV_W):
            xc = xc + rows[s + j] * w[j:j + 1, :]
        zr, zi = _lru_preact(xc, (wa_ref, ba_ref, wx_ref, bx_ref))
        a, u = _lru_terms(xc, zr, zi, lam_ref)
        h = a * h + u
        y = h * jax.nn.gelu(x_ref[s, :, c:2 * c])
        out_ref[s] = _rms(y, on_ref[...]).astype(out_ref.dtype)
    hl_ref[...] = h
    for j in range(CONV_W - 1):
        cs_ref[j] = rows[t + j]


def _lru_sample(x_tm, cst_tm, h0, cw, cb, wa, ba, wx, bx, lam, on):
    t, s, w2 = x_tm.shape
    c = w2 // 2
    return pl.pallas_call(
        _lru_sample_kernel,
        out_shape=[jax.ShapeDtypeStruct((t, s, c), BF16), jax.ShapeDtypeStruct((s, c), F32),
                   jax.ShapeDtypeStruct((CONV_W - 1, s, c), F32)],
        compiler_params=pltpu.CompilerParams(vmem_limit_bytes=_vmem_limit(SMALL_KERNEL_VMEM_BYTES)),
        name="lru_sample",
    )(x_tm, cst_tm, h0, cw, cb, wa, ba, wx, bx, lam, on)


def _head_norm_rope(xc, gain, cos, sin, ones_ref):
    x2 = xc * xc
    hi = x2.astype(BF16)
    lo = (x2 - hi.astype(F32)).astype(BF16)
    ss = jnp.dot(hi, ones_ref[...], preferred_element_type=F32) + jnp.dot(lo, ones_ref[...], preferred_element_type=F32)
    xn = (xc * lax.rsqrt(ss + NORM_EPS)) * gain
    lane = lax.broadcasted_iota(jnp.int32, xn.shape, xn.ndim - 1)
    first_half = (lane & (HEAD_DIM // 2)) == 0
    ax = xn.ndim - 1
    swapped = jnp.where(first_half, pltpu.roll(xn, V7X_LANES - HEAD_DIM // 2, ax), pltpu.roll(xn, HEAD_DIM // 2, ax))
    return xn * cos + swapped * sin


def _group_lhs(qcols, g, lane_lo, axis):
    zero = jnp.zeros_like(qcols[0])
    parts = [jnp.where(lane_lo, qcols[2 * g], zero), jnp.where(lane_lo, qcols[2 * g + 1], zero),
             jnp.where(lane_lo, zero, qcols[2 * g]), jnp.where(lane_lo, zero, qcols[2 * g + 1])]
    return jnp.concatenate(parts, axis=axis).astype(BF16)


def _attn_prep(pa_ref, cos_ref, sin_ref, attn_refs, qh_ref, k2_ref, vt_ref, kvl_ref):
    qn_ref, kn_ref, _, ones_ref, _ = attn_refs
    tq = pa_ref.shape[0]
    blk = WINDOW
    lanes = V7X_LANES
    ncol = qh_ref.shape[0] // 2
    aw = ncol * lanes

    cos = cos_ref[...]
    sin = sin_ref[...]
    cols = [pa_ref[:, cc * lanes:(cc + 1) * lanes] for cc in range(ncol + 1)]
    sq = [x * x for x in cols]
    hi = [x.astype(BF16) for x in sq]
    lo = [(x - h.astype(F32)).astype(BF16) for x, h in zip(sq, hi)]
    ss = (jnp.dot(jnp.concatenate(hi, axis=0), ones_ref[...], preferred_element_type=F32)
          + jnp.dot(jnp.concatenate(lo, axis=0), ones_ref[...], preferred_element_type=F32))
    lane = lax.broadcasted_iota(jnp.int32, (tq, lanes), 1)
    first_half = (lane & (HEAD_DIM // 2)) == 0
    lane_lo = lane < HEAD_DIM
    rot = []
    for cc, x in enumerate(cols):
        gain = qn_ref[...] if cc < ncol else kn_ref[...]
        xn = (x * lax.rsqrt(ss[cc * tq:(cc + 1) * tq] + NORM_EPS)) * gain
        swapped = jnp.where(first_half, pltpu.roll(xn, lanes - HEAD_DIM // 2, 1), pltpu.roll(xn, HEAD_DIM // 2, 1))
        rot.append(xn * cos + swapped * sin)
    k = rot[ncol]
    v = pa_ref[:, aw + lanes:aw + 2 * lanes]

    zero = jnp.zeros((tq, lanes), F32)
    for cc in range(ncol):
        qh_ref[2 * cc] = jnp.where(lane_lo, rot[cc], zero).astype(BF16)
        qh_ref[2 * cc + 1] = jnp.where(lane_lo, zero, rot[cc]).astype(BF16)

    k_sw = pltpu.roll(k, HEAD_DIM, 1)
    k2_ref[0, blk:blk + tq, :] = jnp.where(lane_lo, k, k_sw).astype(BF16)
    k2_ref[1, blk:blk + tq, :] = jnp.where(lane_lo, k_sw, k).astype(BF16)
    for jb in range(tq // blk):
        vt_ref[:, (jb + 1) * blk:(jb + 2) * blk] = v[jb * blk:(jb + 1) * blk].T.astype(BF16)
    kvl_ref[0] = k[tq - blk:tq]
    kvl_ref[1] = v[tq - blk:tq]


def _attn_core(first, attn_refs, out_ref, pk_ref, pv_ref, qh_ref, k2_ref, vt_ref, kvl_ref, st_ref, pt_ref, rd_ref):
    _, _, sk_ref, _, on_ref = attn_refs
    tq = qh_ref.shape[1]
    blk = WINDOW
    nblk = tq // blk
    heads_per_group = qh_ref.shape[0] // N_KV_HEADS
    pairs = [(jb, g) for jb in range(nblk) for g in range(N_KV_HEADS)]

    for idx, (jb, g) in enumerate(pairs):
        r0 = jb * blk
        lhs = jnp.concatenate([qh_ref[h, r0:r0 + blk, :]
                               for h in range(g * heads_per_group, (g + 1) * heads_per_group)], axis=0)
        st_ref[idx] = lax.dot_general(k2_ref[g, r0:r0 + 2 * blk, :], lhs, (((1,), (1,)), ((), ())),
                                      preferred_element_type=F32)

    ncolq = heads_per_group * blk
    kj = lax.broadcasted_iota(jnp.int32, (blk, ncolq), 0)
    qi = lax.broadcasted_iota(jnp.int32, (blk, ncolq), 1) % blk
    tri = kj <= qi
    for idx, (jb, g) in enumerate(pairs):
        s_prev = st_ref[idx, 0:blk, :]
        if jb == 0:
            s_prev = s_prev + jnp.where(first, jnp.float32(-jnp.inf), jnp.float32(0.0))
        m = jnp.where(tri, st_ref[idx, blk:2 * blk, :], s_prev)
        sk = sk_ref[g:g + 1, :]
        mx = jnp.maximum(jnp.max(m, axis=0, keepdims=True), sk)
        p = jnp.exp2(m - mx)
        den = jnp.sum(p, axis=0, keepdims=True) + jnp.exp2(sk - mx)
        pz = jnp.zeros_like(p)
        pt_ref[idx] = jnp.concatenate([jnp.where(tri, pz, p), jnp.where(tri, p, pz)], axis=0).astype(BF16)
        rd_ref[idx] = 1.0 / den

    for jb in range(nblk):
        r0 = jb * blk
        ocols = []
        for g in range(N_KV_HEADS):
            idx = jb * N_KV_HEADS + g
            vt_g = vt_ref[g * HEAD_DIM:(g + 1) * HEAD_DIM, r0:r0 + 2 * blk]
            og = jnp.dot(vt_g, pt_ref[idx], preferred_element_type=F32) * rd_ref[idx]
            for half in range(heads_per_group // 2):
                pair = jnp.concatenate([og[:, (2 * half) * blk:(2 * half + 1) * blk],
                                        og[:, (2 * half + 1) * blk:(2 * half + 2) * blk]], axis=0)
                ocols.append(pair.T)
        o = jnp.concatenate(ocols, axis=1)
        out_ref[r0:r0 + blk, :] = _rms(o, on_ref[...]).astype(out_ref.dtype)

    k2_ref[:, 0:blk, :] = k2_ref[:, tq:tq + blk, :]
    vt_ref[:, 0:blk] = vt_ref[:, tq:tq + blk]
    pk_ref[0] = kvl_ref[0]
    pv_ref[0] = kvl_ref[1]


N_FFN_IN = 7
N_LRU = 8
N_ATTN = 5


def _mixer_in_kernel(*refs, tiles_per_seq, ntile):
    x_ref, cos_ref, sin_ref = refs[0:3]
    p0 = 3
    g1_ref, wg_ref, wu_ref, wd_ref, gm_ref, winl_ref, wina_ref = refs[p0:p0 + N_FFN_IN]
    lru_refs = refs[p0 + N_FFN_IN:p0 + N_FFN_IN + N_LRU]
    attn_refs = refs[p0 + N_FFN_IN + N_LRU:p0 + N_FFN_IN + N_LRU + N_ATTN]
    o0 = p0 + N_FFN_IN + N_LRU + N_ATTN
    x1_ref, mix_ref, hl_ref, cs_ref, pk_ref, pv_ref = refs[o0:o0 + 6]
    (xn_ref, acc_ref, pl_ref, pa_ref, xh_ref, h_ref, xc_ref, zr_ref, zi_ref, gate_ref, ctail_ref,
     qh_ref, k2_ref, vt_ref, kvl_ref, st_ref, pt_ref, rd_ref) = refs[o0 + 6:]
    c = xc_ref.shape[1]
    lo_ref = mix_ref.at[:, 0:c]
    ao_ref = mix_ref.at[:, c:mix_ref.shape[1]]

    i = pl.program_id(0)

    @pl.when(i == 0)
    def _():
        for ref in (pl_ref, pa_ref, xh_ref, h_ref, xc_ref, zr_ref, zi_ref, gate_ref, ctail_ref,
                    qh_ref, k2_ref, vt_ref, kvl_ref):
            ref[...] = jnp.zeros_like(ref)

    def step(with_ffn):
        first = lax.rem(jnp.maximum(i - 1, 0), tiles_per_seq) == 0
        _attn_prep(pa_ref, cos_ref, sin_ref, attn_refs, qh_ref, k2_ref, vt_ref, kvl_ref)
        _lru_prep(pl_ref, first, lru_refs, xh_ref, xc_ref, zr_ref, zi_ref, gate_ref, ctail_ref)
        _attn_core(first, attn_refs, ao_ref, pk_ref, pv_ref, qh_ref, k2_ref, vt_ref, kvl_ref, st_ref, pt_ref, rd_ref)
        _lru_core(first, lru_refs, xc_ref, zr_ref, zi_ref, gate_ref, ctail_ref, lo_ref, hl_ref, cs_ref, h_ref)
        if with_ffn:
            _ffn_in_tile(x_ref, (g1_ref, wg_ref, wu_ref, wd_ref), gm_ref, winl_ref, wina_ref,
                         x1_ref, pl_ref, pa_ref, xn_ref, acc_ref)

    pl.when(i < ntile)(functools.partial(step, True))
    pl.when(i >= ntile)(functools.partial(step, False))


def _mixer_in(x, seq_len, cos, sin, ffn_in_p, lru_p, attn_p):
    n, d = x.shape
    tm = TOKEN_TILE
    assert seq_len % tm == 0 and n % seq_len == 0 and tm % WINDOW == 0
    nseq = n // seq_len
    tps = seq_len // tm
    ntile = n // tm
    wg, winl, wina = ffn_in_p[1], ffn_in_p[5], ffn_in_p[6]
    w2, wa = winl.shape[1], wina.shape[1]
    c = w2 // 2
    aw = attn_p[4].shape[1]
    kvw = N_KV_HEADS * HEAD_DIM
    npair = tm // WINDOW * N_KV_HEADS
    heads_per_group = aw // HEAD_DIM // N_KV_HEADS
    cur = lambda i: jnp.minimum(i, ntile - 1)
    mix = lambda i: jnp.maximum(i - 1, 0)
    params = list(ffn_in_p) + list(lru_p) + list(attn_p)
    weights = sum(p.size * p.dtype.itemsize for p in params)
    tiles = (2 * 4 * tm * 2 * d + 2 * 2 * tm * (c + aw) + 4 * tm * (w2 + wa) + tm * d * (2 + 4) + 6 * 4 * tm * c
             + npair * 2 * WINDOW * heads_per_group * WINDOW * (4 + 2))
    return pl.pallas_call(
        functools.partial(_mixer_in_kernel, tiles_per_seq=tps, ntile=ntile),
        grid=(ntile + 1,),
        in_specs=[pl.BlockSpec((tm, d), lambda i: (cur(i), 0)),
                  pl.BlockSpec((tm, V7X_LANES), lambda i: (lax.rem(mix(i), tps), 0)),
                  pl.BlockSpec((tm, V7X_LANES), lambda i: (lax.rem(mix(i), tps), 0))]
                 + [_resident(p.shape) for p in params],
        out_specs=[pl.BlockSpec((tm, d), lambda i: (cur(i), 0)),
                   pl.BlockSpec((tm, c + aw), lambda i: (mix(i), 0)),
                   pl.BlockSpec((1, 1, c), lambda i: (mix(i) // tps, 0, 0)),
                   pl.BlockSpec((1, CONV_W - 1, c), lambda i: (mix(i) // tps, 0, 0)),
                   pl.BlockSpec((1, WINDOW, kvw), lambda i: (mix(i) // tps, 0, 0)),
                   pl.BlockSpec((1, WINDOW, kvw), lambda i: (mix(i) // tps, 0, 0))],
        out_shape=[jax.ShapeDtypeStruct((n, d), F32), jax.ShapeDtypeStruct((n, c + aw), BF16),
                   jax.ShapeDtypeStruct((nseq, 1, c), F32),
                   jax.ShapeDtypeStruct((nseq, CONV_W - 1, c), F32), jax.ShapeDtypeStruct((nseq, WINDOW, kvw), F32),
                   jax.ShapeDtypeStruct((nseq, WINDOW, kvw), F32)],
        scratch_shapes=[pltpu.VMEM((tm, d), BF16), pltpu.VMEM((tm, d), F32),
                        pltpu.VMEM((tm, w2), F32), pltpu.VMEM((tm, wa), F32),
                        pltpu.VMEM((tm + V7X_SUBLANES, c), F32), pltpu.VMEM((1, c), F32),
                        pltpu.VMEM((tm, c), F32), pltpu.VMEM((tm, c), F32), pltpu.VMEM((tm, c), F32),
                        pltpu.VMEM((tm, c), F32), pltpu.VMEM((CONV_W - 1, c), F32),
                        pltpu.VMEM((2 * aw // V7X_LANES, tm, V7X_LANES), BF16),
                        pltpu.VMEM((N_KV_HEADS, WINDOW + tm, V7X_LANES), BF16),
                        pltpu.VMEM((V7X_LANES, WINDOW + tm), BF16),
                        pltpu.VMEM((2, WINDOW, kvw), F32),
                        pltpu.VMEM((npair, 2 * WINDOW, heads_per_group * WINDOW), F32),
                        pltpu.VMEM((npair, 2 * WINDOW, heads_per_group * WINDOW), BF16),
                        pltpu.VMEM((npair, 1, heads_per_group * WINDOW), F32)],
        compiler_params=pltpu.CompilerParams(dimension_semantics=("arbitrary",),
                                             vmem_limit_bytes=_vmem_limit(weights + tiles + MIXER_TEMP_BYTES)),
        name="mixer_in",
    )(x, cos, sin, *params)


def _ffn_in_kernel(x_ref, g1_ref, wg_ref, wu_ref, wd_ref, gm_ref, winl_ref, wina_ref,
                   x1_ref, pl_ref, pa_ref, xn_ref, acc_ref):
    _ffn_in_tile(x_ref, (g1_ref, wg_ref, wu_ref, wd_ref), gm_ref, winl_ref, wina_ref,
                 x1_ref, pl_ref, pa_ref, xn_ref, acc_ref)


def _ffn_in(x, g1, wg, wu, wd, gm, winl, wina):
    n, d = x.shape
    tm = min(TOKEN_TILE, n)
    wl, wa = winl.shape[1], wina.shape[1]
    row = lambda w: pl.BlockSpec((tm, w), lambda i: (i, 0))
    weights = 2 * (wg.size + wu.size + wd.size + winl.size + wina.size)
    tiles = 2 * 4 * tm * (2 * d + wl + wa) + tm * d * (2 + 4) + 4 * 4 * tm * max(FF_CHUNK, wl)
    return pl.pallas_call(
        _ffn_in_kernel,
        grid=(n // tm,),
        in_specs=[row(d), _resident(g1.shape), _resident(wg.shape), _resident(wu.shape), _resident(wd.shape),
                  _resident(gm.shape), _resident(winl.shape), _resident(wina.shape)],
        out_specs=[row(d), row(wl), row(wa)],
        out_shape=[jax.ShapeDtypeStruct((n, d), F32), jax.ShapeDtypeStruct((n, wl), F32),
                   jax.ShapeDtypeStruct((n, wa), F32)],
        scratch_shapes=[pltpu.VMEM((tm, d), BF16), pltpu.VMEM((tm, d), F32)],
        compiler_params=pltpu.CompilerParams(dimension_semantics=("arbitrary",),
                                             vmem_limit_bytes=_vmem_limit(weights + tiles + FFN_TEMP_BYTES)),
        name="ffn_in",
    )(x, g1, wg, wu, wd, gm, winl, wina)


def _out_ffn_kernel(mix_ref, x1_ref, wout_ref, g2_ref, wg_ref, wu_ref, wd_ref, out_ref, xn_ref, acc_ref):
    x2 = x1_ref[...] + jnp.dot(mix_ref[...], wout_ref[...], preferred_element_type=F32)
    out_ref[...] = x2
    xn_ref[...] = _rms(x2, g2_ref[...]).astype(BF16)
    out_ref[...] = out_ref[...] + 0.5 * _ffn(xn_ref, wg_ref, wu_ref, wd_ref, acc_ref)


def _out_ffn(mix, x1, wout, g2, wg, wu, wd):
    n, d = x1.shape
    tm = OUT_FFN_TILE if n % OUT_FFN_TILE == 0 else min(TOKEN_TILE, n)
    row = lambda w: pl.BlockSpec((tm, w), lambda i: (i, 0))
    weights = 2 * (wg.size + wu.size + wd.size + wout.size)
    tiles = 2 * tm * (2 * mix.shape[1] + 8 * d) + tm * d * (2 + 4) + 4 * 4 * tm * d
    return pl.pallas_call(
        _out_ffn_kernel,
        grid=(n // tm,),
        in_specs=[row(mix.shape[1]), row(d), _resident(wout.shape),
                  _resident(g2.shape), _resident(wg.shape), _resident(wu.shape), _resident(wd.shape)],
        out_specs=row(d),
        out_shape=jax.ShapeDtypeStruct((n, d), F32),
        scratch_shapes=[pltpu.VMEM((tm, d), BF16), pltpu.VMEM((tm, d), F32)],
        compiler_params=pltpu.CompilerParams(dimension_semantics=("arbitrary",),
                                             vmem_limit_bytes=_vmem_limit(weights + tiles + FFN_TEMP_BYTES)),
        name="out_ffn",
    )(mix, x1, wout, g2, wg, wu, wd)


def _qk_prep_kernel(pa_ref, cos_ref, sin_ref, qn_ref, kn_ref, ones_ref, q_ref, k_ref):
    lanes = V7X_LANES
    aw = q_ref.shape[1]
    cos = cos_ref[...]
    sin = sin_ref[...]
    for cc in range(aw // lanes):
        q_ref[:, cc * lanes:(cc + 1) * lanes] = _head_norm_rope(
            pa_ref[:, cc * lanes:(cc + 1) * lanes], qn_ref[...], cos, sin, ones_ref) * (HEAD_DIM ** -0.5)
    k_ref[...] = _head_norm_rope(pa_ref[:, aw:aw + lanes], kn_ref[...], cos, sin, ones_ref)


def _qk_prep(pa, cos, sin, qn, kn, ones, aw):
    n = pa.shape[0]
    return pl.pallas_call(
        _qk_prep_kernel,
        out_shape=[jax.ShapeDtypeStruct((n, aw), F32), jax.ShapeDtypeStruct((n, N_KV_HEADS * HEAD_DIM), F32)],
        compiler_params=pltpu.CompilerParams(vmem_limit_bytes=_vmem_limit(SMALL_KERNEL_VMEM_BYTES)),
        name="qk_prep_sample",
    )(pa, cos, sin, qn, kn, ones)


def _attn_cached_kernel(q_ref, kn_ref, vn_ref, ck_ref, cv_ref, sk_ref, on_ref,
                        out_ref, nk_ref, nv_ref, kk_ref, vv_ref):
    sb, t, aw = q_ref.shape
    w = ck_ref.shape[3]
    lanes = V7X_LANES
    jpad = kk_ref.shape[1]

    def rows(c_ref):
        return jnp.swapaxes(c_ref[...].reshape(sb, lanes, w), 1, 2)

    def store_window(n_ref, x):
        n_ref[...] = jnp.swapaxes(x, 1, 2).reshape(n_ref.shape)

    kk_ref[:, 0:w, :] = rows(ck_ref)
    vv_ref[:, 0:w, :] = rows(cv_ref)
    kk_ref[:, w:w + t, :] = kn_ref[...]
    vv_ref[:, w:w + t, :] = vn_ref[...]
    kk_ref[:, w + t:jpad, :] = jnp.zeros((sb, jpad - w - t, lanes), F32)
    vv_ref[:, w + t:jpad, :] = jnp.zeros((sb, jpad - w - t, lanes), F32)
    store_window(nk_ref, kk_ref[:, t:t + w, :])
    store_window(nv_ref, vv_ref[:, t:t + w, :])

    kk = kk_ref[...]
    vv = vv_ref[...]
    k_sw = pltpu.roll(kk, HEAD_DIM, 2)
    v_sw = pltpu.roll(vv, HEAD_DIM, 2)
    lane_k = lax.broadcasted_iota(jnp.int32, kk.shape, 2) < HEAD_DIM
    lane_q = lax.broadcasted_iota(jnp.int32, (sb, t, lanes), 2) < HEAD_DIM
    k2 = (jnp.where(lane_k, kk, k_sw).astype(BF16), jnp.where(lane_k, k_sw, kk).astype(BF16))
    v2 = (vv.astype(BF16), v_sw.astype(BF16))

    qcols = [q_ref[:, :, cc * lanes:(cc + 1) * lanes] for cc in range(aw // lanes)]
    tq = lax.broadcasted_iota(jnp.int32, (sb, 4 * t, jpad), 1) % t
    j = lax.broadcasted_iota(jnp.int32, (sb, 4 * t, jpad), 2)
    valid = jnp.logical_and(j > tq, j <= tq + w)
    neg_inf = jnp.float32(-jnp.inf)

    ocols = []
    for g in range(N_KV_HEADS):
        lhs = _group_lhs(qcols, g, lane_q, 1)
        s = jnp.einsum('bqd,bjd->bqj', lhs, k2[g], preferred_element_type=F32)
        s = jnp.where(valid, s, neg_inf)
        heads = (4 * g, 4 * g + 2, 4 * g + 1, 4 * g + 3)
        sk = jnp.concatenate([jnp.broadcast_to(sk_ref[h:h + 1, 0:1].reshape(1, 1, 1), (sb, t, 1)) for h in heads],
                             axis=1)
        mx = jnp.maximum(jnp.max(s, axis=-1, keepdims=True), sk)
        p = jnp.exp(s - mx)
        den = jnp.sum(p, axis=-1, keepdims=True) + jnp.exp(sk - mx)
        pb = p.astype(BF16)
        o_even = jnp.einsum('bqj,bjd->bqd', pb[:, 0:2 * t], v2[g], preferred_element_type=F32) / den[:, 0:2 * t]
        o_odd = jnp.einsum('bqj,bjd->bqd', pb[:, 2 * t:4 * t], v2[1 - g], preferred_element_type=F32) / den[:, 2 * t:4 * t]
        for half in range(2):
            ocols.append(jnp.where(lane_q, o_even[:, half * t:(half + 1) * t], o_odd[:, half * t:(half + 1) * t]))
    o = jnp.concatenate(ocols, axis=2)
    out_ref[...] = _rms(o, on_ref[...].reshape(1, 1, aw)).astype(out_ref.dtype)


def _attn_cached(q3, kn3, vn3, ck, cv, sk, on):
    s, t, aw = q3.shape
    nkv, hd, w = ck.shape[1:]
    kvw = nkv * hd
    sb = min(SAMPLE_SEQ_TILE, s)
    jpad = -(-(w + t) // V7X_SUBLANES) * V7X_SUBLANES
    seq = lambda a, c: pl.BlockSpec((sb, a, c), lambda i: (i, 0, 0))
    win = pl.BlockSpec((sb, nkv, hd, w), lambda i: (i, 0, 0, 0))
    return pl.pallas_call(
        _attn_cached_kernel,
        grid=(s // sb,),
        in_specs=[seq(t, aw), seq(t, kvw), seq(t, kvw), win, win, _resident(sk.shape), _resident(on.shape)],
        out_specs=[seq(t, aw), win, win],
        out_shape=[jax.ShapeDtypeStruct((s, t, aw), BF16), jax.ShapeDtypeStruct((s, nkv, hd, w), F32),
                   jax.ShapeDtypeStruct((s, nkv, hd, w), F32)],
        scratch_shapes=[pltpu.VMEM((sb, jpad, kvw), F32), pltpu.VMEM((sb, jpad, kvw), F32)],
        compiler_params=pltpu.CompilerParams(dimension_semantics=("arbitrary",),
                                             vmem_limit_bytes=_vmem_limit(SMALL_KERNEL_VMEM_BYTES)),
        name="attn_sample",
    )(q3, kn3, vn3, ck, cv, sk, on)


ROPE_SPLIT = 128


def _rope_tables(start, n):
    half = HEAD_DIM // 2
    lane = jnp.arange(V7X_LANES, dtype=jnp.int32)
    inv_freq = ROPE_THETA ** (-(lane % half).astype(F32) / half)
    sign = jnp.where((lane % HEAD_DIM) < half, -1.0, 1.0).astype(F32)
    if n % ROPE_SPLIT:
        pos = start + jnp.arange(n, dtype=jnp.int32)
        ang = pos.astype(F32)[:, None] * inv_freq[None, :]
        return jnp.cos(ang), jnp.sin(ang) * sign[None, :]
    lo = jnp.arange(ROPE_SPLIT, dtype=jnp.int32)
    hi = start + ROPE_SPLIT * jnp.arange(n // ROPE_SPLIT, dtype=jnp.int32)
    a_hi = hi.astype(F32)[:, None] * inv_freq[None, :]
    a_lo = lo.astype(F32)[:, None] * inv_freq[None, :]
    ch, sh = jnp.cos(a_hi)[:, None, :], jnp.sin(a_hi)[:, None, :]
    cl, sl = jnp.cos(a_lo)[None, :, :], jnp.sin(a_lo)[None, :, :]
    cos = (ch * cl - sh * sl).reshape(n, V7X_LANES)
    sin = (sh * cl + ch * sl).reshape(n, V7X_LANES)
    return cos, sin * sign[None, :]


def _block_diag(w):
    n, c, d = w.shape
    eye = jnp.eye(n, dtype=w.dtype)
    return (w[:, :, None, :] * eye[:, None, :, None]).reshape(n * c, n * d)


def _row(v):
    return v.reshape(1, -1).astype(F32)


def kernel(x_prompt, x_sample, state_lru_h, state_conv, cache_k, cache_v, ffn1_norm, ffn1_w_gate, ffn1_w_up, ffn1_w_down, mix_norm, w_in, conv_w, conv_b, lru_wa, lru_ba, lru_wx, lru_bx, lru_lambda, q_norm, k_norm, attn_sinks, lru_out_norm, attn_out_norm, w_out, ffn2_norm, ffn2_w_gate, ffn2_w_up, ffn2_w_down):
    depth = w_in.shape[0]
    b, t, d = x_prompt.shape
    s, ts, _ = x_sample.shape
    c = lru_lambda.shape[1]
    aw = attn_out_norm.shape[1]
    kvw = N_KV_HEADS * HEAD_DIM
    win = cache_k.shape[2]
    assert t >= WINDOW and win == WINDOW
    assert GATE_DIAG_TILE % lru_wa.shape[2] == 0 and c % min(GATE_DIAG_TILE, c) == 0

    cos_p, sin_p = _rope_tables(0, t)
    cos_s, sin_s = _rope_tables(PAST_LEN, ts)
    cos_s, sin_s = jnp.tile(cos_s, (s, 1)), jnp.tile(sin_s, (s, 1))
    ones = _block_diag(jnp.full((V7X_LANES // HEAD_DIM, HEAD_DIM, HEAD_DIM), 1.0 / HEAD_DIM, BF16))
    log2e = 1.0 / jnp.log(jnp.float32(2.0))

    yp = x_prompt.reshape(b * t, d)
    ys = x_sample.reshape(s * ts, d)
    outs = [[] for _ in range(8)]
    for l in range(depth):
        ffn1 = (_row(ffn1_norm[l]), (0.5 * ffn1_w_gate[l]).astype(BF16), ffn1_w_up[l].astype(BF16),
                ffn1_w_down[l].astype(BF16))
        ffn2 = (_row(ffn2_norm[l]), (0.5 * ffn2_w_gate[l]).astype(BF16), ffn2_w_up[l].astype(BF16),
                ffn2_w_down[l].astype(BF16))
        win_b = w_in[l].astype(BF16)
        ffn_in_p = ffn1 + (_row(mix_norm[l]), win_b[:, :2 * c], win_b[:, 2 * c:])
        wout_b = w_out[l].astype(BF16)
        lru_p = (conv_w[l], _row(conv_b[l]), (0.5 * _block_diag(lru_wa[l])).astype(BF16), _row(0.5 * lru_ba[l]),
                 (0.5 * _block_diag(lru_wx[l])).astype(BF16), _row(0.5 * lru_bx[l]), _row(lru_lambda[l]),
                 _row(lru_out_norm[l]))
        reps = V7X_LANES // HEAD_DIM
        qn = _row(jnp.tile(q_norm[l], reps))
        kn = _row(jnp.tile(k_norm[l], reps))
        sk = jnp.broadcast_to(attn_sinks[l].astype(F32)[:, None], (attn_sinks.shape[1], V7X_LANES))
        qn_p = qn * (HEAD_DIM ** -0.5 * log2e)
        sk_rows = jnp.repeat(attn_sinks[l].astype(F32).reshape(N_KV_HEADS, -1), WINDOW, axis=1) * log2e
        a_on = _row(attn_out_norm[l])

        x1, mixed, hl, cs, pk, pv = _mixer_in(yp, t, cos_p, sin_p, ffn_in_p, lru_p, (qn_p, kn, sk_rows, ones, a_on))
        yp = _out_ffn(mixed, x1, wout_b, *ffn2)
        outs[0].append(hl.reshape(b, c))
        outs[1].append(cs)
        outs[2].append(pk.reshape(b, WINDOW, N_KV_HEADS, HEAD_DIM))
        outs[3].append(pv.reshape(b, WINDOW, N_KV_HEADS, HEAD_DIM))

        x1s, pl_s, pa_s = _ffn_in(ys, *ffn_in_p)
        x_tm = pl_s.reshape(s, ts, 2 * c).transpose(1, 0, 2)
        cst_tm = state_conv[l].transpose(1, 0, 2)
        lo_tm, hls, cs_tm = _lru_sample(x_tm, cst_tm, state_lru_h[l], *lru_p)
        q_s, k_s = _qk_prep(pa_s, cos_s, sin_s, qn, kn, ones, aw)
        v_s = pa_s[:, aw + kvw:]
        aos, nk, nv = _attn_cached(q_s.reshape(s, ts, aw), k_s.reshape(s, ts, kvw), v_s.reshape(s, ts, kvw),
                                   jnp.transpose(cache_k[l], (0, 2, 3, 1)), jnp.transpose(cache_v[l], (0, 2, 3, 1)),
                                   sk, a_on)
        mixed_s = jnp.concatenate([lo_tm.transpose(1, 0, 2).reshape(s * ts, c), aos.reshape(s * ts, aw)], axis=1)
        ys = _out_ffn(mixed_s, x1s, wout_b, *ffn2)
        outs[4].append(hls)
        outs[5].append(cs_tm.transpose(1, 0, 2))
        outs[6].append(jnp.transpose(nk, (0, 3, 1, 2)))
        outs[7].append(jnp.transpose(nv, (0, 3, 1, 2)))

    st = [jnp.stack(o) for o in outs]
    return (yp.reshape(b, t, d), ys.reshape(s, ts, d), st[0], st[1], st[2], st[3], st[4], st[5], st[6], st[7])
```
